```python
import math, functools
import jax, jax.numpy as jnp
from jax import lax
import numpy as np

D_MODEL = 1024
BATCH = 1
SEQ = 16384
DEPTH = 2
DEC_BATCH = 128
DEC_SEQ = 1
PAST_LEN = 16384
PAGE_SIZE = 128

HEAD_DIM = 64
EPS = 1e-6
ROPE_THETA = 10000.0
Q_BLOCK = 128
A_HEAD_DIM = 64
A_D_INNER = D_MODEL
A_HEADS = A_D_INNER // A_HEAD_DIM
A_GROUPS = 2
A_STATE = 128
A_CONV_CH = A_D_INNER + 2 * A_GROUPS * A_STATE
SSM_CONV = 4
SSD_CHUNK = 128
WINDOW = 128
B_HEADS = D_MODEL // 128
B_KV_HEADS = 2
C_HEADS = D_MODEL // 256
C_KV_HEADS = 2
D_HEADS = D_MODEL // 128
D_KV_HEADS = 2
IDX_HEADS = 8
IDX_DIM = 64
DSA_TOPK = 256
D_FF = ((8 * D_MODEL // 3 + 127) // 128) * 128
FFN_CONV = 3
N_EVEN = (DEPTH + 1) // 2
N_ODD = DEPTH // 2
EVEN_SIZES = (A_D_INNER, A_CONV_CH, A_HEADS, B_HEADS * HEAD_DIM, B_KV_HEADS * HEAD_DIM, B_KV_HEADS * HEAD_DIM)
ODD_SIZES = (C_HEADS * 2 * HEAD_DIM, C_KV_HEADS * 2 * HEAD_DIM, C_KV_HEADS * 2 * HEAD_DIM,
             D_HEADS * HEAD_DIM, D_KV_HEADS * HEAD_DIM, D_KV_HEADS * HEAD_DIM,
             IDX_HEADS * IDX_DIM, IDX_HEADS, IDX_DIM)
EVEN_MIX = A_D_INNER + B_HEADS * HEAD_DIM
ODD_MIX = C_HEADS * 2 * HEAD_DIM + D_HEADS * HEAD_DIM

kernel_name = 'hybrid_ssd_swa_diff_dsa_convffn_step'

F32 = jnp.float32


def _split(a, sizes):
    cuts = [int(c) for c in np.cumsum(sizes)[:-1]]
    return jnp.split(a, cuts, axis=-1)


def rmsnorm(x, g):
    xf = x.astype(F32)
    y = xf * lax.rsqrt(jnp.mean(xf * xf, axis=-1, keepdims=True) + EPS)
    return (y * g.astype(F32)).astype(x.dtype)


def gated_rmsnorm(y, z, g):
    b, t, d = y.shape
    yz = (y.astype(F32) * jax.nn.silu(z.astype(F32))).reshape(b, t, A_GROUPS, d // A_GROUPS)
    yz = yz * lax.rsqrt(jnp.mean(yz * yz, axis=-1, keepdims=True) + EPS)
    return (yz.reshape(b, t, d) * g.astype(F32)).astype(y.dtype)


def rope(x, pos):
    half = x.shape[-1] // 2
    inv = ROPE_THETA ** (-jnp.arange(half, dtype=F32) / half)
    ang = pos.astype(F32)[:, None] * inv[None, :]
    ang = ang.reshape((1, pos.shape[0]) + (1,) * (x.ndim - 3) + (half,))
    cos, sin = jnp.cos(ang), jnp.sin(ang)
    xf = x.astype(F32)
    x1, x2 = xf[..., :half], xf[..., half:]
    return jnp.concatenate([x1 * cos - x2 * sin, x2 * cos + x1 * sin], axis=-1).astype(x.dtype)


def causal_dwconv(xp, w, bias, t):
    k = w.shape[0]
    return sum(xp[:, j:j + t] * w[j] for j in range(k)) + bias


def _pad_time(a, pad):
    return jnp.pad(a, [(0, 0), (0, pad)] + [(0, 0)] * (a.ndim - 2))


def ssd_scan(x, dt, a_neg, bm, cm, s0):
    b, L, H, P = x.shape
    G, N = bm.shape[2], bm.shape[3]
    HH = H // G
    Q = min(SSD_CHUNK, L)
    nc = -(-L // Q)
    pad = nc * Q - L
    xf = _pad_time(x.astype(F32), pad).reshape(b, nc, Q, G, HH, P)
    dtf = _pad_time(dt.astype(F32), pad).reshape(b, nc, Q, G, HH)
    bf = _pad_time(bm.astype(F32), pad).reshape(b, nc, Q, G, N)
    cf = _pad_time(cm.astype(F32), pad).reshape(b, nc, Q, G, N)
    a = dtf * a_neg.astype(F32).reshape(G, HH)
    a_cs = jnp.cumsum(a, axis=2)
    xdt = xf * dtf[..., None]
    diff = a_cs[:, :, :, None] - a_cs[:, :, None, :]
    causal = jnp.tril(jnp.ones((Q, Q), dtype=bool))[:, :, None, None]
    lmat = jnp.exp(jnp.where(causal, diff, -jnp.inf))
    cb = jnp.einsum('bclgn,bcsgn->bclsg', cf, bf)
    y_diag = jnp.einsum('bclsgh,bcsghp->bclghp', cb[..., None] * lmat, xdt)
    decay_s = jnp.exp(a_cs[:, :, -1:] - a_cs)
    st = jnp.einsum('bcsgn,bcsghp->bcghpn', bf, xdt * decay_s[..., None])
    chunk_decay = jnp.exp(a_cs[:, :, -1])

    def step(s, inp):
        st_c, dec_c = inp
        return s * dec_c[..., None, None] + st_c, s

    s_fin, s_in = lax.scan(step, s0.astype(F32).reshape(b, G, HH, P, N),
                           (jnp.moveaxis(st, 1, 0), jnp.moveaxis(chunk_decay, 1, 0)))
    s_in = jnp.moveaxis(s_in, 0, 1)
    y_off = jnp.einsum('bclgn,bcghpn->bclghp', cf, s_in) * jnp.exp(a_cs)[..., None]
    y = (y_diag + y_off).reshape(b, nc * Q, H, P)[:, :L]
    return y.astype(x.dtype), s_fin.reshape(b, H, P, N).astype(s0.dtype)


def swa_core(q, k, v, q_pos, k_pos, sinks):
    tq, nh, hd = q.shape
    kvh = k.shape[1]
    g = nh // kvh
    s = jnp.einsum('qkgd,tkd->kgqt', q.reshape(tq, kvh, g, hd), k).astype(F32) * (hd ** -0.5)
    dist = q_pos[:, None] - k_pos[None, :]
    ok = (dist >= 0) & (dist < WINDOW) & (k_pos >= 0)[None, :]
    s = jnp.where(ok, s, -jnp.inf)
    sink = sinks.astype(F32).reshape(kvh, g, 1, 1)
    m = jnp.maximum(jnp.max(s, axis=-1, keepdims=True), sink)
    p = jnp.exp(s - m)
    p = p / (jnp.sum(p, axis=-1, keepdims=True) + jnp.exp(sink - m))
    o = jnp.einsum('kgqt,tkd->qkgd', p.astype(v.dtype), v)
    return o.reshape(tq, nh, hd)


def diff_core(q, k, v, q_pos, k_pos, lam):
    tq, nh, _, hd = q.shape
    kvh = k.shape[1]
    g = nh // kvh
    s = jnp.einsum('qkgcd,tkcd->ckgqt', q.reshape(tq, kvh, g, 2, hd), k).astype(F32) * (hd ** -0.5)
    s = jnp.where(k_pos[None, :] <= q_pos[:, None], s, -jnp.inf)
    p = jax.nn.softmax(s, axis=-1)
    w = p[0] - lam * p[1]
    o = jnp.einsum('kgqt,tkd->qkgd', w.astype(v.dtype), v)
    return o.reshape(tq, nh, v.shape[-1])


def dsa_core(q, qi, wi, k, v, ki, q_pos, k_pos, n_sel):
    tq, nh, hd = q.shape
    kvh = k.shape[1]
    g = nh // kvh
    sc = jnp.einsum('qhi,ti->qht', qi, ki).astype(F32) * (qi.shape[-1] ** -0.5)
    score = jnp.einsum('qh,qht->qt', wi.astype(F32), jax.nn.relu(sc))
    ok = k_pos[None, :] <= q_pos[:, None]
    score = jnp.where(ok, score, -jnp.inf)
    _, idx = lax.top_k(score, n_sel)
    valid = jnp.take_along_axis(ok, idx, axis=1)
    ks, vs = k[idx], v[idx]
    s = jnp.einsum('qkgd,qnkd->kgqn', q.reshape(tq, kvh, g, hd), ks).astype(F32) * (hd ** -0.5)
    s = jnp.where(valid, s, -jnp.inf)
    p = jax.nn.softmax(s, axis=-1)
    o = jnp.einsum('kgqn,qnkd->qkgd', p.astype(vs.dtype), vs)
    return o.reshape(tq, nh, hd)


def mixer_even(h, start, conv_st, ssm_st, win_k, win_v, w_in, conv_w, conv_b, dt_bias, a_log,
               d_skip, norm_g, qn_g, kn_g, sinks, w_out):
    b, t, _ = h.shape
    pos = start + jnp.arange(t)
    z, xbc, dt_raw, q, k, v = _split(h @ w_in, EVEN_SIZES)
    xbc_p = jnp.concatenate([conv_st.astype(xbc.dtype), xbc], axis=1)
    xbc_c = jax.nn.silu(causal_dwconv(xbc_p, conv_w, conv_b, t))
    xs, bm, cm = _split(xbc_c, (A_D_INNER, A_GROUPS * A_STATE, A_GROUPS * A_STATE))
    xs = xs.reshape(b, t, A_HEADS, A_HEAD_DIM)
    dt = jax.nn.softplus(dt_raw.astype(F32) + dt_bias.astype(F32))
    a_neg = -jnp.exp(a_log.astype(F32))
    y, ssm_new = ssd_scan(xs, dt, a_neg, bm.reshape(b, t, A_GROUPS, A_STATE),
                          cm.reshape(b, t, A_GROUPS, A_STATE), ssm_st)
    y = y + xs * d_skip[:, None]
    ya = gated_rmsnorm(y.reshape(b, t, A_D_INNER), z, norm_g)
    q = rope(rmsnorm(q.reshape(b, t, B_HEADS, HEAD_DIM), qn_g), pos)
    k = rope(rmsnorm(k.reshape(b, t, B_KV_HEADS, HEAD_DIM), kn_g), pos)
    v = v.reshape(b, t, B_KV_HEADS, HEAD_DIM)
    if win_k is None:
        nb = t // WINDOW
        qb = q.reshape(b, nb, WINDOW, B_HEADS, HEAD_DIM)

        def band(arr):
            ab = arr.reshape(b, nb, WINDOW, B_KV_HEADS, HEAD_DIM)
            prev = jnp.concatenate([jnp.zeros_like(ab[:, :1]), ab[:, :-1]], axis=1)
            return jnp.concatenate([prev, ab], axis=2)

        q_pos = pos.reshape(nb, WINDOW)
        k_pos = q_pos[:, :1] - WINDOW + jnp.arange(2 * WINDOW)[None, :]
        core = jax.vmap(jax.vmap(swa_core, in_axes=(0, 0, 0, 0, 0, None)),
                        in_axes=(0, 0, 0, None, None, None))
        ob = core(qb, band(k), band(v), q_pos, k_pos, sinks)
        wb = min(WINDOW, t)
        new_k, new_v = k[:, t - wb:], v[:, t - wb:]
    else:
        wb = win_k.shape[1]
        kk = jnp.concatenate([win_k.astype(k.dtype), k], axis=1)
        vv = jnp.concatenate([win_v.astype(v.dtype), v], axis=1)
        k_pos = start - wb + jnp.arange(wb + t)
        ob = jax.vmap(swa_core, in_axes=(0, 0, 0, None, None, None))(q, kk, vv, pos, k_pos, sinks)
        new_k, new_v = kk[:, -wb:], vv[:, -wb:]
    y = jnp.concatenate([ya, ob.reshape(b, t, B_HEADS * HEAD_DIM).astype(ya.dtype)], axis=-1) @ w_out
    return y, (xbc_p[:, -(SSM_CONV - 1):], ssm_new, new_k, new_v)


def mixer_odd(h, start, paged, w_in, qn_g, kn_g, lam_p, sub_g, dqn_g, dkn_g, w_out, lam_init):
    b, t, _ = h.shape
    pos = start + jnp.arange(t)
    cq, ck, cv, dq, dk, dv, iq, iw, ik = _split(h @ w_in, ODD_SIZES)
    cq = rope(rmsnorm(cq.reshape(b, t, C_HEADS, 2, HEAD_DIM), qn_g), pos)
    ck = rope(rmsnorm(ck.reshape(b, t, C_KV_HEADS, 2, HEAD_DIM), kn_g), pos)
    cv = cv.reshape(b, t, C_KV_HEADS, 2 * HEAD_DIM)
    dq = rope(rmsnorm(dq.reshape(b, t, D_HEADS, HEAD_DIM), dqn_g), pos)
    dk = rope(rmsnorm(dk.reshape(b, t, D_KV_HEADS, HEAD_DIM), dkn_g), pos)
    dv = dv.reshape(b, t, D_KV_HEADS, HEAD_DIM)
    iq = rope(iq.reshape(b, t, IDX_HEADS, IDX_DIM), pos)
    ik = rope(ik, pos)
    iw = iw * (IDX_HEADS ** -0.5)
    lp = lam_p.astype(F32)
    lam = jnp.exp(jnp.sum(lp[0] * lp[1])) - jnp.exp(jnp.sum(lp[2] * lp[3])) + lam_init
    if paged is None:
        n_sel = min(DSA_TOPK, t // 4)
        nb = t // Q_BLOCK
        blk = lambda a: jnp.swapaxes(a.reshape((b, nb, Q_BLOCK) + a.shape[2:]), 0, 1)
        diff_b = jax.vmap(diff_core, in_axes=(0, 0, 0, None, None, None))
        dsa_b = jax.vmap(functools.partial(dsa_core, n_sel=n_sel), in_axes=(0, 0, 0, 0, 0, 0, None, None))

        def one_block(args):
            cqb, dqb, iqb, iwb, qpos = args
            return (diff_b(cqb, ck, cv, qpos, pos, lam),
                    dsa_b(dqb, iqb, iwb, dk, dv, ik, qpos, pos))

        oc, od = lax.map(one_block, (blk(cq), blk(dq), blk(iq), blk(iw), pos.reshape(nb, Q_BLOCK)))
        unblk = lambda a: jnp.swapaxes(a, 0, 1).reshape((b, t) + a.shape[3:])
        oc, od = unblk(oc), unblk(od)
    else:
        c_k, c_v, d_k, d_v, d_i, table, o = paged
        n_keys = table.shape[1] * PAGE_SIZE + t
        n_sel = min(DSA_TOPK, n_keys // 4)
        k_pos = jnp.arange(n_keys)

        def gather(cache, pt, new):
            past = cache[o, pt]
            past = past.reshape((-1,) + past.shape[2:])
            return jnp.concatenate([past.astype(new.dtype), new], axis=0)

        def one_seq(args):
            pt, cq1, ck1, cv1, dq1, dk1, dv1, iq1, iw1, ik1 = args
            oc1 = diff_core(cq1, gather(c_k, pt, ck1), gather(c_v, pt, cv1), pos, k_pos, lam)
            od1 = dsa_core(dq1, iq1, iw1, gather(d_k, pt, dk1), gather(d_v, pt, dv1),
                           gather(d_i, pt, ik1), pos, k_pos, n_sel)
            return oc1, od1

        oc, od = lax.map(one_seq, (table, cq, ck, cv, dq, dk, dv, iq, iw, ik))
    oc = rmsnorm(oc, sub_g) * (1.0 - lam_init)
    y = jnp.concatenate([oc.reshape(b, t, -1), od.reshape(b, t, -1).astype(oc.dtype)], axis=-1) @ w_out
    return y, (ck, cv, dk, dv, ik)


def conv_ffn(h, conv_st, w_gate, w_up, conv_w, conv_b, w_down):
    t = h.shape[1]
    gp = jnp.concatenate([conv_st.astype(h.dtype), h @ w_gate], axis=1)
    gate = jax.nn.silu(causal_dwconv(gp, conv_w, conv_b, t))
    y = (gate * (h @ w_up)) @ w_down
    return y, gp[:, -(FFN_CONV - 1):]


def setup_inputs(seed: int = 0) -> dict:
    key = jax.random.key(seed)
    keys = iter(jax.random.split(key, 80))

    def nrm(shape, scale=1.0):
        return jax.random.normal(next(keys), shape, F32) * scale

    def gain(shape):
        return 1.0 + 0.02 * jax.random.normal(next(keys), shape, F32)

    n_pages = PAST_LEN // PAGE_SIZE
    n_used = DEC_BATCH * n_pages
    n_pool = n_used + n_used // 4
    wb = min(WINDOW, PAST_LEN)
    page_table = jax.random.permutation(next(keys), n_pool)[:n_used].reshape(DEC_BATCH, n_pages).astype(jnp.int32)
    dt0 = jnp.exp(jax.random.uniform(next(keys), (N_EVEN, A_HEADS), F32, math.log(1e-3), math.log(1e-1)))
    a_dt_bias = dt0 + jnp.log(-jnp.expm1(-dt0))
    a_A_log = jnp.log(jax.random.uniform(next(keys), (N_EVEN, A_HEADS), F32, 1.0, 16.0))
    return {
        'x_prompt': nrm((BATCH, SEQ, D_MODEL)),
        'x_sample': nrm((DEC_BATCH, DEC_SEQ, D_MODEL)),
        'state_ssm': nrm((N_EVEN, DEC_BATCH, A_HEADS, A_HEAD_DIM, A_STATE), 0.1),
        'state_ssm_conv': nrm((N_EVEN, DEC_BATCH, SSM_CONV - 1, A_CONV_CH)),
        'cache_swa_k': nrm((N_EVEN, DEC_BATCH, wb, B_KV_HEADS, HEAD_DIM)),
        'cache_swa_v': nrm((N_EVEN, DEC_BATCH, wb, B_KV_HEADS, HEAD_DIM)),
        'cache_c_k': nrm((N_ODD, n_pool, PAGE_SIZE, C_KV_HEADS, 2, HEAD_DIM)),
        'cache_c_v': nrm((N_ODD, n_pool, PAGE_SIZE, C_KV_HEADS, 2 * HEAD_DIM)),
        'cache_d_k': nrm((N_ODD, n_pool, PAGE_SIZE, D_KV_HEADS, HEAD_DIM)),
        'cache_d_v': nrm((N_ODD, n_pool, PAGE_SIZE, D_KV_HEADS, HEAD_DIM)),
        'cache_d_idx': nrm((N_ODD, n_pool, PAGE_SIZE, IDX_DIM)),
        'state_ffn_conv': nrm((DEPTH, DEC_BATCH, FFN_CONV - 1, D_FF)),
        'page_table': page_table,
        'norm_mix_g': gain((DEPTH, D_MODEL)),
        'norm_ffn_g': gain((DEPTH, D_MODEL)),
        'a_w_in': nrm((N_EVEN, D_MODEL, sum(EVEN_SIZES)), D_MODEL ** -0.5),
        'a_conv_w': nrm((N_EVEN, SSM_CONV, A_CONV_CH), SSM_CONV ** -0.5),
        'a_conv_b': nrm((N_EVEN, A_CONV_CH), 0.01),
        'a_dt_bias': a_dt_bias,
        'a_A_log': a_A_log,
        'a_D': gain((N_EVEN, A_HEADS)),
        'a_norm_g': gain((N_EVEN, A_D_INNER)),
        'b_qn_g': gain((N_EVEN, HEAD_DIM)),
        'b_kn_g': gain((N_EVEN, HEAD_DIM)),
        'b_sinks': nrm((N_EVEN, B_HEADS)),
        'e_w_out': nrm((N_EVEN, EVEN_MIX, D_MODEL), EVEN_MIX ** -0.5),
        'm_w_in': nrm((N_ODD, D_MODEL, sum(ODD_SIZES)), D_MODEL ** -0.5),
        'c_qn_g': gain((N_ODD, HEAD_DIM)),
        'c_kn_g': gain((N_ODD, HEAD_DIM)),
        'c_lam': nrm((N_ODD, 4, HEAD_DIM), 0.1),
        'c_subln_g': gain((N_ODD, 2 * HEAD_DIM)),
        'd_qn_g': gain((N_ODD, HEAD_DIM)),
        'd_kn_g': gain((N_ODD, HEAD_DIM)),
        'm_w_out': nrm((N_ODD, ODD_MIX, D_MODEL), ODD_MIX ** -0.5),
        'ffn_w_gate': nrm((DEPTH, D_MODEL, D_FF), D_MODEL ** -0.5),
        'ffn_w_up': nrm((DEPTH, D_MODEL, D_FF), D_MODEL ** -0.5),
        'ffn_conv_w': nrm((DEPTH, FFN_CONV, D_FF), FFN_CONV ** -0.5),
        'ffn_conv_b': nrm((DEPTH, D_FF), 0.01),
        'ffn_w_down': nrm((DEPTH, D_FF, D_MODEL), D_FF ** -0.5),
    }


def reference(x_prompt, x_sample, state_ssm, state_ssm_conv, cache_swa_k, cache_swa_v,
              cache_c_k, cache_c_v, cache_d_k, cache_d_v, cache_d_idx, state_ffn_conv, page_table,
              norm_mix_g, norm_ffn_g, a_w_in, a_conv_w, a_conv_b, a_dt_bias, a_A_log, a_D, a_norm_g,
              b_qn_g, b_kn_g, b_sinks, e_w_out, m_w_in, c_qn_g, c_kn_g, c_lam, c_subln_g,
              d_qn_g, d_kn_g, m_w_out, ffn_w_gate, ffn_w_up, ffn_conv_w, ffn_conv_b, ffn_w_down):
    xp, xs = x_prompt, x_sample
    bp = xp.shape[0]
    ssm_pl, ssm_sl, cnv_pl, cnv_sl, swk_pl, swk_sl, swv_pl, swv_sl = [], [], [], [], [], [], [], []
    ck_pl, ck_sl, cv_pl, cv_sl, dk_pl, dk_sl, dv_pl, dv_sl, di_pl, di_sl = [], [], [], [], [], [], [], [], [], []
    fc_pl, fc_sl = [], []
    for i in range(DEPTH):
        if i % 2 == 0:
            e = i // 2
            w = (a_w_in[e], a_conv_w[e], a_conv_b[e], a_dt_bias[e], a_A_log[e], a_D[e], a_norm_g[e],
                 b_qn_g[e], b_kn_g[e], b_sinks[e], e_w_out[e])
            zc = jnp.zeros((bp, SSM_CONV - 1, A_CONV_CH), xp.dtype)
            zs = jnp.zeros((bp, A_HEADS, A_HEAD_DIM, A_STATE), xp.dtype)
            yp, (c1, s1, k1, v1) = mixer_even(rmsnorm(xp, norm_mix_g[i]), 0, zc, zs, None, None, *w)
            ys, (c2, s2, k2, v2) = mixer_even(rmsnorm(xs, norm_mix_g[i]), PAST_LEN, state_ssm_conv[e],
                                              state_ssm[e], cache_swa_k[e], cache_swa_v[e], *w)
            cnv_pl.append(c1); ssm_pl.append(s1); swk_pl.append(k1); swv_pl.append(v1)
            cnv_sl.append(c2); ssm_sl.append(s2); swk_sl.append(k2); swv_sl.append(v2)
        else:
            o = i // 2
            lam_init = 0.8 - 0.6 * math.exp(-0.3 * i)
            w = (m_w_in[o], c_qn_g[o], c_kn_g[o], c_lam[o], c_subln_g[o], d_qn_g[o], d_kn_g[o], m_w_out[o])
            yp, (k1, v1, k2, v2, i1) = mixer_odd(rmsnorm(xp, norm_mix_g[i]), 0, None, *w, lam_init)
            ys, (k3, v3, k4, v4, i2) = mixer_odd(
                rmsnorm(xs, norm_mix_g[i]), PAST_LEN,
                (cache_c_k, cache_c_v, cache_d_k, cache_d_v, cache_d_idx, page_table, o), *w, lam_init)
            ck_pl.append(k1); cv_pl.append(v1); dk_pl.append(k2); dv_pl.append(v2); di_pl.append(i1)
            ck_sl.append(k3); cv_sl.append(v3); dk_sl.append(k4); dv_sl.append(v4); di_sl.append(i2)
        xp = xp + yp
        xs = xs + ys
        fw = (ffn_w_gate[i], ffn_w_up[i], ffn_conv_w[i], ffn_conv_b[i], ffn_w_down[i])
        yp, f1 = conv_ffn(rmsnorm(xp, norm_ffn_g[i]), jnp.zeros((bp, FFN_CONV - 1, D_FF), xp.dtype), *fw)
        ys, f2 = conv_ffn(rmsnorm(xs, norm_ffn_g[i]), state_ffn_conv[i], *fw)
        fc_pl.append(f1); fc_sl.append(f2)
        xp = xp + yp
        xs = xs + ys
    ssm_p, ssm_s = jnp.stack(ssm_pl), jnp.stack(ssm_sl)
    cnv_p, cnv_s = jnp.stack(cnv_pl), jnp.stack(cnv_sl)
    swk_p, swk_s = jnp.stack(swk_pl), jnp.stack(swk_sl)
    swv_p, swv_s = jnp.stack(swv_pl), jnp.stack(swv_sl)
    ck_p, ck_s = jnp.stack(ck_pl), jnp.stack(ck_sl)
    cv_p, cv_s = jnp.stack(cv_pl), jnp.stack(cv_sl)
    dk_p, dk_s = jnp.stack(dk_pl), jnp.stack(dk_sl)
    dv_p, dv_s = jnp.stack(dv_pl), jnp.stack(dv_sl)
    di_p, di_s = jnp.stack(di_pl), jnp.stack(di_sl)
    fc_p, fc_s = jnp.stack(fc_pl), jnp.stack(fc_sl)
    return (xp, xs,
            ssm_p, cnv_p, swk_p, swv_p, ck_p, cv_p, dk_p, dv_p, di_p, fc_p,
            ssm_s, cnv_s, swk_s, swv_s, ck_s, cv_s, dk_s, dv_s, di_s, fc_s)
```

```python
import functools
import math

import jax
import jax.numpy as jnp
import numpy as np
from jax import lax
from jax.experimental import pallas as pl
from jax.experimental.pallas import tpu as pltpu

F32 = jnp.float32
BF16 = jnp.bfloat16
I32 = jnp.int32

EPS = 1e-6
ROPE_THETA = 10000.0
HEAD_DIM = 64
LANES = 128
SUBLANES = 8
VMEM_LIMIT = 56 * 1024 * 1024
WINDOW = 128
SSD_CHUNK = 128
PAGE = 128
A_GROUPS = 2
A_HEAD_DIM = 64
A_STATE = 128
INT_MIN = -2147483648


def _cparams(*sem):
    return pltpu.CompilerParams(dimension_semantics=sem, vmem_limit_bytes=VMEM_LIMIT)


def _pick_tile(n, cap):
    best = LANES
    for m in range(1, n // LANES + 1):
        if n % (m * LANES) == 0 and m * LANES <= cap:
            best = m * LANES
    return best


def _row_tile(m, cap):
    t = min(m, cap)
    while m % t:
        t //= 2
    return t


def _split3(x):
    h = x.astype(BF16)
    r = x - h.astype(F32)
    m = r.astype(BF16)
    lo = (r - m.astype(F32)).astype(BF16)
    return h, m, lo


def _dot(a, b):
    return jnp.dot(a, b, preferred_element_type=F32)


def _dot_nt(a, b):
    return lax.dot_general(a, b, (((1,), (1,)), ((), ())), preferred_element_type=F32)


def _dot3(x, w01):
    h, m, lo = _split3(x)
    return _dot(h, w01) + _dot(m, w01) + _dot(lo, w01)


def _dot3_left(w01, x):
    h, m, lo = _split3(x)
    return _dot(w01, h) + _dot(w01, m) + _dot(w01, lo)


def _silu(x):
    return x * (1.0 / (1.0 + jnp.exp(-x)))


def _softplus(x):
    return jnp.maximum(x, 0.0) + jnp.log(1.0 + jnp.exp(-jnp.abs(x)))


def _mm_kernel(*refs, n_lhs, has_norm, has_res):
    xs = refs[:n_lhs]
    pos = n_lhs
    g_ref = None
    if has_norm:
        g_ref = refs[pos]
        pos += 1
    ws = refs[pos:pos + n_lhs]
    pos += n_lhs
    res_ref = None
    if has_res:
        res_ref = refs[pos]
        pos += 1
    o_ref = refs[pos]
    xb = refs[pos + 1:pos + 1 + n_lhs]

    @pl.when(pl.program_id(1) == 0)
    def _():
        for k in range(n_lhs):
            x = xs[k][...]
            if has_norm and k == 0:
                x = x * lax.rsqrt(jnp.mean(x * x, axis=-1, keepdims=True) + EPS) * g_ref[...]
            xb[k][...] = x.astype(BF16)

    acc = _dot(xb[0][...], ws[0][...])
    for k in range(1, n_lhs):
        acc = acc + _dot(xb[k][...], ws[k][...])
    if has_res:
        acc = acc + res_ref[...]
    o_ref[...] = acc


def _mm(xs, ws, *, norm_g=None, res=None, tm_cap=512, tn_cap=1280, name="mm"):
    m = xs[0].shape[0]
    n = ws[0].shape[1]
    tm = _row_tile(m, tm_cap)
    tn = _pick_tile(n, tn_cap)
    n_lhs = len(xs)
    in_specs = [pl.BlockSpec((tm, x.shape[1]), lambda i, j: (i, 0)) for x in xs]
    args = list(xs)
    if norm_g is not None:
        in_specs.append(pl.BlockSpec((1, xs[0].shape[1]), lambda i, j: (0, 0)))
        args.append(norm_g.reshape(1, -1))
    in_specs += [pl.BlockSpec((w.shape[0], tn), lambda i, j: (0, j)) for w in ws]
    args += list(ws)
    if res is not None:
        in_specs.append(pl.BlockSpec((tm, tn), lambda i, j: (i, j)))
        args.append(res)
    return pl.pallas_call(
        functools.partial(_mm_kernel, n_lhs=n_lhs, has_norm=norm_g is not None, has_res=res is not None),
        out_shape=jax.ShapeDtypeStruct((m, n), F32),
        grid=(m // tm, n // tn),
        in_specs=in_specs,
        out_specs=pl.BlockSpec((tm, tn), lambda i, j: (i, j)),
        scratch_shapes=[pltpu.VMEM((tm, x.shape[1]), BF16) for x in xs],
        compiler_params=_cparams("parallel", "arbitrary"),
        name=name,
    )(*args)


def _ffn_kernel(x_ref, halo_ref, g_ref, wg_ref, wu_ref, cw_ref, cb_ref, wd_ref, p0_ref, p1_ref,
                o_ref, gout_ref, xb_ref, hb_ref, gs_ref, acc_ref, *, seq_mode, tm):
    i = pl.program_id(0)
    j = pl.program_id(1)
    nj = pl.num_programs(1)

    def norm(x):
        return (x * lax.rsqrt(jnp.mean(x * x, axis=-1, keepdims=True) + EPS) * g_ref[...]).astype(BF16)

    @pl.when(j == 0)
    def _():
        xb_ref[...] = norm(x_ref[...])
        if seq_mode:
            hb_ref[...] = norm(halo_ref[...])
        acc_ref[...] = jnp.zeros_like(acc_ref)

    g = _dot(xb_ref[...], wg_ref[...])
    u = _dot(xb_ref[...], wu_ref[...])
    cw = cw_ref[...]
    if seq_mode:
        carried = jnp.concatenate([jnp.zeros((SUBLANES - 2, g.shape[1]), F32), p0_ref[...], p1_ref[...]], axis=0)
        prev = jnp.where(i == 0, carried, _dot(hb_ref[...], wg_ref[...]))
        gs_ref[0:SUBLANES, :] = prev
        gs_ref[SUBLANES:, :] = g
        g1 = gs_ref[pl.ds(SUBLANES - 1, tm), :]
        g2 = gs_ref[pl.ds(SUBLANES - 2, tm), :]
        gout_ref[...] = g[tm - SUBLANES:, :]
    else:
        g1 = p1_ref[...]
        g2 = p0_ref[...]
        gout_ref[...] = g
    c = cw[0:1, :] * g2 + cw[1:2, :] * g1 + cw[2:3, :] * g + cb_ref[...]
    act = (_silu(c) * u).astype(BF16)
    acc_ref[...] += _dot(act, wd_ref[...])

    @pl.when(j == nj - 1)
    def _():
        o_ref[...] = x_ref[...] + acc_ref[...]


def _ffn(x, prev0, prev1, norm_g, wg, wu, conv_w, conv_b, wd, *, seq_mode):
    m, d = x.shape
    f = wg.shape[1]
    tm = _row_tile(m, 1024 if seq_mode else 128)
    tn = _pick_tile(f, 1408)
    ni, nj = m // tm, f // tn
    hb = tm // SUBLANES
    if seq_mode:
        prev_spec = pl.BlockSpec((1, tn), lambda i, j: (0, j))
        gout_rows, gout_shape = SUBLANES, (ni * SUBLANES, f)
    else:
        prev_spec = pl.BlockSpec((tm, tn), lambda i, j: (i, j))
        gout_rows, gout_shape = tm, (m, f)
    out, gout = pl.pallas_call(
        functools.partial(_ffn_kernel, seq_mode=seq_mode, tm=tm),
        out_shape=(jax.ShapeDtypeStruct((m, d), F32), jax.ShapeDtypeStruct(gout_shape, F32)),
        grid=(ni, nj),
        in_specs=[
            pl.BlockSpec((tm, d), lambda i, j: (i, 0)),
            pl.BlockSpec((SUBLANES, d), lambda i, j: (jnp.maximum(i * hb - 1, 0), 0)),
            pl.BlockSpec((1, d), lambda i, j: (0, 0)),
            pl.BlockSpec((d, tn), lambda i, j: (0, j)),
            pl.BlockSpec((d, tn), lambda i, j: (0, j)),
            pl.BlockSpec((conv_w.shape[0], tn), lambda i, j: (0, j)),
            pl.BlockSpec((1, tn), lambda i, j: (0, j)),
            pl.BlockSpec((tn, d), lambda i, j: (j, 0)),
            prev_spec, prev_spec,
        ],
        out_specs=(pl.BlockSpec((tm, d), lambda i, j: (i, 0)),
                   pl.BlockSpec((gout_rows, tn), lambda i, j: (i, j))),
        scratch_shapes=[pltpu.VMEM((tm, d), BF16), pltpu.VMEM((SUBLANES, d), BF16),
                        pltpu.VMEM((tm + SUBLANES, tn), F32), pltpu.VMEM((tm, d), F32)],
        compiler_params=_cparams("parallel", "arbitrary"),
        name="conv_ffn",
    )(x, x, norm_g.reshape(1, -1), wg, wu, conv_w, conv_b.reshape(1, -1), wd, prev0, prev1)
    return out, gout


def _conv_seq_kernel(x_ref, halo_ref, st_ref, w_ref, b_ref, o_ref, xs_ref, *, tm, kw):
    i = pl.program_id(0)
    prev = jnp.where(i == 0, st_ref[...], halo_ref[...])
    xs_ref[0:SUBLANES, :] = prev
    xs_ref[SUBLANES:, :] = x_ref[...]
    w = w_ref[...]
    acc = b_ref[...] + w[kw - 1:kw, :] * x_ref[...]
    for t in range(1, kw):
        acc = acc + w[kw - 1 - t:kw - t, :] * xs_ref[pl.ds(SUBLANES - t, tm), :]
    o_ref[...] = _silu(acc)


def _conv_seq(src, col0, width, state8, w, b):
    m = src.shape[0]
    kw = w.shape[0]
    tm = _row_tile(m, 512)
    tn = _pick_tile(math.gcd(width, col0) if col0 else width, 512)
    cb = col0 // tn
    hb = tm // SUBLANES
    return pl.pallas_call(
        functools.partial(_conv_seq_kernel, tm=tm, kw=kw),
        out_shape=jax.ShapeDtypeStruct((m, width), F32),
        grid=(m // tm, width // tn),
        in_specs=[
            pl.BlockSpec((tm, tn), lambda i, j: (i, cb + j)),
            pl.BlockSpec((SUBLANES, tn), lambda i, j: (jnp.maximum(i * hb - 1, 0), cb + j)),
            pl.BlockSpec((SUBLANES, tn), lambda i, j: (0, j)),
            pl.BlockSpec((kw, tn), lambda i, j: (0, j)),
            pl.BlockSpec((1, tn), lambda i, j: (0, j)),
        ],
        out_specs=pl.BlockSpec((tm, tn), lambda i, j: (i, j)),
        scratch_shapes=[pltpu.VMEM((tm + SUBLANES, tn), F32)],
        compiler_params=_cparams("parallel", "parallel"),
        name="ssm_conv_seq",
    )(src, src, state8, w, b.reshape(1, -1))


def _conv_step_kernel(x_ref, s_ref, w_ref, b_ref, o_ref, *, kw):
    w = w_ref[...]
    acc = b_ref[...] + w[kw - 1:kw, :] * x_ref[...]
    for t in range(kw - 1):
        acc = acc + w[t:t + 1, :] * s_ref[t]
    o_ref[...] = _silu(acc)


def _conv_step(src, col0, width, state, w, b):
    m = src.shape[0]
    kw = w.shape[0]
    tn = _pick_tile(math.gcd(width, col0) if col0 else width, 512)
    cb = col0 // tn
    return pl.pallas_call(
        functools.partial(_conv_step_kernel, kw=kw),
        out_shape=jax.ShapeDtypeStruct((m, width), F32),
        grid=(width // tn,),
        in_specs=[
            pl.BlockSpec((m, tn), lambda j: (0, cb + j)),
            pl.BlockSpec((kw - 1, m, tn), lambda j: (0, 0, j)),
            pl.BlockSpec((kw, tn), lambda j: (0, j)),
            pl.BlockSpec((1, tn), lambda j: (0, j)),
        ],
        out_specs=pl.BlockSpec((m, tn), lambda j: (0, j)),
        compiler_params=_cparams("parallel"),
        name="ssm_conv_step",
    )(src, state, w, b.reshape(1, -1))


def _seg_ones(seg):
    r = lax.broadcasted_iota(I32, (LANES, LANES), 0) // seg
    c = lax.broadcasted_iota(I32, (LANES, LANES), 1) // seg
    return (r == c).astype(BF16)


def _rope128(x, cos, sin_signed):
    lane = lax.broadcasted_iota(I32, x.shape, 1)
    rot = jnp.where(lane % HEAD_DIM < HEAD_DIM // 2,
                    pltpu.roll(x, LANES - HEAD_DIM // 2, 1), pltpu.roll(x, HEAD_DIM // 2, 1))
    return x * cos + rot * sin_signed


def _norm_rope_kernel(x_ref, g_ref, cos_ref, sin_ref, o_ref, ob_ref, *, do_norm, width):
    cos = cos_ref[...]
    sin = sin_ref[...]
    ones = _seg_ones(HEAD_DIM)
    for c in range(width // LANES):
        x = x_ref[:, c * LANES:(c + 1) * LANES]
        if do_norm:
            ms = _dot3(x * x, ones) * (1.0 / HEAD_DIM)
            x = x * lax.rsqrt(ms + EPS) * g_ref[...]
        y = _rope128(x, cos, sin)
        o_ref[:, c * LANES:(c + 1) * LANES] = y
        ob_ref[:, c * LANES:(c + 1) * LANES] = y.astype(BF16)


def _norm_rope(src, col0, width, gain, cos, sin, *, name):
    m = src.shape[0]
    tm = _row_tile(m, 512)
    assert col0 % width == 0
    cb = col0 // width
    g = jnp.ones((1, LANES), F32) if gain is None else jnp.tile(gain.reshape(1, HEAD_DIM), (1, LANES // HEAD_DIM))
    return pl.pallas_call(
        functools.partial(_norm_rope_kernel, do_norm=gain is not None, width=width),
        out_shape=(jax.ShapeDtypeStruct((m, width), F32), jax.ShapeDtypeStruct((m, width), BF16)),
        grid=(m // tm,),
        in_specs=[
            pl.BlockSpec((tm, width), lambda i: (i, cb)),
            pl.BlockSpec((1, LANES), lambda i: (0, 0)),
            pl.BlockSpec((tm, LANES), lambda i: (i, 0)),
            pl.BlockSpec((tm, LANES), lambda i: (i, 0)),
        ],
        out_specs=(pl.BlockSpec((tm, width), lambda i: (i, 0)), pl.BlockSpec((tm, width), lambda i: (i, 0))),
        compiler_params=_cparams("parallel"),
        name=name,
    )(src, g, cos, sin)


def _rope_tables(pos):
    half = HEAD_DIM // 2
    inv = ROPE_THETA ** (-jnp.arange(half, dtype=F32) / half)
    ang = pos.astype(F32)[:, None] * inv[None, :]
    cos, sin = jnp.cos(ang), jnp.sin(ang)
    cos128 = jnp.tile(jnp.concatenate([cos, cos], axis=1), (1, LANES // HEAD_DIM))
    sin128 = jnp.tile(jnp.concatenate([-sin, sin], axis=1), (1, LANES // HEAD_DIM))
    return cos128, sin128


def _head_expand(n_heads_pad, width):
    r = lax.broadcasted_iota(I32, (n_heads_pad, width), 0)
    c = lax.broadcasted_iota(I32, (n_heads_pad, width), 1) // A_HEAD_DIM
    return (r == c).astype(BF16)


def _gated_norm(y, xs, z, dskip, gain):
    yz = (y + xs * dskip) * _silu(z)
    gw = yz.shape[1] // A_GROUPS
    parts = []
    for g in range(A_GROUPS):
        p = yz[:, g * gw:(g + 1) * gw]
        parts.append(p * lax.rsqrt(jnp.mean(p * p, axis=-1, keepdims=True) + EPS))
    return jnp.concatenate(parts, axis=1) * gain


def _ssd_seq_kernel(xs_ref, b_ref, c_ref, dt_ref, z_ref, dtb_ref, alog_ref, dskip_ref, gain_ref, s0_ref,
                    ya_ref, sout_ref, st_ref, y_ref, *, n_heads):
    ci = pl.program_id(0)
    q = SSD_CHUNK
    d_inner = xs_ref.shape[1]
    hpg = n_heads // A_GROUPS
    gw = d_inner // A_GROUPS

    @pl.when(ci == 0)
    def _():
        st_ref[...] = s0_ref[...]

    xs = xs_ref[...]
    dt = _softplus(dt_ref[...] + dtb_ref[...])
    a = dt * (-jnp.exp(alog_ref[...]))
    row = lax.broadcasted_iota(I32, (q, q), 0)
    col = lax.broadcasted_iota(I32, (q, q), 1)
    causal = col <= row
    tri = causal.astype(BF16)
    tri_t = (row <= col).astype(BF16)
    acs = _dot3_left(tri, a)
    acs_t = _dot3(a.T, tri_t)
    expand = _head_expand(LANES, d_inner)
    acs_x = _dot3(acs, expand)
    dt_x = _dot3(dt, expand)
    e_acs = jnp.exp(acs_x)
    last = acs_x[q - 1:q, :]
    decay_s = jnp.exp(last - acs_x)
    xdt = xs * dt_x
    xdt_b = xdt.astype(BF16)
    xdec_b = (xdt * decay_s).astype(BF16)
    chunk_decay = e_acs[q - 1:q, :]

    for g in range(A_GROUPS):
        bg = b_ref[:, g * A_STATE:(g + 1) * A_STATE]
        cg = c_ref[:, g * A_STATE:(g + 1) * A_STATE].astype(BF16)
        cb = _dot_nt(cg, bg.astype(BF16))
        st_g = st_ref[g]
        y_off = _dot(cg, st_g.astype(BF16)) * e_acs[:, g * gw:(g + 1) * gw]
        for hh in range(hpg):
            h = g * hpg + hh
            diff = acs[:, h:h + 1] - acs_t[h:h + 1, :]
            m = (cb * jnp.where(causal, jnp.exp(diff), 0.0)).astype(BF16)
            lo = h * A_HEAD_DIM
            y_ref[:, lo:lo + A_HEAD_DIM] = (_dot(m, xdt_b[:, lo:lo + A_HEAD_DIM])
                                            + y_off[:, hh * A_HEAD_DIM:(hh + 1) * A_HEAD_DIM])
        st_ref[g] = st_g * chunk_decay[:, g * gw:(g + 1) * gw] + _dot(bg.T.astype(BF16), xdec_b[:, g * gw:(g + 1) * gw])

    ya_ref[...] = _gated_norm(y_ref[...], xs, z_ref[...], dskip_ref[...], gain_ref[...])

    @pl.when(ci == pl.num_programs(0) - 1)
    def _():
        sout_ref[...] = st_ref[...]


def _ssd_seq(xbc, proj, z_col, dt_col, dt_bias, a_log, d_skip, gain, s0_t, n_heads):
    t = xbc.shape[0]
    q = SSD_CHUNK
    d_inner = n_heads * A_HEAD_DIM
    gn = A_GROUPS * A_STATE
    bcol = d_inner // gn
    pad = lambda v: jnp.pad(v.reshape(1, -1), ((0, 0), (0, LANES - v.shape[-1])))
    dskip_x = jnp.repeat(d_skip, A_HEAD_DIM).reshape(1, d_inner)
    const = lambda shape: pl.BlockSpec(shape, lambda c: (0,) * len(shape))
    return pl.pallas_call(
        functools.partial(_ssd_seq_kernel, n_heads=n_heads),
        out_shape=(jax.ShapeDtypeStruct((t, d_inner), F32), jax.ShapeDtypeStruct(s0_t.shape, F32)),
        grid=(t // q,),
        in_specs=[
            pl.BlockSpec((q, d_inner), lambda c: (c, 0)),
            pl.BlockSpec((q, gn), lambda c: (c, bcol)),
            pl.BlockSpec((q, gn), lambda c: (c, bcol + 1)),
            pl.BlockSpec((q, LANES), lambda c: (c, dt_col // LANES)),
            pl.BlockSpec((q, d_inner), lambda c: (c, z_col // d_inner)),
            const((1, LANES)), const((1, LANES)), const((1, d_inner)), const((1, d_inner)),
            const(s0_t.shape),
        ],
        out_specs=(pl.BlockSpec((q, d_inner), lambda c: (c, 0)), const(s0_t.shape)),
        scratch_shapes=[pltpu.VMEM(s0_t.shape, F32), pltpu.VMEM((q, d_inner), F32)],
        compiler_params=_cparams("arbitrary"),
        name="ssd_seq",
    )(xbc, xbc, xbc, proj, proj, pad(dt_bias), pad(a_log), dskip_x, gain.reshape(1, -1), s0_t)


def _ssd_step_kernel(s_ref, xdt_t_ref, dec_t_ref, b_ref, c_ref, sout_ref, y_ref, *, n_heads):
    b = pl.program_id(0)
    nb = xdt_t_ref.shape[1]
    hp = n_heads * A_HEAD_DIM
    gw = hp // A_GROUPS
    lane = lax.broadcasted_iota(I32, (hp, nb), 1)
    dec = jnp.sum(jnp.where(lane == b, dec_t_ref[...], 0.0), axis=1, keepdims=True)
    rows = lax.broadcasted_iota(I32, (nb, A_STATE), 0)
    s = s_ref[0].reshape(hp, A_STATE)
    xdt_t = xdt_t_ref[...].astype(BF16)
    outs = []
    for g in range(A_GROUPS):
        brow = b_ref[0, :, g * A_STATE:(g + 1) * A_STATE]
        zb = jnp.where(rows == b, jnp.broadcast_to(brow, (nb, A_STATE)), 0.0).astype(BF16)
        upd = _dot(xdt_t[g * gw:(g + 1) * gw, :], zb)
        sn = s[g * gw:(g + 1) * gw, :] * dec[g * gw:(g + 1) * gw, :] + upd
        sout_ref[0, g * (n_heads // A_GROUPS):(g + 1) * (n_heads // A_GROUPS)] = sn.reshape(
            n_heads // A_GROUPS, A_HEAD_DIM, A_STATE)
        crow = c_ref[0, :, g * A_STATE:(g + 1) * A_STATE]
        c8 = jnp.broadcast_to(crow, (SUBLANES, A_STATE)).astype(BF16)
        outs.append(_dot_nt(c8, sn.astype(BF16))[0:1, :])
    y_ref[0] = jnp.concatenate(outs, axis=1)


def _ssd_step(state, xdt_t, dec_t, bmat, cmat):
    nb, n_heads, p, n = state.shape
    hp = n_heads * p
    new_state, y = pl.pallas_call(
        functools.partial(_ssd_step_kernel, n_heads=n_heads),
        out_shape=(jax.ShapeDtypeStruct(state.shape, F32), jax.ShapeDtypeStruct((nb, 1, hp), F32)),
        grid=(nb,),
        in_specs=[
            pl.BlockSpec((1, n_heads, p, n), lambda b: (b, 0, 0, 0)),
            pl.BlockSpec((hp, nb), lambda b: (0, 0)),
            pl.BlockSpec((hp, nb), lambda b: (0, 0)),
            pl.BlockSpec((1, 1, bmat.shape[1]), lambda b: (b, 0, 0)),
            pl.BlockSpec((1, 1, cmat.shape[1]), lambda b: (b, 0, 0)),
        ],
        out_specs=(pl.BlockSpec((1, n_heads, p, n), lambda b: (b, 0, 0, 0)),
                   pl.BlockSpec((1, 1, hp), lambda b: (b, 0, 0))),
        compiler_params=_cparams("arbitrary"),
        name="ssd_step",
    )(state, xdt_t, dec_t, bmat[:, None, :], cmat[:, None, :])
    return new_state, y.reshape(nb, hp)


def _ssd_step_pre_kernel(xs_ref, dt_ref, dtb_ref, alog_ref, xdt_ref, dec_ref):
    dt = _softplus(dt_ref[...] + dtb_ref[...])
    expand = _head_expand(LANES, xs_ref.shape[1])
    xdt_ref[...] = xs_ref[...] * _dot3(dt, expand)
    dec_ref[...] = jnp.exp(_dot3(dt * (-jnp.exp(alog_ref[...])), expand))


def _ssd_step_pre(xbc, proj, dt_col, dt_bias, a_log, d_inner):
    m = xbc.shape[0]
    pad = lambda v: jnp.pad(v.reshape(1, -1), ((0, 0), (0, LANES - v.shape[-1])))
    return pl.pallas_call(
        _ssd_step_pre_kernel,
        out_shape=(jax.ShapeDtypeStruct((m, d_inner), F32), jax.ShapeDtypeStruct((m, d_inner), F32)),
        grid=(1,),
        in_specs=[pl.BlockSpec((m, d_inner), lambda i: (0, 0)),
                  pl.BlockSpec((m, LANES), lambda i: (0, dt_col // LANES)),
                  pl.BlockSpec((1, LANES), lambda i: (0, 0)), pl.BlockSpec((1, LANES), lambda i: (0, 0))],
        out_specs=(pl.BlockSpec((m, d_inner), lambda i: (0, 0)), pl.BlockSpec((m, d_inner), lambda i: (0, 0))),
        compiler_params=_cparams("arbitrary"),
        name="ssd_step_pre",
    )(xbc, proj, pad(dt_bias), pad(a_log))


def _gated_norm_kernel(y_ref, xs_ref, z_ref, dskip_ref, gain_ref, o_ref):
    o_ref[...] = _gated_norm(y_ref[...], xs_ref[...], z_ref[...], dskip_ref[...], gain_ref[...])


def _gated_norm_call(y, xbc, proj, z_col, d_skip, gain):
    m, d_inner = y.shape
    dskip_x = jnp.repeat(d_skip, A_HEAD_DIM).reshape(1, d_inner)
    return pl.pallas_call(
        _gated_norm_kernel,
        out_shape=jax.ShapeDtypeStruct((m, d_inner), F32),
        grid=(1,),
        in_specs=[pl.BlockSpec((m, d_inner), lambda i: (0, 0)),
                  pl.BlockSpec((m, d_inner), lambda i: (0, 0)),
                  pl.BlockSpec((m, d_inner), lambda i: (0, z_col // d_inner)),
                  pl.BlockSpec((1, d_inner), lambda i: (0, 0)), pl.BlockSpec((1, d_inner), lambda i: (0, 0))],
        out_specs=pl.BlockSpec((m, d_inner), lambda i: (0, 0)),
        compiler_params=_cparams("arbitrary"),
        name="gated_norm",
    )(y, xbc, proj, dskip_x, gain.reshape(1, -1))


def _swa_seq_kernel(sink_ref, q_ref, kc_ref, kp_ref, vc_ref, vp_ref, o_ref, *, n_heads, n_kv):
    i = pl.program_id(0)
    w = WINDOW
    grp = n_heads // n_kv
    r = lax.broadcasted_iota(I32, (grp * w, 2 * w), 0) % w
    c = lax.broadcasted_iota(I32, (grp * w, 2 * w), 1)
    ok = (c > r) & (c <= r + w) & ((i > 0) | (c >= w))
    hrow = lax.broadcasted_iota(I32, (grp * w, 1), 0) // w
    scale = HEAD_DIM ** -0.5
    for kv in range(n_kv):
        sl = slice(kv * HEAD_DIM, (kv + 1) * HEAD_DIM)
        kcat = jnp.concatenate([kp_ref[:, sl], kc_ref[:, sl]], axis=0)
        vcat = jnp.concatenate([vp_ref[:, sl], vc_ref[:, sl]], axis=0)
        q4 = jnp.concatenate([q_ref[:, (kv * grp + j) * HEAD_DIM:(kv * grp + j + 1) * HEAD_DIM]
                              for j in range(grp)], axis=0)
        sink = jnp.zeros((grp * w, 1), F32)
        for j in range(grp):
            sink = jnp.where(hrow == j, sink_ref[kv * grp + j], sink)
        s = jnp.where(ok, _dot_nt(q4, kcat) * scale, -jnp.inf)
        m = jnp.maximum(jnp.max(s, axis=1, keepdims=True), sink)
        p = jnp.exp(s - m)
        denom = jnp.sum(p, axis=1, keepdims=True) + jnp.exp(sink - m)
        o = _dot(p.astype(BF16), vcat) / denom
        for j in range(grp):
            h = kv * grp + j
            o_ref[:, h * HEAD_DIM:(h + 1) * HEAD_DIM] = o[j * w:(j + 1) * w, :]


def _swa_seq(q_b, k_b, v_b, sinks, n_heads, n_kv):
    t = q_b.shape[0]
    w = WINDOW
    kvw = n_kv * HEAD_DIM
    cur = lambda i: (i, 0)
    prv = lambda i: (jnp.maximum(i - 1, 0), 0)
    return pl.pallas_call(
        functools.partial(_swa_seq_kernel, n_heads=n_heads, n_kv=n_kv),
        out_shape=jax.ShapeDtypeStruct((t, n_heads * HEAD_DIM), F32),
        grid=(t // w,),
        in_specs=[pl.BlockSpec(memory_space=pltpu.SMEM),
                  pl.BlockSpec((w, n_heads * HEAD_DIM), cur),
                  pl.BlockSpec((w, kvw), cur), pl.BlockSpec((w, kvw), prv),
                  pl.BlockSpec((w, kvw), cur), pl.BlockSpec((w, kvw), prv)],
        out_specs=pl.BlockSpec((w, n_heads * HEAD_DIM), cur),
        compiler_params=_cparams("parallel"),
        name="swa_seq",
    )(sinks, q_b, k_b, k_b, v_b, v_b)


def _swa_step_kernel(sink_ref, qm_ref, knew_ref, vnew_ref, kc_ref, vc_ref, o_ref, ko_ref, vo_ref,
                     *, n_heads, n_kv, bs):
    grp = n_heads // n_kv
    wb = kc_ref.shape[1]
    scale = HEAD_DIM ** -0.5
    hrow = lax.broadcasted_iota(I32, (n_heads, 1), 0)
    sink = jnp.zeros((n_heads, 1), F32)
    for h in range(n_heads):
        sink = jnp.where(hrow == h, sink_ref[h], sink)
    col = lax.broadcasted_iota(I32, (n_heads, wb), 1)
    ok = col > wb - WINDOW
    for bi in range(bs):
        qm = qm_ref[bi]
        kc = kc_ref[bi]
        vc = vc_ref[bi]
        knew = knew_ref[bi]
        vnew = vnew_ref[bi]
        s = jnp.where(ok, _dot_nt(qm.astype(BF16), kc.astype(BF16)) * scale, -jnp.inf)
        s_new = jnp.sum(qm * knew, axis=1, keepdims=True) * scale
        m = jnp.maximum(jnp.maximum(jnp.max(s, axis=1, keepdims=True), s_new), sink)
        p = jnp.exp(s - m)
        p_new = jnp.exp(s_new - m)
        denom = jnp.sum(p, axis=1, keepdims=True) + p_new + jnp.exp(sink - m)
        o = (_dot(p.astype(BF16), vc.astype(BF16)) + p_new * vnew) / denom
        osel = o[:, 0:HEAD_DIM]
        for kv in range(1, n_kv):
            osel = jnp.where(hrow // grp == kv, o[:, kv * HEAD_DIM:(kv + 1) * HEAD_DIM], osel)
        o_ref[bi] = osel
        ko_ref[bi, 0:wb - 1, :] = kc_ref[bi, 1:wb, :]
        ko_ref[bi, wb - 1:wb, :] = knew
        vo_ref[bi, 0:wb - 1, :] = vc_ref[bi, 1:wb, :]
        vo_ref[bi, wb - 1:wb, :] = vnew


def _swa_step(qm, knew, vnew, cache_k, cache_v, sinks, n_heads, n_kv):
    nb, wb, kvw = cache_k.shape
    bs = 8 if nb % 8 == 0 else 1
    blk = lambda shape: pl.BlockSpec((bs,) + shape, lambda b: (b,) + (0,) * len(shape))
    return pl.pallas_call(
        functools.partial(_swa_step_kernel, n_heads=n_heads, n_kv=n_kv, bs=bs),
        out_shape=(jax.ShapeDtypeStruct((nb, n_heads, HEAD_DIM), F32),
                   jax.ShapeDtypeStruct(cache_k.shape, F32), jax.ShapeDtypeStruct(cache_v.shape, F32)),
        grid=(nb // bs,),
        in_specs=[pl.BlockSpec(memory_space=pltpu.SMEM), blk((n_heads, kvw)), blk((1, kvw)), blk((1, kvw)),
                  blk((wb, kvw)), blk((wb, kvw))],
        out_specs=(blk((n_heads, HEAD_DIM)), blk((wb, kvw)), blk((wb, kvw))),
        compiler_params=_cparams("parallel"),
        name="swa_step",
    )(sinks, qm, knew, vnew, cache_k, cache_v)


def _lambda(lam_ref, lam_init):
    lp = lam_ref[...]
    return (jnp.exp(jnp.sum(lp[0:1, :] * lp[1:2, :], axis=1, keepdims=True))
            - jnp.exp(jnp.sum(lp[2:3, :] * lp[3:4, :], axis=1, keepdims=True)) + lam_init)


def _subln(o, subg, lam_init):
    return o * lax.rsqrt(jnp.mean(o * o, axis=-1, keepdims=True) + EPS) * subg * (1.0 - lam_init)


def _diff_seq_kernel(q_ref, k_ref, v_ref, lam_ref, subg_ref, o_ref, m_ref, l_ref, acc_ref,
                     *, n_heads, n_kv, lam_init, tq):
    i = pl.program_id(0)
    grp = n_heads // n_kv
    hd = HEAD_DIM
    vd = 2 * hd
    scale = hd ** -0.5
    m_ref[...] = jnp.full(m_ref.shape, -jnp.inf, F32)
    l_ref[...] = jnp.zeros_like(l_ref)
    acc_ref[...] = jnp.zeros_like(acc_ref)
    row = lax.broadcasted_iota(I32, (grp * tq, tq), 0) % tq
    col = lax.broadcasted_iota(I32, (grp * tq, tq), 1)
    causal = col <= row

    def step(j, masked):
        kb = k_ref[j]
        vb = v_ref[j]
        for kv in range(n_kv):
            for c in range(2):
                idx = kv * 2 + c
                q2 = jnp.concatenate(
                    [q_ref[:, ((kv * grp + g) * 2 + c) * hd:((kv * grp + g) * 2 + c + 1) * hd] for g in range(grp)],
                    axis=0)
                s = _dot_nt(q2, kb[:, idx * hd:(idx + 1) * hd]) * scale
                if masked:
                    s = jnp.where(causal, s, -jnp.inf)
                m_old = m_ref[idx]
                m_new = jnp.maximum(m_old, jnp.max(s, axis=1, keepdims=True))
                alpha = jnp.exp(m_old - m_new)
                p = jnp.exp(s - m_new)
                l_ref[idx] = alpha * l_ref[idx] + jnp.sum(p, axis=1, keepdims=True)
                acc_ref[idx] = alpha * acc_ref[idx] + _dot(p.astype(BF16), vb[:, kv * vd:(kv + 1) * vd])
                m_ref[idx] = m_new

    def body(j, carry):
        step(j, False)
        return carry

    lax.fori_loop(0, i, body, 0)
    step(i, True)
    lam = _lambda(lam_ref, lam_init)
    for kv in range(n_kv):
        o0 = acc_ref[kv * 2] / l_ref[kv * 2]
        o1 = acc_ref[kv * 2 + 1] / l_ref[kv * 2 + 1]
        o = _subln(o0 - lam * o1, subg_ref[...], lam_init)
        for g in range(grp):
            h = kv * grp + g
            o_ref[:, h * vd:(h + 1) * vd] = o[g * tq:(g + 1) * tq, :]


def _resident(shape):
    return pl.BlockSpec(shape, lambda *_: (0,) * len(shape), pipeline_mode=pl.Buffered(1))


def _diff_seq(q_b, k_b, v_b, c_lam, sub_g, lam_init, n_heads, n_kv):
    t = q_b.shape[0]
    tq = min(256, t)
    nb = t // tq
    kw = k_b.shape[1]
    vd = 2 * HEAD_DIM
    grp = n_heads // n_kv
    return pl.pallas_call(
        functools.partial(_diff_seq_kernel, n_heads=n_heads, n_kv=n_kv, lam_init=lam_init, tq=tq),
        out_shape=jax.ShapeDtypeStruct((t, n_heads * vd), F32),
        grid=(nb,),
        in_specs=[pl.BlockSpec((tq, q_b.shape[1]), lambda i: (i, 0)),
                  _resident((nb, tq, kw)), _resident((nb, tq, v_b.shape[1])),
                  pl.BlockSpec(c_lam.shape, lambda i: (0, 0)),
                  pl.BlockSpec((1, vd), lambda i: (0, 0))],
        out_specs=pl.BlockSpec((tq, n_heads * vd), lambda i: (i, 0)),
        scratch_shapes=[pltpu.VMEM((2 * n_kv, grp * tq, 1), F32), pltpu.VMEM((2 * n_kv, grp * tq, 1), F32),
                        pltpu.VMEM((2 * n_kv, grp * tq, vd), F32)],
        compiler_params=_cparams("parallel"),
        name="diff_seq",
    )(q_b, k_b.reshape(nb, tq, kw), v_b.reshape(nb, tq, v_b.shape[1]), c_lam, sub_g.reshape(1, vd))


def _sort_key(score):
    bits = pltpu.bitcast(score, I32)
    bits = jnp.where(score == 0.0, 0, bits)
    return jnp.where(bits < 0, bits ^ 0x7FFFFFFF, bits)


def _count_ge(s_ref, nblk, cand, shape, strict=False):
    def body(kb, acc):
        key = s_ref[kb]
        hit = (key > cand) if strict else (key >= cand)
        return acc + jnp.where(hit, 1.0, 0.0)
    acc = lax.fori_loop(0, nblk, body, jnp.zeros(shape, F32))
    return jnp.sum(acc, axis=1, keepdims=True)


def _kth_largest(s_ref, nblk, n_sel, shape):
    rows = shape[0]

    def body(_, carry):
        ans, bit = carry
        cand = ans + bit
        cnt = _count_ge(s_ref, nblk, cand, shape)
        return jnp.where(cnt >= n_sel, cand, ans), lax.shift_right_logical(bit, 1)

    ans, _ = lax.fori_loop(0, 32, body, (jnp.full((rows, 1), INT_MIN, I32), jnp.int32(INT_MIN)))
    return ans


def _upper_ones(n):
    r = lax.broadcasted_iota(I32, (n, n), 0)
    c = lax.broadcasted_iota(I32, (n, n), 1)
    return (r <= c).astype(BF16)


def _select_ties(key, kth, need, carry, upper):
    eqf = jnp.where(key == kth, 1.0, 0.0)
    rank = carry + _dot(eqf.astype(BF16), upper)
    sel = (key > kth) | ((key == kth) & (rank <= need))
    return sel, carry + jnp.sum(eqf, axis=1, keepdims=True)


def _dsa_seq_kernel(iq_ref, iw_ref, dq_ref, ik_ref, dk_ref, dv_ref, o_ref, s_ref, m_ref, l_ref, acc_ref,
                    *, n_heads, n_kv, n_idx, n_sel, tq, tk):
    i = pl.program_id(0)
    hd = HEAD_DIM
    grp = n_heads // n_kv
    scale = hd ** -0.5
    nblk = (i * tq + tq + tk - 1) // tk
    last = nblk - 1
    qpos = i * tq + lax.broadcasted_iota(I32, (tq, tk), 0)
    col = lax.broadcasted_iota(I32, (tq, tk), 1)
    iw = iw_ref[...] * (n_idx ** -0.5)

    def idx_scores(kb):
        ikb = ik_ref[kb][:, 0:hd]
        score = None
        for h in range(n_idx):
            sc = jnp.maximum(_dot_nt(iq_ref[:, h * hd:(h + 1) * hd], ikb) * scale, 0.0)
            term = iw[:, h:h + 1] * sc
            score = term if score is None else score + term
        return score

    def fill(kb, carry):
        s_ref[kb] = _sort_key(idx_scores(kb))
        return carry

    lax.fori_loop(0, last, fill, 0)
    causal_last = last * tk + col <= qpos
    s_ref[last] = _sort_key(jnp.where(causal_last, idx_scores(last), -jnp.inf))

    shape = (tq, tk)
    kth = _kth_largest(s_ref, nblk, n_sel, shape)
    n_ge = _count_ge(s_ref, nblk, kth, shape)
    n_gt = _count_ge(s_ref, nblk, kth, shape, strict=True)
    need = n_sel - n_gt
    ties = jnp.max(n_ge) > n_sel

    m_ref[...] = jnp.full(m_ref.shape, -jnp.inf, F32)
    l_ref[...] = jnp.zeros_like(l_ref)
    acc_ref[...] = jnp.zeros_like(acc_ref)

    def attend(kb, sel):
        dkb = dk_ref[kb]
        dvb = dv_ref[kb]
        for h in range(n_heads):
            kv = h // grp
            s = _dot_nt(dq_ref[:, h * hd:(h + 1) * hd], dkb[:, kv * hd:(kv + 1) * hd]) * scale
            s = jnp.where(sel, s, -jnp.inf)
            m_old = m_ref[h]
            m_new = jnp.maximum(m_old, jnp.max(s, axis=1, keepdims=True))
            m_safe = jnp.where(m_new == -jnp.inf, 0.0, m_new)
            alpha = jnp.exp(m_old - m_safe)
            p = jnp.exp(s - m_safe)
            l_ref[h] = alpha * l_ref[h] + jnp.sum(p, axis=1, keepdims=True)
            acc_ref[h] = alpha * acc_ref[h] + _dot(p.astype(BF16), dvb[:, kv * hd:(kv + 1) * hd])
            m_ref[h] = m_new

    @pl.when(jnp.logical_not(ties))
    def _():
        def body(kb, carry):
            attend(kb, s_ref[kb] >= kth)
            return carry
        lax.fori_loop(0, last, body, 0)
        attend(last, (s_ref[last] >= kth) & causal_last)

    @pl.when(ties)
    def _():
        upper = _upper_ones(tk)

        def body(kb, carry):
            sel, carry = _select_ties(s_ref[kb], kth, need, carry, upper)
            attend(kb, sel)
            return carry
        carry = lax.fori_loop(0, last, body, jnp.zeros((tq, 1), F32))
        sel, _ = _select_ties(s_ref[last], kth, need, carry, upper)
        attend(last, sel & causal_last)

    for h in range(n_heads):
        o_ref[:, h * hd:(h + 1) * hd] = acc_ref[h] / l_ref[h]


def _dsa_seq(iq_b, iw, dq_b, ik_b, dk_b, dv_b, n_heads, n_kv, n_idx, n_sel):
    t = iq_b.shape[0]
    tq = min(128, t)
    tk = min(256, t)
    nkb = t // tk
    kvw = n_kv * HEAD_DIM
    return pl.pallas_call(
        functools.partial(_dsa_seq_kernel, n_heads=n_heads, n_kv=n_kv, n_idx=n_idx, n_sel=n_sel, tq=tq, tk=tk),
        out_shape=jax.ShapeDtypeStruct((t, n_heads * HEAD_DIM), F32),
        grid=(t // tq,),
        in_specs=[pl.BlockSpec((tq, iq_b.shape[1]), lambda i: (i, 0)),
                  pl.BlockSpec((tq, LANES), lambda i: (i, 0)),
                  pl.BlockSpec((tq, dq_b.shape[1]), lambda i: (i, 0)),
                  _resident((nkb, tk, LANES)), _resident((nkb, tk, kvw)), _resident((nkb, tk, kvw))],
        out_specs=pl.BlockSpec((tq, n_heads * HEAD_DIM), lambda i: (i, 0)),
        scratch_shapes=[pltpu.VMEM((nkb, tq, tk), I32),
                        pltpu.VMEM((n_heads, tq, 1), F32), pltpu.VMEM((n_heads, tq, 1), F32),
                        pltpu.VMEM((n_heads, tq, HEAD_DIM), F32)],
        compiler_params=_cparams("parallel"),
        name="dsa_seq",
    )(iq_b, iw, dq_b, ik_b.reshape(nkb, tk, LANES), dk_b.reshape(nkb, tk, kvw), dv_b.reshape(nkb, tk, kvw))


def _page_specs(n, width, layer, n_pages, ch):
    def spec(p):
        return pl.BlockSpec((None, None, PAGE, width),
                            lambda b, c, pt: (layer, pt[b * n_pages + c * ch + p], 0, 0))
    return [spec(p) for p in range(n)]


def _paged_attn_kernel(pt_ref, *refs, ch, mode, has_mask, lam_init, n_heads, n_kv):
    pos = 0
    qm_ref, knew_ref, vnew_ref, selnew_ref = refs[0:4]
    pos = 4
    mask_ref = None
    if has_mask:
        mask_ref = refs[pos]
        pos += 1
    lam_ref = subg_ref = None
    if mode == "diff":
        lam_ref, subg_ref = refs[pos], refs[pos + 1]
        pos += 2
    k_refs = refs[pos:pos + ch]
    v_refs = refs[pos + ch:pos + 2 * ch]
    o_ref = refs[pos + 2 * ch]
    m_ref, l_ref, acc_ref = refs[pos + 2 * ch + 1:pos + 2 * ch + 4]
    c = pl.program_id(1)
    scale = HEAD_DIM ** -0.5

    @pl.when(c == 0)
    def _():
        m_ref[...] = jnp.full(m_ref.shape, -jnp.inf, F32)
        l_ref[...] = jnp.zeros_like(l_ref)
        acc_ref[...] = jnp.zeros_like(acc_ref)

    qm = qm_ref[0]
    qb = qm.astype(BF16)
    s = jnp.concatenate([_dot_nt(qb, k_refs[p][...].astype(BF16)) for p in range(ch)], axis=1) * scale
    if has_mask:
        s = jnp.where(mask_ref[0] > 0.0, s, -jnp.inf)
    m_old = m_ref[...]
    m_new = jnp.maximum(m_old, jnp.max(s, axis=1, keepdims=True))
    m_safe = jnp.where(m_new == -jnp.inf, 0.0, m_new)
    alpha = jnp.exp(m_old - m_safe)
    p = jnp.exp(s - m_safe).astype(BF16)
    l_ref[...] = alpha * l_ref[...] + jnp.sum(p.astype(F32), axis=1, keepdims=True)
    pv = _dot(p[:, 0:PAGE], v_refs[0][...].astype(BF16))
    for j in range(1, ch):
        pv = pv + _dot(p[:, j * PAGE:(j + 1) * PAGE], v_refs[j][...].astype(BF16))
    acc_ref[...] = alpha * acc_ref[...] + pv
    m_ref[...] = m_new

    @pl.when(c == pl.num_programs(1) - 1)
    def _():
        s_new = jnp.sum(qm * knew_ref[0], axis=1, keepdims=True) * scale
        s_new = jnp.where(selnew_ref[0][:, 0:1] > 0.0, s_new, -jnp.inf)
        m_o = m_ref[...]
        m_f = jnp.maximum(m_o, s_new)
        m_s = jnp.where(m_f == -jnp.inf, 0.0, m_f)
        al = jnp.exp(m_o - m_s)
        p_new = jnp.exp(s_new - m_s)
        l = al * l_ref[...] + p_new
        o = (al * acc_ref[...] + p_new * vnew_ref[0]) / l
        rows = o.shape[0]
        r = lax.broadcasted_iota(I32, (rows, 1), 0)
        if mode == "diff":
            vd = 2 * HEAD_DIM
            grp = n_heads // n_kv
            kv_of = (r % n_heads) // grp
            osel = o[:, 0:vd]
            for kv in range(1, n_kv):
                osel = jnp.where(kv_of == kv, o[:, kv * vd:(kv + 1) * vd], osel)
            lam = _lambda(lam_ref, lam_init)
            od = osel[0:n_heads, :] - lam * osel[n_heads:2 * n_heads, :]
            o_ref[0] = _subln(od, subg_ref[...], lam_init)
        else:
            grp = n_heads // n_kv
            kv_of = r // grp
            osel = o[:, 0:HEAD_DIM]
            for kv in range(1, n_kv):
                osel = jnp.where(kv_of == kv, o[:, kv * HEAD_DIM:(kv + 1) * HEAD_DIM], osel)
            o_ref[0] = osel


def _paged_attn(page_table, layer, qm, knew, vnew, selnew, mask, cache_k, cache_v, *, mode, n_heads, n_kv,
                lam=None, sub_g=None, lam_init=0.0):
    nb, n_pages = page_table.shape
    rows, dk = qm.shape[1:]
    dv = cache_v.shape[-1]
    ch = min(16, n_pages)
    nc = n_pages // ch
    has_mask = mask is not None
    per_b = lambda shape: pl.BlockSpec((1,) + shape, lambda b, c, pt: (b,) + (0,) * len(shape))
    in_specs = [per_b((rows, dk)), per_b((1, dk)), per_b((1, dv)), per_b((1, LANES))]
    args = [qm, knew, vnew, selnew]
    if has_mask:
        in_specs.append(pl.BlockSpec((1, 1, ch * PAGE), lambda b, c, pt: (b, 0, c)))
        args.append(mask)
    if mode == "diff":
        in_specs += [pl.BlockSpec(lam.shape, lambda b, c, pt: (0, 0)),
                     pl.BlockSpec((1, 2 * HEAD_DIM), lambda b, c, pt: (0, 0))]
        args += [lam, sub_g.reshape(1, -1)]
        out_rows, out_w = n_heads, 2 * HEAD_DIM
    else:
        out_rows, out_w = n_heads, HEAD_DIM
    in_specs += _page_specs(ch, dk, layer, n_pages, ch) + _page_specs(ch, dv, layer, n_pages, ch)
    args += [cache_k] * ch + [cache_v] * ch
    grid_spec = pltpu.PrefetchScalarGridSpec(
        num_scalar_prefetch=1, grid=(nb, nc), in_specs=in_specs,
        out_specs=pl.BlockSpec((1, out_rows, out_w), lambda b, c, pt: (b, 0, 0)),
        scratch_shapes=[pltpu.VMEM((rows, 1), F32), pltpu.VMEM((rows, 1), F32), pltpu.VMEM((rows, dv), F32)])
    return pl.pallas_call(
        functools.partial(_paged_attn_kernel, ch=ch, mode=mode, has_mask=has_mask, lam_init=lam_init,
                          n_heads=n_heads, n_kv=n_kv),
        out_shape=jax.ShapeDtypeStruct((nb, out_rows, out_w), F32),
        grid_spec=grid_spec,
        compiler_params=_cparams("parallel", "arbitrary"),
        name="paged_attn_" + mode,
    )(page_table.reshape(-1), *args)


def _paged_idx_kernel(pt_ref, iq_ref, w_ref, *refs, ch, n_idx):
    pages = refs[:ch]
    o_ref = refs[ch]
    scale = HEAD_DIM ** -0.5
    iq = iq_ref[0]
    w = w_ref[0] * (n_idx ** -0.5)
    outs = []
    for p in range(ch):
        sc = jnp.maximum(_dot_nt(iq, pages[p][...].astype(BF16)) * scale, 0.0)
        outs.append(jnp.sum(w * sc, axis=0, keepdims=True))
    o_ref[0] = jnp.concatenate(outs, axis=1)


def _paged_idx_scores(page_table, layer, iq, w, cache_idx, n_idx):
    nb, n_pages = page_table.shape
    ch = min(16, n_pages)
    nc = n_pages // ch
    grid_spec = pltpu.PrefetchScalarGridSpec(
        num_scalar_prefetch=1, grid=(nb, nc),
        in_specs=[pl.BlockSpec((1, n_idx, HEAD_DIM), lambda b, c, pt: (b, 0, 0)),
                  pl.BlockSpec((1, n_idx, 1), lambda b, c, pt: (b, 0, 0))]
        + _page_specs(ch, cache_idx.shape[-1], layer, n_pages, ch),
        out_specs=pl.BlockSpec((1, 1, ch * PAGE), lambda b, c, pt: (b, 0, c)))
    return pl.pallas_call(
        functools.partial(_paged_idx_kernel, ch=ch, n_idx=n_idx),
        out_shape=jax.ShapeDtypeStruct((nb, 1, n_pages * PAGE), F32),
        grid_spec=grid_spec,
        compiler_params=_cparams("parallel", "arbitrary"),
        name="paged_idx_scores",
    )(page_table.reshape(-1), iq, w, *([cache_idx] * ch))


def _topk_mask_kernel(sc_ref, iq_ref, ik_ref, w_ref, mask_ref, mnew_ref, s_ref, *, n_idx, n_sel, bw):
    nb, t = sc_ref.shape
    nblk = t // bw
    scale = HEAD_DIM ** -0.5
    prod = iq_ref[...].astype(F32) * ik_ref[...].astype(F32)
    seg_r = lax.broadcasted_iota(I32, (prod.shape[1], LANES), 0) // HEAD_DIM
    seg_c = lax.broadcasted_iota(I32, (prod.shape[1], LANES), 1)
    qk = _dot3(prod, (seg_r == seg_c).astype(BF16))
    s_new = jnp.sum(w_ref[...] * (n_idx ** -0.5) * jnp.maximum(qk * scale, 0.0), axis=1, keepdims=True)
    lane = lax.broadcasted_iota(I32, (nb, bw), 1)
    for kb in range(nblk):
        s_ref[kb] = _sort_key(sc_ref[:, kb * bw:(kb + 1) * bw])
    s_ref[nblk] = _sort_key(jnp.where(lane == 0, s_new, -jnp.inf))
    shape = (nb, bw)
    kth = _kth_largest(s_ref, nblk + 1, n_sel, shape)
    need = n_sel - _count_ge(s_ref, nblk + 1, kth, shape, strict=True)
    upper = _upper_ones(bw)
    carry = jnp.zeros((nb, 1), F32)
    for kb in range(nblk):
        sel, carry = _select_ties(s_ref[kb], kth, need, carry, upper)
        mask_ref[:, kb * bw:(kb + 1) * bw] = jnp.where(sel, 1.0, 0.0)
    sel, _ = _select_ties(s_ref[nblk], kth, need, carry, upper)
    mnew_ref[...] = jnp.where(sel & (lane == 0), 1.0, 0.0)[:, 0:LANES]


def _topk_mask(scores, iq_b, ik_new_b, w, n_idx, n_sel):
    nb, t = scores.shape
    bw = min(256, t)
    full = lambda a: pl.BlockSpec(a.shape, lambda i: (0,) * a.ndim)
    return pl.pallas_call(
        functools.partial(_topk_mask_kernel, n_idx=n_idx, n_sel=n_sel, bw=bw),
        out_shape=(jax.ShapeDtypeStruct((nb, t), F32), jax.ShapeDtypeStruct((nb, LANES), F32)),
        grid=(1,),
        in_specs=[full(scores), full(iq_b), full(ik_new_b), full(w)],
        out_specs=(pl.BlockSpec((nb, t), lambda i: (0, 0)), pl.BlockSpec((nb, LANES), lambda i: (0, 0))),
        scratch_shapes=[pltpu.VMEM((t // bw + 1, nb, bw), I32)],
        compiler_params=_cparams("arbitrary"),
        name="topk_mask",
    )(scores, iq_b, ik_new_b, w)


def _bf(w):
    return w.astype(BF16)


def _place(q3, slot_of_row, n_slots):
    onehot = (np.asarray(slot_of_row)[:, None] == np.arange(n_slots)[None, :]).astype(np.float32)
    out = q3[:, :, None, :] * jnp.asarray(onehot)[None, :, :, None]
    return out.reshape(q3.shape[0], q3.shape[1], n_slots * q3.shape[2])


def _even_weights(w_in, d_inner, conv_ch, n_a_heads, qw, kvw):
    c = np.cumsum([0, d_inner, conv_ch, n_a_heads, qw, kvw, kvw])
    z, xbc, dt, q, k, v = (w_in[:, c[j]:c[j + 1]] for j in range(6))
    dt = jnp.pad(dt, ((0, 0), (0, LANES - n_a_heads)))
    w = _bf(jnp.concatenate([z, xbc, q, k, v, dt], axis=1))
    off = np.cumsum([0, d_inner, conv_ch, qw, kvw, kvw])
    return w, dict(z=int(off[0]), xbc=int(off[1]), q=int(off[2]), k=int(off[3]), v=int(off[4]), dt=int(off[5]))


def _odd_weights(w_in, sizes):
    c = np.cumsum([0] + list(sizes))
    cq, ck, cv, dq, dk, dv, iq, iw, ik = (w_in[:, c[j]:c[j + 1]] for j in range(9))
    ik = jnp.pad(ik, ((0, 0), (0, LANES - ik.shape[1])))
    iw = jnp.pad(iw, ((0, 0), (0, LANES - iw.shape[1])))
    parts = [cq, dq, iq, ck, cv, dk, dv, ik, iw]
    off = np.cumsum([0] + [p.shape[1] for p in parts])
    names = ["cq", "dq", "iq", "ck", "cv", "dk", "dv", "ik", "iw"]
    return _bf(jnp.concatenate(parts, axis=1)), {n: int(o) for n, o in zip(names, off[:-1])}


def _mixer_even(x, pos, seq_mode, st_conv, st_ssm, win_k, win_v, norm_g, w_in, conv_w, conv_b, dt_bias, a_log,
                d_skip, gain, qn_g, kn_g, sinks, w_out):
    m, _ = x.shape
    n_a_heads = a_log.shape[0]
    d_inner = gain.shape[0]
    conv_ch = conv_w.shape[1]
    n_heads = sinks.shape[0]
    qw = n_heads * HEAD_DIM
    kvw = (w_in.shape[1] - d_inner - conv_ch - n_a_heads - qw) // 2
    n_kv = kvw // HEAD_DIM
    kw = conv_w.shape[0]
    gn = A_GROUPS * A_STATE
    hpg = n_a_heads // A_GROUPS
    w, off = _even_weights(w_in, d_inner, conv_ch, n_a_heads, qw, kvw)
    cos, sin = _rope_tables(pos)
    proj = _mm([x], [w], norm_g=norm_g, name="in_proj_even")
    q_f, q_b = _norm_rope(proj, off["q"], qw, qn_g, cos, sin, name="swa_q_rope")
    k_f, k_b = _norm_rope(proj, off["k"], kvw, kn_g, cos, sin, name="swa_k_rope")
    v_f = proj[:, off["v"]:off["v"] + kvw]
    xbc_raw = proj[:, off["xbc"]:off["xbc"] + conv_ch]
    if seq_mode:
        state8 = jnp.zeros((SUBLANES, conv_ch), F32)
        xbc = _conv_seq(proj, off["xbc"], conv_ch, state8, conv_w, conv_b)
        s0_t = jnp.zeros((A_GROUPS, A_STATE, hpg * A_HEAD_DIM), F32)
        ya, st = _ssd_seq(xbc, proj, off["z"], off["dt"], dt_bias, a_log, d_skip, gain, s0_t, n_a_heads)
        ssm_new = st.reshape(A_GROUPS, A_STATE, hpg, A_HEAD_DIM).transpose(0, 2, 3, 1).reshape(
            1, n_a_heads, A_HEAD_DIM, A_STATE)
        conv_new = xbc_raw[m - (kw - 1):][None]
        ob = _swa_seq(q_b, k_b, _bf(v_f), sinks, n_heads, n_kv)
        wb = min(WINDOW, m)
        new_k = k_f[m - wb:].reshape(1, wb, n_kv, HEAD_DIM)
        new_v = v_f[m - wb:].reshape(1, wb, n_kv, HEAD_DIM)
    else:
        xbc = _conv_step(proj, off["xbc"], conv_ch, st_conv.transpose(1, 0, 2), conv_w, conv_b)
        xdt, dec = _ssd_step_pre(xbc, proj, off["dt"], dt_bias, a_log, d_inner)
        ssm_new, y = _ssd_step(st_ssm, xdt.T, dec.T, xbc[:, d_inner:d_inner + gn], xbc[:, d_inner + gn:])
        ya = _gated_norm_call(y, xbc, proj, off["z"], d_skip, gain)
        conv_new = jnp.concatenate([st_conv[:, 1:], xbc_raw[:, None, :]], axis=1)
        wb = win_k.shape[1]
        grp = n_heads // n_kv
        qm = _place(q_f.reshape(m, n_heads, HEAD_DIM), [h // grp for h in range(n_heads)], n_kv)
        o, new_k, new_v = _swa_step(qm, k_f[:, None, :], v_f[:, None, :], win_k.reshape(m, wb, kvw),
                                    win_v.reshape(m, wb, kvw), sinks, n_heads, n_kv)
        ob = o.reshape(m, qw)
        new_k = new_k.reshape(m, wb, n_kv, HEAD_DIM)
        new_v = new_v.reshape(m, wb, n_kv, HEAD_DIM)
    wo = _bf(w_out)
    y = _mm([ya, ob], [wo[:d_inner], wo[d_inner:]], res=x, name="out_proj_even")
    return y, (conv_new, ssm_new, new_k, new_v)


def _mixer_odd(x, pos, seq_mode, paged, norm_g, w_in, qn_g, kn_g, lam_p, sub_g, dqn_g, dkn_g, w_out, lam_init,
               sizes, n_sel):
    m, _ = x.shape
    hd = HEAD_DIM
    c_heads = sizes[0] // (2 * hd)
    c_kv = sizes[1] // (2 * hd)
    d_heads = sizes[3] // hd
    d_kv = sizes[4] // hd
    n_idx = sizes[7]
    w, off = _odd_weights(w_in, sizes)
    cos, sin = _rope_tables(pos)
    proj = _mm([x], [w], norm_g=norm_g, name="in_proj_odd")
    cq_f, cq_b = _norm_rope(proj, off["cq"], sizes[0], qn_g, cos, sin, name="diff_q_rope")
    ck_f, ck_b = _norm_rope(proj, off["ck"], sizes[1], kn_g, cos, sin, name="diff_k_rope")
    dq_f, dq_b = _norm_rope(proj, off["dq"], sizes[3], dqn_g, cos, sin, name="dsa_q_rope")
    dk_f, dk_b = _norm_rope(proj, off["dk"], sizes[4], dkn_g, cos, sin, name="dsa_k_rope")
    iq_f, iq_b = _norm_rope(proj, off["iq"], sizes[6], None, cos, sin, name="idx_q_rope")
    ik_f, ik_b = _norm_rope(proj, off["ik"], LANES, None, cos, sin, name="idx_k_rope")
    cv_f = proj[:, off["cv"]:off["cv"] + sizes[2]]
    dv_f = proj[:, off["dv"]:off["dv"] + sizes[5]]
    iw = proj[:, off["iw"]:off["iw"] + LANES]
    if seq_mode:
        oc = _diff_seq(cq_b, ck_b, _bf(cv_f), lam_p, sub_g, lam_init, c_heads, c_kv)
        od = _dsa_seq(iq_b, iw, dq_b, ik_b, dk_b, _bf(dv_f), d_heads, d_kv, n_idx, n_sel)
    else:
        c_k, c_v, d_k, d_v, d_i, table, layer = paged
        pool = c_k.shape[1]
        ones = jnp.ones((m, 1, LANES), F32)
        grp = c_heads // c_kv
        q4 = cq_f.reshape(m, c_heads, 2, hd).transpose(0, 2, 1, 3).reshape(m, 2 * c_heads, hd)
        slots = [(h // grp) * 2 + c for c in range(2) for h in range(c_heads)]
        qm_c = _place(q4, slots, 2 * c_kv)
        oc = _paged_attn(table, layer, qm_c, ck_f[:, None, :], cv_f[:, None, :], ones, None,
                         c_k.reshape(c_k.shape[0], pool, PAGE, -1), c_v.reshape(c_v.shape[0], pool, PAGE, -1),
                         mode="diff", n_heads=c_heads, n_kv=c_kv, lam=lam_p, sub_g=sub_g, lam_init=lam_init)
        oc = oc.reshape(m, -1)
        scores = _paged_idx_scores(table, layer, iq_b.reshape(m, n_idx, hd), iw[:, :n_idx, None], d_i, n_idx)
        ik_tiled = jnp.tile(ik_b[:, :hd], (1, n_idx))
        mask, mnew = _topk_mask(scores.reshape(m, -1), iq_b, ik_tiled, iw, n_idx, n_sel)
        dgrp = d_heads // d_kv
        qm_d = _place(dq_f.reshape(m, d_heads, hd), [h // dgrp for h in range(d_heads)], d_kv)
        od = _paged_attn(table, layer, qm_d, dk_f[:, None, :], dv_f[:, None, :], mnew[:, None, :], mask[:, None, :],
                         d_k.reshape(d_k.shape[0], pool, PAGE, -1), d_v.reshape(d_v.shape[0], pool, PAGE, -1),
                         mode="gqa", n_heads=d_heads, n_kv=d_kv)
        od = od.reshape(m, -1)
    wo = _bf(w_out)
    y = _mm([oc, od], [wo[:oc.shape[1]], wo[oc.shape[1]:]], res=x, name="out_proj_odd")
    lead = (1, m) if seq_mode else (m, 1)
    caches = (ck_f.reshape(lead + (c_kv, 2, hd)), cv_f.reshape(lead + (c_kv, 2 * hd)),
              dk_f.reshape(lead + (d_kv, hd)), dv_f.reshape(lead + (d_kv, hd)), ik_f[:, :hd].reshape(lead + (hd,)))
    return y, caches


def kernel(x_prompt, x_sample, state_ssm, state_ssm_conv, cache_swa_k, cache_swa_v, cache_c_k, cache_c_v, cache_d_k, cache_d_v, cache_d_idx, state_ffn_conv, page_table, norm_mix_g, norm_ffn_g, a_w_in, a_conv_w, a_conv_b, a_dt_bias, a_A_log, a_D, a_norm_g, b_qn_g, b_kn_g, b_sinks, e_w_out, m_w_in, c_qn_g, c_kn_g, c_lam, c_subln_g, d_qn_g, d_kn_g, m_w_out, ffn_w_gate, ffn_w_up, ffn_conv_w, ffn_conv_b, ffn_w_down):
    bp, seq, d_model = x_prompt.shape
    nb = x_sample.shape[0]
    assert bp == 1 and x_sample.shape[1] == 1
    depth = norm_mix_g.shape[0]
    d_ff = ffn_w_gate.shape[2]
    past = page_table.shape[1] * PAGE
    xp = x_prompt.reshape(seq, d_model)
    xs = x_sample.reshape(nb, d_model)
    pos_p = jnp.arange(seq)
    pos_s = jnp.full((nb,), past, I32)
    hd = HEAD_DIM
    c_kv, d_kv, idx_dim = cache_c_k.shape[3], cache_d_k.shape[3], cache_d_idx.shape[3]
    d_heads = d_model // 128
    c_heads = d_model // 256
    n_idx = m_w_in.shape[2] - (c_heads * 2 * hd + 2 * c_kv * 2 * hd + d_heads * hd + 2 * d_kv * hd
                               + d_heads * hd + idx_dim)
    odd_sizes = (c_heads * 2 * hd, c_kv * 2 * hd, c_kv * 2 * hd, d_heads * hd, d_kv * hd, d_kv * hd,
                 d_heads * hd, n_idx, idx_dim)
    outs_p = {k: [] for k in ("ssm", "cnv", "swk", "swv", "ck", "cv", "dk", "dv", "di", "fc")}
    outs_s = {k: [] for k in outs_p}
    for i in range(depth):
        if i % 2 == 0:
            e = i // 2
            wts = (norm_mix_g[i], a_w_in[e], a_conv_w[e], a_conv_b[e], a_dt_bias[e], a_A_log[e], a_D[e],
                   a_norm_g[e], b_qn_g[e], b_kn_g[e], b_sinks[e], e_w_out[e])
            xp, (c1, s1, k1, v1) = _mixer_even(xp, pos_p, True, None, None, None, None, *wts)
            xs, (c2, s2, k2, v2) = _mixer_even(xs, pos_s, False, state_ssm_conv[e], state_ssm[e],
                                               cache_swa_k[e], cache_swa_v[e], *wts)
            for d, vals in ((outs_p, (c1, s1, k1, v1)), (outs_s, (c2, s2, k2, v2))):
                for key, val in zip(("cnv", "ssm", "swk", "swv"), vals):
                    d[key].append(val)
        else:
            o = i // 2
            lam_init = 0.8 - 0.6 * math.exp(-0.3 * i)
            wts = (norm_mix_g[i], m_w_in[o], c_qn_g[o], c_kn_g[o], c_lam[o], c_subln_g[o], d_qn_g[o], d_kn_g[o],
                   m_w_out[o], lam_init, odd_sizes)
            xp, cp = _mixer_odd(xp, pos_p, True, None, *wts, min(256, seq // 4))
            xs, cs = _mixer_odd(xs, pos_s, False,
                                (cache_c_k, cache_c_v, cache_d_k, cache_d_v, cache_d_idx, page_table, o),
                                *wts, min(256, (past + 1) // 4))
            for d, vals in ((outs_p, cp), (outs_s, cs)):
                for key, val in zip(("ck", "cv", "dk", "dv", "di"), vals):
                    d[key].append(val)
        fw = (norm_ffn_g[i], _bf(ffn_w_gate[i]), _bf(ffn_w_up[i]), ffn_conv_w[i], ffn_conv_b[i], _bf(ffn_w_down[i]))
        zrow = jnp.zeros((1, d_ff), F32)
        xp, gp = _ffn(xp, zrow, zrow, *fw, seq_mode=True)
        outs_p["fc"].append(gp[gp.shape[0] - (ffn_conv_w.shape[1] - 1):][None])
        st = state_ffn_conv[i]
        xs, gs = _ffn(xs, st[:, 0, :], st[:, 1, :], *fw, seq_mode=False)
        outs_s["fc"].append(jnp.stack([st[:, 1, :], gs], axis=1))
    order = ("ssm", "cnv", "swk", "swv", "ck", "cv", "dk", "dv", "di", "fc")
    return ((xp.reshape(1, seq, d_model), xs.reshape(nb, 1, d_model))
            + tuple(jnp.stack(outs_p[k]) for k in order) + tuple(jnp.stack(outs_s[k]) for k in order))
```

```python
import functools
import math

import jax
import jax.numpy as jnp
import numpy as np
from jax import lax
from jax.experimental import pallas as pl
from jax.experimental.pallas import tpu as pltpu

F32 = jnp.float32
BF16 = jnp.bfloat16
I32 = jnp.int32

EPS = 1e-6
ROPE_THETA = 10000.0
HEAD_DIM = 64
LANES = 128
SUBLANES = 8
VMEM_LIMIT = 56 * 1024 * 1024
WINDOW = 128
SSD_CHUNK = 128
PAGE = 128
A_GROUPS = 2
A_HEAD_DIM = 64
A_STATE = 128
INT_MIN = -2147483648


def _cparams(*sem):
    return pltpu.CompilerParams(dimension_semantics=sem, vmem_limit_bytes=VMEM_LIMIT)


def _pick_tile(n, cap):
    best = LANES
    for m in range(1, n // LANES + 1):
        if n % (m * LANES) == 0 and m * LANES <= cap:
            best = m * LANES
    return best


def _row_tile(m, cap):
    t = min(m, cap)
    while m % t:
        t //= 2
    return t


def _split3(x):
    h = x.astype(BF16)
    r = x - h.astype(F32)
    m = r.astype(BF16)
    lo = (r - m.astype(F32)).astype(BF16)
    return h, m, lo


def _dot(a, b):
    return jnp.dot(a, b, preferred_element_type=F32)


def _dot_nt(a, b):
    return lax.dot_general(a, b, (((1,), (1,)), ((), ())), preferred_element_type=F32)


def _dot3(x, w01):
    h, m, lo = _split3(x)
    return _dot(h, w01) + _dot(m, w01) + _dot(lo, w01)


def _dot3_left(w01, x):
    h, m, lo = _split3(x)
    return _dot(w01, h) + _dot(w01, m) + _dot(w01, lo)


def _silu(x):
    return x * (1.0 / (1.0 + jnp.exp(-x)))


def _softplus(x):
    return jnp.maximum(x, 0.0) + jnp.log(1.0 + jnp.exp(-jnp.abs(x)))


def _mm_kernel(*refs, n_lhs, has_norm, has_res):
    xs = refs[:n_lhs]
    pos = n_lhs
    g_ref = None
    if has_norm:
        g_ref = refs[pos]
        pos += 1
    ws = refs[pos:pos + n_lhs]
    pos += n_lhs
    res_ref = None
    if has_res:
        res_ref = refs[pos]
        pos += 1
    o_ref = refs[pos]
    xb = refs[pos + 1:pos + 1 + n_lhs]

    @pl.when(pl.program_id(1) == 0)
    def _():
        for k in range(n_lhs):
            x = xs[k][...]
            if has_norm and k == 0:
                x = x * lax.rsqrt(jnp.mean(x * x, axis=-1, keepdims=True) + EPS) * g_ref[...]
            xb[k][...] = x.astype(BF16)

    acc = _dot(xb[0][...], ws[0][...])
    for k in range(1, n_lhs):
        acc = acc + _dot(xb[k][...], ws[k][...])
    if has_res:
        acc = acc + res_ref[...]
    o_ref[...] = acc


def _mm(xs, ws, *, norm_g=None, res=None, tm_cap=512, tn_cap=1280, name="mm"):
    m = xs[0].shape[0]
    n = ws[0].shape[1]
    tm = _row_tile(m, tm_cap)
    tn = _pick_tile(n, tn_cap)
    n_lhs = len(xs)
    in_specs = [pl.BlockSpec((tm, x.shape[1]), lambda i, j: (i, 0)) for x in xs]
    args = list(xs)
    if norm_g is not None:
        in_specs.append(pl.BlockSpec((1, xs[0].shape[1]), lambda i, j: (0, 0)))
        args.append(norm_g.reshape(1, -1))
    in_specs += [pl.BlockSpec((w.shape[0], tn), lambda i, j: (0, j)) for w in ws]
    args += list(ws)
    if res is not None:
        in_specs.append(pl.BlockSpec((tm, tn), lambda i, j: (i, j)))
        args.append(res)
    return pl.pallas_call(
        functools.partial(_mm_kernel, n_lhs=n_lhs, has_norm=norm_g is not None, has_res=res is not None),
        out_shape=jax.ShapeDtypeStruct((m, n), F32),
        grid=(m // tm, n // tn),
        in_specs=in_specs,
        out_specs=pl.BlockSpec((tm, tn), lambda i, j: (i, j)),
        scratch_shapes=[pltpu.VMEM((tm, x.shape[1]), BF16) for x in xs],
        compiler_params=_cparams("parallel", "arbitrary"),
        name=name,
    )(*args)


def _ffn_kernel(x_ref, halo_ref, g_ref, wg_ref, wu_ref, cw_ref, cb_ref, wd_ref, p0_ref, p1_ref,
                o_ref, gout_ref, xb_ref, hb_ref, gs_ref, acc_ref, *, seq_mode, tm):
    i = pl.program_id(0)
    j = pl.program_id(1)
    nj = pl.num_programs(1)

    def norm(x):
        return (x * lax.rsqrt(jnp.mean(x * x, axis=-1, keepdims=True) + EPS) * g_ref[...]).astype(BF16)

    @pl.when(j == 0)
    def _():
        xb_ref[...] = norm(x_ref[...])
        if seq_mode:
            hb_ref[...] = norm(halo_ref[...])
        acc_ref[...] = jnp.zeros_like(acc_ref)

    g = _dot(xb_ref[...], wg_ref[...])
    u = _dot(xb_ref[...], wu_ref[...])
    cw = cw_ref[...]
    if seq_mode:
        carried = jnp.concatenate([jnp.zeros((SUBLANES - 2, g.shape[1]), F32), p0_ref[...], p1_ref[...]], axis=0)
        prev = jnp.where(i == 0, carried, _dot(hb_ref[...], wg_ref[...]))
        gs_ref[0:SUBLANES, :] = prev
        gs_ref[SUBLANES:, :] = g
        g1 = gs_ref[pl.ds(SUBLANES - 1, tm), :]
        g2 = gs_ref[pl.ds(SUBLANES - 2, tm), :]
        gout_ref[...] = g[tm - SUBLANES:, :]
    else:
        g1 = p1_ref[...]
        g2 = p0_ref[...]
        gout_ref[...] = g
    c = cw[0:1, :] * g2 + cw[1:2, :] * g1 + cw[2:3, :] * g + cb_ref[...]
    act = (_silu(c) * u).astype(BF16)
    acc_ref[...] += _dot(act, wd_ref[...])

    @pl.when(j == nj - 1)
    def _():
        o_ref[...] = x_ref[...] + acc_ref[...]


def _ffn(x, prev0, prev1, norm_g, wg, wu, conv_w, conv_b, wd, *, seq_mode):
    m, d = x.shape
    f = wg.shape[1]
    tm = _row_tile(m, 1024 if seq_mode else 128)
    tn = _pick_tile(f, 1408)
    ni, nj = m // tm, f // tn
    hb = tm // SUBLANES
    if seq_mode:
        prev_spec = pl.BlockSpec((1, tn), lambda i, j: (0, j))
        gout_rows, gout_shape = SUBLANES, (ni * SUBLANES, f)
    else:
        prev_spec = pl.BlockSpec((tm, tn), lambda i, j: (i, j))
        gout_rows, gout_shape = tm, (m, f)
    out, gout = pl.pallas_call(
        functools.partial(_ffn_kernel, seq_mode=seq_mode, tm=tm),
        out_shape=(jax.ShapeDtypeStruct((m, d), F32), jax.ShapeDtypeStruct(gout_shape, F32)),
        grid=(ni, nj),
        in_specs=[
            pl.BlockSpec((tm, d), lambda i, j: (i, 0)),
            pl.BlockSpec((SUBLANES, d), lambda i, j: (jnp.maximum(i * hb - 1, 0), 0)),
            pl.BlockSpec((1, d), lambda i, j: (0, 0)),
            pl.BlockSpec((d, tn), lambda i, j: (0, j)),
            pl.BlockSpec((d, tn), lambda i, j: (0, j)),
            pl.BlockSpec((conv_w.shape[0], tn), lambda i, j: (0, j)),
            pl.BlockSpec((1, tn), lambda i, j: (0, j)),
            pl.BlockSpec((tn, d), lambda i, j: (j, 0)),
            prev_spec, prev_spec,
        ],
        out_specs=(pl.BlockSpec((tm, d), lambda i, j: (i, 0)),
                   pl.BlockSpec((gout_rows, tn), lambda i, j: (i, j))),
        scratch_shapes=[pltpu.VMEM((tm, d), BF16), pltpu.VMEM((SUBLANES, d), BF16),
                        pltpu.VMEM((tm + SUBLANES, tn), F32), pltpu.VMEM((tm, d), F32)],
        compiler_params=_cparams("parallel", "arbitrary"),
        name="conv_ffn",
    )(x, x, norm_g.reshape(1, -1), wg, wu, conv_w, conv_b.reshape(1, -1), wd, prev0, prev1)
    return out, gout


def _conv_seq_kernel(x_ref, halo_ref, st_ref, w_ref, b_ref, o_ref, xs_ref, *, tm, kw):
    i = pl.program_id(0)
    prev = jnp.where(i == 0, st_ref[...], halo_ref[...])
    xs_ref[0:SUBLANES, :] = prev
    xs_ref[SUBLANES:, :] = x_ref[...]
    w = w_ref[...]
    acc = b_ref[...] + w[kw - 1:kw, :] * x_ref[...]
    for t in range(1, kw):
        acc = acc + w[kw - 1 - t:kw - t, :] * xs_ref[pl.ds(SUBLANES - t, tm), :]
    o_ref[...] = _silu(acc)


def _conv_seq(src, col0, width, state8, w, b):
    m = src.shape[0]
    kw = w.shape[0]
    tm = _row_tile(m, 512)
    tn = _pick_tile(math.gcd(width, col0) if col0 else width, 512)
    cb = col0 // tn
    hb = tm // SUBLANES
    return pl.pallas_call(
        functools.partial(_conv_seq_kernel, tm=tm, kw=kw),
        out_shape=jax.ShapeDtypeStruct((m, width), F32),
        grid=(m // tm, width // tn),
        in_specs=[
            pl.BlockSpec((tm, tn), lambda i, j: (i, cb + j)),
            pl.BlockSpec((SUBLANES, tn), lambda i, j: (jnp.maximum(i * hb - 1, 0), cb + j)),
            pl.BlockSpec((SUBLANES, tn), lambda i, j: (0, j)),
            pl.BlockSpec((kw, tn), lambda i, j: (0, j)),
            pl.BlockSpec((1, tn), lambda i, j: (0, j)),
        ],
        out_specs=pl.BlockSpec((tm, tn), lambda i, j: (i, j)),
        scratch_shapes=[pltpu.VMEM((tm + SUBLANES, tn), F32)],
        compiler_params=_cparams("parallel", "parallel"),
        name="ssm_conv_seq",
    )(src, src, state8, w, b.reshape(1, -1))


def _conv_step_kernel(x_ref, s_ref, w_ref, b_ref, o_ref, *, kw):
    w = w_ref[...]
    acc = b_ref[...] + w[kw - 1:kw, :] * x_ref[...]
    for t in range(kw - 1):
        acc = acc + w[t:t + 1, :] * s_ref[t]
    o_ref[...] = _silu(acc)


def _conv_step(src, col0, width, state, w, b):
    m = src.shape[0]
    kw = w.shape[0]
    tn = _pick_tile(math.gcd(width, col0) if col0 else width, 512)
    cb = col0 // tn
    return pl.pallas_call(
        functools.partial(_conv_step_kernel, kw=kw),
        out_shape=jax.ShapeDtypeStruct((m, width), F32),
        grid=(width // tn,),
        in_specs=[
            pl.BlockSpec((m, tn), lambda j: (0, cb + j)),
            pl.BlockSpec((kw - 1, m, tn), lambda j: (0, 0, j)),
            pl.BlockSpec((kw, tn), lambda j: (0, j)),
            pl.BlockSpec((1, tn), lambda j: (0, j)),
        ],
        out_specs=pl.BlockSpec((m, tn), lambda j: (0, j)),
        compiler_params=_cparams("parallel"),
        name="ssm_conv_step",
    )(src, state, w, b.reshape(1, -1))


def _seg_ones(seg):
    r = lax.broadcasted_iota(I32, (LANES, LANES), 0) // seg
    c = lax.broadcasted_iota(I32, (LANES, LANES), 1) // seg
    return (r == c).astype(BF16)


def _rope128(x, cos, sin_signed):
    lane = lax.broadcasted_iota(I32, x.shape, 1)
    rot = jnp.where(lane % HEAD_DIM < HEAD_DIM // 2,
                    pltpu.roll(x, LANES - HEAD_DIM // 2, 1), pltpu.roll(x, HEAD_DIM // 2, 1))
    return x * cos + rot * sin_signed


def _norm_rope_kernel(x_ref, g_ref, cos_ref, sin_ref, o_ref, ob_ref, *, do_norm, width):
    cos = cos_ref[...]
    sin = sin_ref[...]
    ones = _seg_ones(HEAD_DIM)
    for c in range(width // LANES):
        x = x_ref[:, c * LANES:(c + 1) * LANES]
        if do_norm:
            ms = _dot3(x * x, ones) * (1.0 / HEAD_DIM)
            x = x * lax.rsqrt(ms + EPS) * g_ref[...]
        y = _rope128(x, cos, sin)
        o_ref[:, c * LANES:(c + 1) * LANES] = y
        ob_ref[:, c * LANES:(c + 1) * LANES] = y.astype(BF16)


def _norm_rope(src, col0, width, gain, cos, sin, *, name):
    m = src.shape[0]
    tm = _row_tile(m, 512)
    assert col0 % width == 0
    cb = col0 // width
    g = jnp.ones((1, LANES), F32) if gain is None else jnp.tile(gain.reshape(1, HEAD_DIM), (1, LANES // HEAD_DIM))
    return pl.pallas_call(
        functools.partial(_norm_rope_kernel, do_norm=gain is not None, width=width),
        out_shape=(jax.ShapeDtypeStruct((m, width), F32), jax.ShapeDtypeStruct((m, width), BF16)),
        grid=(m // tm,),
        in_specs=[
            pl.BlockSpec((tm, width), lambda i: (i, cb)),
            pl.BlockSpec((1, LANES), lambda i: (0, 0)),
            pl.BlockSpec((tm, LANES), lambda i: (i, 0)),
            pl.BlockSpec((tm, LANES), lambda i: (i, 0)),
        ],
        out_specs=(pl.BlockSpec((tm, width), lambda i: (i, 0)), pl.BlockSpec((tm, width), lambda i: (i, 0))),
        compiler_params=_cparams("parallel"),
        name=name,
    )(src, g, cos, sin)


def _rope_tables(pos):
    half = HEAD_DIM // 2
    inv = ROPE_THETA ** (-jnp.arange(half, dtype=F32) / half)
    ang = pos.astype(F32)[:, None] * inv[None, :]
    cos, sin = jnp.cos(ang), jnp.sin(ang)
    cos128 = jnp.tile(jnp.concatenate([cos, cos], axis=1), (1, LANES // HEAD_DIM))
    sin128 = jnp.tile(jnp.concatenate([-sin, sin], axis=1), (1, LANES // HEAD_DIM))
    return cos128, sin128


def _head_expand(n_heads_pad, width):
    r = lax.broadcasted_iota(I32, (n_heads_pad, width), 0)
    c = lax.broadcasted_iota(I32, (n_heads_pad, width), 1) // A_HEAD_DIM
    return (r == c).astype(BF16)


def _gated_norm(y, xs, z, dskip, gain):
    yz = (y + xs * dskip) * _silu(z)
    gw = yz.shape[1] // A_GROUPS
    parts = []
    for g in range(A_GROUPS):
        p = yz[:, g * gw:(g + 1) * gw]
        parts.append(p * lax.rsqrt(jnp.mean(p * p, axis=-1, keepdims=True) + EPS))
    return jnp.concatenate(parts, axis=1) * gain


def _ssd_seq_kernel(xs_ref, b_ref, c_ref, dt_ref, z_ref, dtb_ref, alog_ref, dskip_ref, gain_ref, s0_ref,
                    ya_ref, sout_ref, st_ref, y_ref, *, n_heads):
    ci = pl.program_id(0)
    q = SSD_CHUNK
    d_inner = xs_ref.shape[1]
    hpg = n_heads // A_GROUPS
    gw = d_inner // A_GROUPS

    @pl.when(ci == 0)
    def _():
        st_ref[...] = s0_ref[...]

    xs = xs_ref[...]
    dt = _softplus(dt_ref[...] + dtb_ref[...])
    a = dt * (-jnp.exp(alog_ref[...]))
    row = lax.broadcasted_iota(I32, (q, q), 0)
    col = lax.broadcasted_iota(I32, (q, q), 1)
    causal = col <= row
    tri = causal.astype(BF16)
    tri_t = (row <= col).astype(BF16)
    acs = _dot3_left(tri, a)
    acs_t = _dot3(a.T, tri_t)
    expand = _head_expand(LANES, d_inner)
    acs_x = _dot3(acs, expand)
    dt_x = _dot3(dt, expand)
    e_acs = jnp.exp(acs_x)
    last = acs_x[q - 1:q, :]
    decay_s = jnp.exp(last - acs_x)
    xdt = xs * dt_x
    xdt_b = xdt.astype(BF16)
    xdec_b = (xdt * decay_s).astype(BF16)
    chunk_decay = e_acs[q - 1:q, :]

    for g in range(A_GROUPS):
        bg = b_ref[:, g * A_STATE:(g + 1) * A_STATE]
        cg = c_ref[:, g * A_STATE:(g + 1) * A_STATE].astype(BF16)
        cb = _dot_nt(cg, bg.astype(BF16))
        st_g = st_ref[g]
        y_off = _dot(cg, st_g.astype(BF16)) * e_acs[:, g * gw:(g + 1) * gw]
        for hh in range(hpg):
            h = g * hpg + hh
            diff = acs[:, h:h + 1] - acs_t[h:h + 1, :]
            m = (cb * jnp.where(causal, jnp.exp(diff), 0.0)).astype(BF16)
            lo = h * A_HEAD_DIM
            y_ref[:, lo:lo + A_HEAD_DIM] = (_dot(m, xdt_b[:, lo:lo + A_HEAD_DIM])
                                            + y_off[:, hh * A_HEAD_DIM:(hh + 1) * A_HEAD_DIM])
        st_ref[g] = st_g * chunk_decay[:, g * gw:(g + 1) * gw] + _dot(bg.T.astype(BF16), xdec_b[:, g * gw:(g + 1) * gw])

    ya_ref[...] = _gated_norm(y_ref[...], xs, z_ref[...], dskip_ref[...], gain_ref[...])

    @pl.when(ci == pl.num_programs(0) - 1)
    def _():
        sout_ref[...] = st_ref[...]


def _ssd_seq(xbc, proj, z_col, dt_col, dt_bias, a_log, d_skip, gain, s0_t, n_heads):
    t = xbc.shape[0]
    q = SSD_CHUNK
    d_inner = n_heads * A_HEAD_DIM
    gn = A_GROUPS * A_STATE
    bcol = d_inner // gn
    pad = lambda v: jnp.pad(v.reshape(1, -1), ((0, 0), (0, LANES - v.shape[-1])))
    dskip_x = jnp.repeat(d_skip, A_HEAD_DIM).reshape(1, d_inner)
    const = lambda shape: pl.BlockSpec(shape, lambda c: (0,) * len(shape))
    return pl.pallas_call(
        functools.partial(_ssd_seq_kernel, n_heads=n_heads),
        out_shape=(jax.ShapeDtypeStruct((t, d_inner), F32), jax.ShapeDtypeStruct(s0_t.shape, F32)),
        grid=(t // q,),
        in_specs=[
            pl.BlockSpec((q, d_inner), lambda c: (c, 0)),
            pl.BlockSpec((q, gn), lambda c: (c, bcol)),
            pl.BlockSpec((q, gn), lambda c: (c, bcol + 1)),
            pl.BlockSpec((q, LANES), lambda c: (c, dt_col // LANES)),
            pl.BlockSpec((q, d_inner), lambda c: (c, z_col // d_inner)),
            const((1, LANES)), const((1, LANES)), const((1, d_inner)), const((1, d_inner)),
            const(s0_t.shape),
        ],
        out_specs=(pl.BlockSpec((q, d_inner), lambda c: (c, 0)), const(s0_t.shape)),
        scratch_shapes=[pltpu.VMEM(s0_t.shape, F32), pltpu.VMEM((q, d_inner), F32)],
        compiler_params=_cparams("arbitrary"),
        name="ssd_seq",
    )(xbc, xbc, xbc, proj, proj, pad(dt_bias), pad(a_log), dskip_x, gain.reshape(1, -1), s0_t)


def _ssd_step_kernel(s_ref, xdt_t_ref, dec_t_ref, b_ref, c_ref, sout_ref, y_ref, *, n_heads):
    b = pl.program_id(0)
    nb = xdt_t_ref.shape[1]
    hp = n_heads * A_HEAD_DIM
    gw = hp // A_GROUPS
    lane = lax.broadcasted_iota(I32, (hp, nb), 1)
    dec = jnp.sum(jnp.where(lane == b, dec_t_ref[...], 0.0), axis=1, keepdims=True)
    rows = lax.broadcasted_iota(I32, (nb, A_STATE), 0)
    s = s_ref[0].reshape(hp, A_STATE)
    xdt_t = xdt_t_ref[...].astype(BF16)
    outs = []
    for g in range(A_GROUPS):
        brow = b_ref[0, :, g * A_STATE:(g + 1) * A_STATE]
        zb = jnp.where(rows == b, jnp.broadcast_to(brow, (nb, A_STATE)), 0.0).astype(BF16)
        upd = _dot(xdt_t[g * gw:(g + 1) * gw, :], zb)
        sn = s[g * gw:(g + 1) * gw, :] * dec[g * gw:(g + 1) * gw, :] + upd
        sout_ref[0, g * (n_heads // A_GROUPS):(g + 1) * (n_heads // A_GROUPS)] = sn.reshape(
            n_heads // A_GROUPS, A_HEAD_DIM, A_STATE)
        crow = c_ref[0, :, g * A_STATE:(g + 1) * A_STATE]
        c8 = jnp.broadcast_to(crow, (SUBLANES, A_STATE)).astype(BF16)
        outs.append(_dot_nt(c8, sn.astype(BF16))[0:1, :])
    y_ref[0] = jnp.concatenate(outs, axis=1)


def _ssd_step(state, xdt_t, dec_t, bmat, cmat):
    nb, n_heads, p, n = state.shape
    hp = n_heads * p
    new_state, y = pl.pallas_call(
        functools.partial(_ssd_step_kernel, n_heads=n_heads),
        out_shape=(jax.ShapeDtypeStruct(state.shape, F32), jax.ShapeDtypeStruct((nb, 1, hp), F32)),
        grid=(nb,),
        in_specs=[
            pl.BlockSpec((1, n_heads, p, n), lambda b: (b, 0, 0, 0)),
            pl.BlockSpec((hp, nb), lambda b: (0, 0)),
            pl.BlockSpec((hp, nb), lambda b: (0, 0)),
            pl.BlockSpec((1, 1, bmat.shape[1]), lambda b: (b, 0, 0)),
            pl.BlockSpec((1, 1, cmat.shape[1]), lambda b: (b, 0, 0)),
        ],
        out_specs=(pl.BlockSpec((1, n_heads, p, n), lambda b: (b, 0, 0, 0)),
                   pl.BlockSpec((1, 1, hp), lambda b: (b, 0, 0))),
        compiler_params=_cparams("arbitrary"),
        name="ssd_step",
    )(state, xdt_t, dec_t, bmat[:, None, :], cmat[:, None, :])
    return new_state, y.reshape(nb, hp)


def _ssd_step_pre_kernel(xs_ref, dt_ref, dtb_ref, alog_ref, xdt_ref, dec_ref):
    dt = _softplus(dt_ref[...] + dtb_ref[...])
    expand = _head_expand(LANES, xs_ref.shape[1])
    xdt_ref[...] = xs_ref[...] * _dot3(dt, expand)
    dec_ref[...] = jnp.exp(_dot3(dt * (-jnp.exp(alog_ref[...])), expand))


def _ssd_step_pre(xbc, proj, dt_col, dt_bias, a_log, d_inner):
    m = xbc.shape[0]
    pad = lambda v: jnp.pad(v.reshape(1, -1), ((0, 0), (0, LANES - v.shape[-1])))
    return pl.pallas_call(
        _ssd_step_pre_kernel,
        out_shape=(jax.ShapeDtypeStruct((m, d_inner), F32), jax.ShapeDtypeStruct((m, d_inner), F32)),
        grid=(1,),
        in_specs=[pl.BlockSpec((m, d_inner), lambda i: (0, 0)),
                  pl.BlockSpec((m, LANES), lambda i: (0, dt_col // LANES)),
                  pl.BlockSpec((1, LANES), lambda i: (0, 0)), pl.BlockSpec((1, LANES), lambda i: (0, 0))],
        out_specs=(pl.BlockSpec((m, d_inner), lambda i: (0, 0)), pl.BlockSpec((m, d_inner), lambda i: (0, 0))),
        compiler_params=_cparams("arbitrary"),
        name="ssd_step_pre",
    )(xbc, proj, pad(dt_bias), pad(a_log))


def _gated_norm_kernel(y_ref, xs_ref, z_ref, dskip_ref, gain_ref, o_ref):
    o_ref[...] = _gated_norm(y_ref[...], xs_ref[...], z_ref[...], dskip_ref[...], gain_ref[...])


def _gated_norm_call(y, xbc, proj, z_col, d_skip, gain):
    m, d_inner = y.shape
    dskip_x = jnp.repeat(d_skip, A_HEAD_DIM).reshape(1, d_inner)
    return pl.pallas_call(
        _gated_norm_kernel,
        out_shape=jax.ShapeDtypeStruct((m, d_inner), F32),
        grid=(1,),
        in_specs=[pl.BlockSpec((m, d_inner), lambda i: (0, 0)),
                  pl.BlockSpec((m, d_inner), lambda i: (0, 0)),
                  pl.BlockSpec((m, d_inner), lambda i: (0, z_col // d_inner)),
                  pl.BlockSpec((1, d_inner), lambda i: (0, 0)), pl.BlockSpec((1, d_inner), lambda i: (0, 0))],
        out_specs=pl.BlockSpec((m, d_inner), lambda i: (0, 0)),
        compiler_params=_cparams("arbitrary"),
        name="gated_norm",
    )(y, xbc, proj, dskip_x, gain.reshape(1, -1))


def _swa_seq_kernel(sink_ref, q_ref, kc_ref, kp_ref, vc_ref, vp_ref, o_ref, *, n_heads, n_kv):
    i = pl.program_id(0)
    w = WINDOW
    grp = n_heads // n_kv
    r = lax.broadcasted_iota(I32, (grp * w, 2 * w), 0) % w
    c = lax.broadcasted_iota(I32, (grp * w, 2 * w), 1)
    ok = (c > r) & (c <= r + w) & ((i > 0) | (c >= w))
    hrow = lax.broadcasted_iota(I32, (grp * w, 1), 0) // w
    scale = HEAD_DIM ** -0.5
    for kv in range(n_kv):
        sl = slice(kv * HEAD_DIM, (kv + 1) * HEAD_DIM)
        kcat = jnp.concatenate([kp_ref[:, sl], kc_ref[:, sl]], axis=0)
        vcat = jnp.concatenate([vp_ref[:, sl], vc_ref[:, sl]], axis=0)
        q4 = jnp.concatenate([q_ref[:, (kv * grp + j) * HEAD_DIM:(kv * grp + j + 1) * HEAD_DIM]
                              for j in range(grp)], axis=0)
        sink = jnp.zeros((grp * w, 1), F32)
        for j in range(grp):
            sink = jnp.where(hrow == j, sink_ref[kv * grp + j], sink)
        s = jnp.where(ok, _dot_nt(q4, kcat) * scale, -jnp.inf)
        m = jnp.maximum(jnp.max(s, axis=1, keepdims=True), sink)
        p = jnp.exp(s - m)
        denom = jnp.sum(p, axis=1, keepdims=True) + jnp.exp(sink - m)
        o = _dot(p.astype(BF16), vcat) / denom
        for j in range(grp):
            h = kv * grp + j
            o_ref[:, h * HEAD_DIM:(h + 1) * HEAD_DIM] = o[j * w:(j + 1) * w, :]


def _swa_seq(q_b, k_b, v_b, sinks, n_heads, n_kv):
    t = q_b.shape[0]
    w = WINDOW
    kvw = n_kv * HEAD_DIM
    cur = lambda i: (i, 0)
    prv = lambda i: (jnp.maximum(i - 1, 0), 0)
    return pl.pallas_call(
        functools.partial(_swa_seq_kernel, n_heads=n_heads, n_kv=n_kv),
        out_shape=jax.ShapeDtypeStruct((t, n_heads * HEAD_DIM), F32),
        grid=(t // w,),
        in_specs=[pl.BlockSpec(memory_space=pltpu.SMEM),
                  pl.BlockSpec((w, n_heads * HEAD_DIM), cur),
                  pl.BlockSpec((w, kvw), cur), pl.BlockSpec((w, kvw), prv),
                  pl.BlockSpec((w, kvw), cur), pl.BlockSpec((w, kvw), prv)],
        out_specs=pl.BlockSpec((w, n_heads * HEAD_DIM), cur),
        compiler_params=_cparams("parallel"),
        name="swa_seq",
    )(sinks, q_b, k_b, k_b, v_b, v_b)


def _swa_step_kernel(sink_ref, qm_ref, knew_ref, vnew_ref, kc_ref, vc_ref, o_ref, ko_ref, vo_ref,
                     *, n_heads, n_kv, bs):
    grp = n_heads // n_kv
    wb = kc_ref.shape[1]
    scale = HEAD_DIM ** -0.5
    hrow = lax.broadcasted_iota(I32, (n_heads, 1), 0)
    sink = jnp.zeros((n_heads, 1), F32)
    for h in range(n_heads):
        sink = jnp.where(hrow == h, sink_ref[h], sink)
    col = lax.broadcasted_iota(I32, (n_heads, wb), 1)
    ok = col > wb - WINDOW
    for bi in range(bs):
        qm = qm_ref[bi]
        kc = kc_ref[bi]
        vc = vc_ref[bi]
        knew = knew_ref[bi]
        vnew = vnew_ref[bi]
        s = jnp.where(ok, _dot_nt(qm.astype(BF16), kc.astype(BF16)) * scale, -jnp.inf)
        s_new = jnp.sum(qm * knew, axis=1, keepdims=True) * scale
        m = jnp.maximum(jnp.maximum(jnp.max(s, axis=1, keepdims=True), s_new), sink)
        p = jnp.exp(s - m)
        p_new = jnp.exp(s_new - m)
        denom = jnp.sum(p, axis=1, keepdims=True) + p_new + jnp.exp(sink - m)
        o = (_dot(p.astype(BF16), vc.astype(BF16)) + p_new * vnew) / denom
        osel = o[:, 0:HEAD_DIM]
        for kv in range(1, n_kv):
            osel = jnp.where(hrow // grp == kv, o[:, kv * HEAD_DIM:(kv + 1) * HEAD_DIM], osel)
        o_ref[bi] = osel
        ko_ref[bi, 0:wb - 1, :] = kc_ref[bi, 1:wb, :]
        ko_ref[bi, wb - 1:wb, :] = knew
        vo_ref[bi, 0:wb - 1, :] = vc_ref[bi, 1:wb, :]
        vo_ref[bi, wb - 1:wb, :] = vnew


def _swa_step(qm, knew, vnew, cache_k, cache_v, sinks, n_heads, n_kv):
    nb, wb, kvw = cache_k.shape
    bs = 8 if nb % 8 == 0 else 1
    blk = lambda shape: pl.BlockSpec((bs,) + shape, lambda b: (b,) + (0,) * len(shape))
    return pl.pallas_call(
        functools.partial(_swa_step_kernel, n_heads=n_heads, n_kv=n_kv, bs=bs),
        out_shape=(jax.ShapeDtypeStruct((nb, n_heads, HEAD_DIM), F32),
                   jax.ShapeDtypeStruct(cache_k.shape, F32), jax.ShapeDtypeStruct(cache_v.shape, F32)),
        grid=(nb // bs,),
        in_specs=[pl.BlockSpec(memory_space=pltpu.SMEM), blk((n_heads, kvw)), blk((1, kvw)), blk((1, kvw)),
                  blk((wb, kvw)), blk((wb, kvw))],
        out_specs=(blk((n_heads, HEAD_DIM)), blk((wb, kvw)), blk((wb, kvw))),
        compiler_params=_cparams("parallel"),
        name="swa_step",
    )(sinks, qm, knew, vnew, cache_k, cache_v)


def _lambda(lam_ref, lam_init):
    lp = lam_ref[...]
    return (jnp.exp(jnp.sum(lp[0:1, :] * lp[1:2, :], axis=1, keepdims=True))
            - jnp.exp(jnp.sum(lp[2:3, :] * lp[3:4, :], axis=1, keepdims=True)) + lam_init)


def _subln(o, subg, lam_init):
    return o * lax.rsqrt(jnp.mean(o * o, axis=-1, keepdims=True) + EPS) * subg * (1.0 - lam_init)


def _diff_seq_kernel(q_ref, k_ref, v_ref, lam_ref, subg_ref, o_ref, m_ref, l_ref, acc_ref,
                     *, n_heads, n_kv, lam_init, tq):
    i = pl.program_id(0)
    grp = n_heads // n_kv
    hd = HEAD_DIM
    vd = 2 * hd
    scale = hd ** -0.5
    m_ref[...] = jnp.full(m_ref.shape, -jnp.inf, F32)
    l_ref[...] = jnp.zeros_like(l_ref)
    acc_ref[...] = jnp.zeros_like(acc_ref)
    row = lax.broadcasted_iota(I32, (grp * tq, tq), 0) % tq
    col = lax.broadcasted_iota(I32, (grp * tq, tq), 1)
    causal = col <= row

    def step(j, masked):
        kb = k_ref[j]
        vb = v_ref[j]
        for kv in range(n_kv):
            for c in range(2):
                idx = kv * 2 + c
                q2 = jnp.concatenate(
                    [q_ref[:, ((kv * grp + g) * 2 + c) * hd:((kv * grp + g) * 2 + c + 1) * hd] for g in range(grp)],
                    axis=0)
                s = _dot_nt(q2, kb[:, idx * hd:(idx + 1) * hd]) * scale
                if masked:
                    s = jnp.where(causal, s, -jnp.inf)
                m_old = m_ref[idx]
                m_new = jnp.maximum(m_old, jnp.max(s, axis=1, keepdims=True))
                alpha = jnp.exp(m_old - m_new)
                p = jnp.exp(s - m_new)
                l_ref[idx] = alpha * l_ref[idx] + jnp.sum(p, axis=1, keepdims=True)
                acc_ref[idx] = alpha * acc_ref[idx] + _dot(p.astype(BF16), vb[:, kv * vd:(kv + 1) * vd])
                m_ref[idx] = m_new

    def body(j, carry):
        step(j, False)
        return carry

    lax.fori_loop(0, i, body, 0)
    step(i, True)
    lam = _lambda(lam_ref, lam_init)
    for kv in range(n_kv):
        o0 = acc_ref[kv * 2] / l_ref[kv * 2]
        o1 = acc_ref[kv * 2 + 1] / l_ref[kv * 2 + 1]
        o = _subln(o0 - lam * o1, subg_ref[...], lam_init)
        for g in range(grp):
            h = kv * grp + g
            o_ref[:, h * vd:(h + 1) * vd] = o[g * tq:(g + 1) * tq, :]


def _resident(shape):
    return pl.BlockSpec(shape, lambda *_: (0,) * len(shape), pipeline_mode=pl.Buffered(1))


def _diff_seq(q_b, k_b, v_b, c_lam, sub_g, lam_init, n_heads, n_kv):
    t = q_b.shape[0]
    tq = min(256, t)
    nb = t // tq
    kw = k_b.shape[1]
    vd = 2 * HEAD_DIM
    grp = n_heads // n_kv
    return pl.pallas_call(
        functools.partial(_diff_seq_kernel, n_heads=n_heads, n_kv=n_kv, lam_init=lam_init, tq=tq),
        out_shape=jax.ShapeDtypeStruct((t, n_heads * vd), F32),
        grid=(nb,),
        in_specs=[pl.BlockSpec((tq, q_b.shape[1]), lambda i: (i, 0)),
                  _resident((nb, tq, kw)), _resident((nb, tq, v_b.shape[1])),
                  pl.BlockSpec(c_lam.shape, lambda i: (0, 0)),
                  pl.BlockSpec((1, vd), lambda i: (0, 0))],
        out_specs=pl.BlockSpec((tq, n_heads * vd), lambda i: (i, 0)),
        scratch_shapes=[pltpu.VMEM((2 * n_kv, grp * tq, 1), F32), pltpu.VMEM((2 * n_kv, grp * tq, 1), F32),
                        pltpu.VMEM((2 * n_kv, grp * tq, vd), F32)],
        compiler_params=_cparams("parallel"),
        name="diff_seq",
    )(q_b, k_b.reshape(nb, tq, kw), v_b.reshape(nb, tq, v_b.shape[1]), c_lam, sub_g.reshape(1, vd))


def _sort_key(score):
    bits = pltpu.bitcast(score, I32)
    bits = jnp.where(score == 0.0, 0, bits)
    return jnp.where(bits < 0, bits ^ 0x7FFFFFFF, bits)


def _count_ge(s_ref, nblk, cand, shape, strict=False):
    rows, width = shape

    def body(kb, acc):
        for cg in range(width // LANES):
            key = s_ref[kb, :, cg * LANES:(cg + 1) * LANES]
            hit = (key > cand) if strict else (key >= cand)
            acc = acc + jnp.where(hit, 1.0, 0.0)
        return acc
    acc = lax.fori_loop(0, nblk, body, jnp.zeros((rows, LANES), F32))
    return jnp.sum(acc, axis=1, keepdims=True)


def _kth_largest(s_ref, nblk, n_sel, shape):
    rows = shape[0]

    def body(_, carry):
        ans, bit = carry
        cand = ans + bit
        cnt = _count_ge(s_ref, nblk, cand, shape)
        return jnp.where(cnt >= n_sel, cand, ans), lax.shift_right_logical(bit, 1)

    ans, _ = lax.fori_loop(0, 32, body, (jnp.full((rows, 1), INT_MIN, I32), jnp.int32(INT_MIN)))
    return ans


def _upper_ones(n):
    r = lax.broadcasted_iota(I32, (n, n), 0)
    c = lax.broadcasted_iota(I32, (n, n), 1)
    return (r <= c).astype(BF16)


def _select_ties(key, kth, need, carry, upper):
    eqf = jnp.where(key == kth, 1.0, 0.0)
    rank = carry + _dot(eqf.astype(BF16), upper)
    sel = (key > kth) | ((key == kth) & (rank <= need))
    return sel, carry + jnp.sum(eqf, axis=1, keepdims=True)


def _dsa_seq_kernel(iq_ref, iw_ref, dq_ref, ik_ref, dk_ref, dv_ref, o_ref, s_ref, m_ref, l_ref, acc_ref,
                    *, n_heads, n_kv, n_idx, n_sel, tq, tk):
    i = pl.program_id(0)
    hd = HEAD_DIM
    grp = n_heads // n_kv
    scale = hd ** -0.5
    nblk = (i * tq + tq + tk - 1) // tk
    last = nblk - 1
    qpos = i * tq + lax.broadcasted_iota(I32, (tq, tk), 0)
    col = lax.broadcasted_iota(I32, (tq, tk), 1)
    iw = iw_ref[...] * (n_idx ** -0.5)

    def idx_scores(kb):
        ikb = ik_ref[kb][:, 0:hd]
        score = None
        for h in range(n_idx):
            sc = jnp.maximum(_dot_nt(iq_ref[:, h * hd:(h + 1) * hd], ikb) * scale, 0.0)
            term = iw[:, h:h + 1] * sc
            score = term if score is None else score + term
        return score

    def fill(kb, carry):
        s_ref[kb] = _sort_key(idx_scores(kb))
        return carry

    lax.fori_loop(0, last, fill, 0)
    causal_last = last * tk + col <= qpos
    s_ref[last] = _sort_key(jnp.where(causal_last, idx_scores(last), -jnp.inf))

    shape = (tq, tk)
    kth = _kth_largest(s_ref, nblk, n_sel, shape)
    n_ge = _count_ge(s_ref, nblk, kth, shape)
    n_gt = _count_ge(s_ref, nblk, kth, shape, strict=True)
    need = n_sel - n_gt
    ties = jnp.max(n_ge) > n_sel

    m_ref[...] = jnp.full(m_ref.shape, -jnp.inf, F32)
    l_ref[...] = jnp.zeros_like(l_ref)
    acc_ref[...] = jnp.zeros_like(acc_ref)

    def attend(kb, sel):
        dkb = dk_ref[kb]
        dvb = dv_ref[kb]
        for h in range(n_heads):
            kv = h // grp
            s = _dot_nt(dq_ref[:, h * hd:(h + 1) * hd], dkb[:, kv * hd:(kv + 1) * hd]) * scale
            s = jnp.where(sel, s, -jnp.inf)
            m_old = m_ref[h]
            m_new = jnp.maximum(m_old, jnp.max(s, axis=1, keepdims=True))
            m_safe = jnp.where(m_new == -jnp.inf, 0.0, m_new)
            alpha = jnp.exp(m_old - m_safe)
            p = jnp.exp(s - m_safe)
            l_ref[h] = alpha * l_ref[h] + jnp.sum(p, axis=1, keepdims=True)
            acc_ref[h] = alpha * acc_ref[h] + _dot(p.astype(BF16), dvb[:, kv * hd:(kv + 1) * hd])
            m_ref[h] = m_new

    @pl.when(jnp.logical_not(ties))
    def _():
        def body(kb, carry):
            attend(kb, s_ref[kb] >= kth)
            return carry
        lax.fori_loop(0, last, body, 0)
        attend(last, (s_ref[last] >= kth) & causal_last)

    @pl.when(ties)
    def _():
        upper = _upper_ones(tk)

        def body(kb, carry):
            sel, carry = _select_ties(s_ref[kb], kth, need, carry, upper)
            attend(kb, sel)
            return carry
        carry = lax.fori_loop(0, last, body, jnp.zeros((tq, 1), F32))
        sel, _ = _select_ties(s_ref[last], kth, need, carry, upper)
        attend(last, sel & causal_last)

    for h in range(n_heads):
        o_ref[:, h * hd:(h + 1) * hd] = acc_ref[h] / l_ref[h]


def _dsa_seq(iq_b, iw, dq_b, ik_b, dk_b, dv_b, n_heads, n_kv, n_idx, n_sel):
    t = iq_b.shape[0]
    tq = min(128, t)
    tk = min(256, t)
    nkb = t // tk
    kvw = n_kv * HEAD_DIM
    return pl.pallas_call(
        functools.partial(_dsa_seq_kernel, n_heads=n_heads, n_kv=n_kv, n_idx=n_idx, n_sel=n_sel, tq=tq, tk=tk),
        out_shape=jax.ShapeDtypeStruct((t, n_heads * HEAD_DIM), F32),
        grid=(t // tq,),
        in_specs=[pl.BlockSpec((tq, iq_b.shape[1]), lambda i: (i, 0)),
                  pl.BlockSpec((tq, LANES), lambda i: (i, 0)),
                  pl.BlockSpec((tq, dq_b.shape[1]), lambda i: (i, 0)),
                  _resident((nkb, tk, LANES)), _resident((nkb, tk, kvw)), _resident((nkb, tk, kvw))],
        out_specs=pl.BlockSpec((tq, n_heads * HEAD_DIM), lambda i: (i, 0)),
        scratch_shapes=[pltpu.VMEM((nkb, tq, tk), I32),
                        pltpu.VMEM((n_heads, tq, 1), F32), pltpu.VMEM((n_heads, tq, 1), F32),
                        pltpu.VMEM((n_heads, tq, HEAD_DIM), F32)],
        compiler_params=_cparams("parallel"),
        name="dsa_seq",
    )(iq_b, iw, dq_b, ik_b.reshape(nkb, tk, LANES), dk_b.reshape(nkb, tk, kvw), dv_b.reshape(nkb, tk, kvw))


def _page_specs(n, width, layer, n_pages, ch):
    def spec(p):
        return pl.BlockSpec((None, None, PAGE, width),
                            lambda b, c, pt: (layer, pt[b * n_pages + c * ch + p], 0, 0))
    return [spec(p) for p in range(n)]


def _paged_attn_kernel(pt_ref, *refs, ch, mode, has_mask, lam_init, n_heads, n_kv):
    pos = 0
    qm_ref, knew_ref, vnew_ref, selnew_ref = refs[0:4]
    pos = 4
    mask_ref = None
    if has_mask:
        mask_ref = refs[pos]
        pos += 1
    lam_ref = subg_ref = None
    if mode == "diff":
        lam_ref, subg_ref = refs[pos], refs[pos + 1]
        pos += 2
    k_refs = refs[pos:pos + ch]
    v_refs = refs[pos + ch:pos + 2 * ch]
    o_ref = refs[pos + 2 * ch]
    m_ref, l_ref, acc_ref = refs[pos + 2 * ch + 1:pos + 2 * ch + 4]
    c = pl.program_id(1)
    scale = HEAD_DIM ** -0.5

    @pl.when(c == 0)
    def _():
        m_ref[...] = jnp.full(m_ref.shape, -jnp.inf, F32)
        l_ref[...] = jnp.zeros_like(l_ref)
        acc_ref[...] = jnp.zeros_like(acc_ref)

    qm = qm_ref[0]
    qb = qm.astype(BF16)
    s = jnp.concatenate([_dot_nt(qb, k_refs[p][...].astype(BF16)) for p in range(ch)], axis=1) * scale
    if has_mask:
        s = jnp.where(mask_ref[0] > 0.0, s, -jnp.inf)
    m_old = m_ref[...]
    m_new = jnp.maximum(m_old, jnp.max(s, axis=1, keepdims=True))
    m_safe = jnp.where(m_new == -jnp.inf, 0.0, m_new)
    alpha = jnp.exp(m_old - m_safe)
    p = jnp.exp(s - m_safe).astype(BF16)
    l_ref[...] = alpha * l_ref[...] + jnp.sum(p.astype(F32), axis=1, keepdims=True)
    pv = _dot(p[:, 0:PAGE], v_refs[0][...].astype(BF16))
    for j in range(1, ch):
        pv = pv + _dot(p[:, j * PAGE:(j + 1) * PAGE], v_refs[j][...].astype(BF16))
    acc_ref[...] = alpha * acc_ref[...] + pv
    m_ref[...] = m_new

    @pl.when(c == pl.num_programs(1) - 1)
    def _():
        s_new = jnp.sum(qm * knew_ref[0], axis=1, keepdims=True) * scale
        s_new = jnp.where(selnew_ref[0][:, 0:1] > 0.0, s_new, -jnp.inf)
        m_o = m_ref[...]
        m_f = jnp.maximum(m_o, s_new)
        m_s = jnp.where(m_f == -jnp.inf, 0.0, m_f)
        al = jnp.exp(m_o - m_s)
        p_new = jnp.exp(s_new - m_s)
        l = al * l_ref[...] + p_new
        o = (al * acc_ref[...] + p_new * vnew_ref[0]) / l
        rows = o.shape[0]
        r = lax.broadcasted_iota(I32, (rows, 1), 0)
        if mode == "diff":
            vd = 2 * HEAD_DIM
            grp = n_heads // n_kv
            kv_of = (r % n_heads) // grp
            osel = o[:, 0:vd]
            for kv in range(1, n_kv):
                osel = jnp.where(kv_of == kv, o[:, kv * vd:(kv + 1) * vd], osel)
            lam = _lambda(lam_ref, lam_init)
            od = osel[0:n_heads, :] - lam * osel[n_heads:2 * n_heads, :]
            o_ref[0] = _subln(od, subg_ref[...], lam_init)
        else:
            grp = n_heads // n_kv
            kv_of = r // grp
            osel = o[:, 0:HEAD_DIM]
            for kv in range(1, n_kv):
                osel = jnp.where(kv_of == kv, o[:, kv * HEAD_DIM:(kv + 1) * HEAD_DIM], osel)
            o_ref[0] = osel


def _paged_attn(page_table, layer, qm, knew, vnew, selnew, mask, cache_k, cache_v, *, mode, n_heads, n_kv,
                lam=None, sub_g=None, lam_init=0.0):
    nb, n_pages = page_table.shape
    rows, dk = qm.shape[1:]
    dv = cache_v.shape[-1]
    ch = min(16, n_pages)
    nc = n_pages // ch
    has_mask = mask is not None
    per_b = lambda shape: pl.BlockSpec((1,) + shape, lambda b, c, pt: (b,) + (0,) * len(shape))
    in_specs = [per_b((rows, dk)), per_b((1, dk)), per_b((1, dv)), per_b((1, LANES))]
    args = [qm, knew, vnew, selnew]
    if has_mask:
        in_specs.append(pl.BlockSpec((1, 1, ch * PAGE), lambda b, c, pt: (b, 0, c)))
        args.append(mask)
    if mode == "diff":
        in_specs += [pl.BlockSpec(lam.shape, lambda b, c, pt: (0, 0)),
                     pl.BlockSpec((1, 2 * HEAD_DIM), lambda b, c, pt: (0, 0))]
        args += [lam, sub_g.reshape(1, -1)]
        out_rows, out_w = n_heads, 2 * HEAD_DIM
    else:
        out_rows, out_w = n_heads, HEAD_DIM
    in_specs += _page_specs(ch, dk, layer, n_pages, ch) + _page_specs(ch, dv, layer, n_pages, ch)
    args += [cache_k] * ch + [cache_v] * ch
    grid_spec = pltpu.PrefetchScalarGridSpec(
        num_scalar_prefetch=1, grid=(nb, nc), in_specs=in_specs,
        out_specs=pl.BlockSpec((1, out_rows, out_w), lambda b, c, pt: (b, 0, 0)),
        scratch_shapes=[pltpu.VMEM((rows, 1), F32), pltpu.VMEM((rows, 1), F32), pltpu.VMEM((rows, dv), F32)])
    return pl.pallas_call(
        functools.partial(_paged_attn_kernel, ch=ch, mode=mode, has_mask=has_mask, lam_init=lam_init,
                          n_heads=n_heads, n_kv=n_kv),
        out_shape=jax.ShapeDtypeStruct((nb, out_rows, out_w), F32),
        grid_spec=grid_spec,
        compiler_params=_cparams("parallel", "arbitrary"),
        name="paged_attn_" + mode,
    )(page_table.reshape(-1), *args)


def _paged_idx_kernel(pt_ref, iq_ref, w_ref, *refs, ch, n_idx):
    pages = refs[:ch]
    o_ref = refs[ch]
    scale = HEAD_DIM ** -0.5
    iq = iq_ref[0]
    w = w_ref[0] * (n_idx ** -0.5)
    outs = []
    for p in range(ch):
        sc = jnp.maximum(_dot_nt(iq, pages[p][...].astype(BF16)) * scale, 0.0)
        outs.append(jnp.sum(w * sc, axis=0, keepdims=True))
    o_ref[0] = jnp.concatenate(outs, axis=1)


def _paged_idx_scores(page_table, layer, iq, w, cache_idx, n_idx):
    nb, n_pages = page_table.shape
    ch = min(16, n_pages)
    nc = n_pages // ch
    grid_spec = pltpu.PrefetchScalarGridSpec(
        num_scalar_prefetch=1, grid=(nb, nc),
        in_specs=[pl.BlockSpec((1, n_idx, HEAD_DIM), lambda b, c, pt: (b, 0, 0)),
                  pl.BlockSpec((1, n_idx, 1), lambda b, c, pt: (b, 0, 0))]
        + _page_specs(ch, cache_idx.shape[-1], layer, n_pages, ch),
        out_specs=pl.BlockSpec((1, 1, ch * PAGE), lambda b, c, pt: (b, 0, c)))
    return pl.pallas_call(
        functools.partial(_paged_idx_kernel, ch=ch, n_idx=n_idx),
        out_shape=jax.ShapeDtypeStruct((nb, 1, n_pages * PAGE), F32),
        grid_spec=grid_spec,
        compiler_params=_cparams("parallel", "arbitrary"),
        name="paged_idx_scores",
    )(page_table.reshape(-1), iq, w, *([cache_idx] * ch))


def _topk_mask_kernel(sc_ref, iq_ref, ik_ref, w_ref, mask_ref, mnew_ref, s_ref, *, n_idx, n_sel, bw):
    nb, t = sc_ref.shape
    nblk = t // bw
    scale = HEAD_DIM ** -0.5
    prod = iq_ref[...].astype(F32) * ik_ref[...].astype(F32)
    seg_r = lax.broadcasted_iota(I32, (prod.shape[1], LANES), 0) // HEAD_DIM
    seg_c = lax.broadcasted_iota(I32, (prod.shape[1], LANES), 1)
    qk = _dot3(prod, (seg_r == seg_c).astype(BF16))
    s_new = jnp.sum(w_ref[...] * (n_idx ** -0.5) * jnp.maximum(qk * scale, 0.0), axis=1, keepdims=True)
    lane = lax.broadcasted_iota(I32, (nb, bw), 1)
    for kb in range(nblk):
        s_ref[kb] = _sort_key(sc_ref[:, kb * bw:(kb + 1) * bw])
    s_ref[nblk] = _sort_key(jnp.where(lane == 0, s_new, -jnp.inf))
    shape = (nb, bw)
    kth = _kth_largest(s_ref, nblk + 1, n_sel, shape)
    need = n_sel - _count_ge(s_ref, nblk + 1, kth, shape, strict=True)
    upper = _upper_ones(bw)
    carry = jnp.zeros((nb, 1), F32)
    for kb in range(nblk):
        sel, carry = _select_ties(s_ref[kb], kth, need, carry, upper)
        mask_ref[:, kb * bw:(kb + 1) * bw] = jnp.where(sel, 1.0, 0.0)
    sel, _ = _select_ties(s_ref[nblk], kth, need, carry, upper)
    mnew_ref[...] = jnp.where(sel & (lane == 0), 1.0, 0.0)[:, 0:LANES]


def _topk_mask(scores, iq_b, ik_new_b, w, n_idx, n_sel):
    nb, t = scores.shape
    bw = min(256, t)
    full = lambda a: pl.BlockSpec(a.shape, lambda i: (0,) * a.ndim)
    return pl.pallas_call(
        functools.partial(_topk_mask_kernel, n_idx=n_idx, n_sel=n_sel, bw=bw),
        out_shape=(jax.ShapeDtypeStruct((nb, t), F32), jax.ShapeDtypeStruct((nb, LANES), F32)),
        grid=(1,),
        in_specs=[full(scores), full(iq_b), full(ik_new_b), full(w)],
        out_specs=(pl.BlockSpec((nb, t), lambda i: (0, 0)), pl.BlockSpec((nb, LANES), lambda i: (0, 0))),
        scratch_shapes=[pltpu.VMEM((t // bw + 1, nb, bw), I32)],
        compiler_params=_cparams("arbitrary"),
        name="topk_mask",
    )(scores, iq_b, ik_new_b, w)


def _keys_on_lanes(k_b, n_slices, tk):
    t = k_b.shape[0]
    return k_b.reshape(t // tk, tk, n_slices, HEAD_DIM).transpose(2, 0, 3, 1)


def _online_softmax_step(s, m_ref, l_ref, acc_ref, idx, v_blk):
    ncol = s.shape[1] // LANES
    cols = lambda a: [a[:, c * LANES:(c + 1) * LANES] for c in range(ncol)]
    m_old = m_ref[idx]
    m_new = jnp.maximum(m_old, jnp.max(functools.reduce(jnp.maximum, cols(s)), axis=1, keepdims=True))
    m_safe = jnp.where(m_new == -jnp.inf, 0.0, m_new)
    alpha = jnp.exp(m_old - m_safe)
    p = jnp.exp(s - m_safe)
    l_ref[idx] = alpha * l_ref[idx] + jnp.sum(functools.reduce(jnp.add, cols(p)), axis=1, keepdims=True)
    acc_ref[idx] = alpha * acc_ref[idx] + _dot(p.astype(BF16), v_blk)
    m_ref[idx] = m_new


def _diff2_kernel(q_ref, kt_ref, v_ref, lam_ref, subg_ref, o_ref, qs_ref, m_ref, l_ref, acc_ref,
                  *, n_heads, n_kv, lam_init, tq, tk):
    i = pl.program_id(0)
    grp = n_heads // n_kv
    hd = HEAD_DIM
    vd = 2 * hd
    scale = hd ** -0.5
    m_ref[...] = jnp.full(m_ref.shape, -jnp.inf, F32)
    l_ref[...] = jnp.zeros_like(l_ref)
    acc_ref[...] = jnp.zeros_like(acc_ref)
    for kv in range(n_kv):
        for c in range(2):
            qs_ref[kv * 2 + c] = jnp.concatenate(
                [q_ref[:, ((kv * grp + g) * 2 + c) * hd:((kv * grp + g) * 2 + c + 1) * hd] for g in range(grp)],
                axis=0) * scale
    n_full = (i * tq) // tk

    def step(j, masked):
        vb = v_ref[j]
        if masked:
            qpos = i * tq + lax.broadcasted_iota(I32, (grp, tq, tk), 1)
            kpos = j * tk + lax.broadcasted_iota(I32, (grp, tq, tk), 2)
            ok = (kpos <= qpos).reshape(grp * tq, tk)
        for kv in range(n_kv):
            for c in range(2):
                idx = kv * 2 + c
                s = _dot(qs_ref[idx], kt_ref[idx, j])
                if masked:
                    s = jnp.where(ok, s, -jnp.inf)
                _online_softmax_step(s, m_ref, l_ref, acc_ref, idx, vb[:, kv * vd:(kv + 1) * vd])

    def body(j, carry):
        step(j, False)
        return carry

    lax.fori_loop(0, n_full, body, 0)
    step(n_full, True)
    lam = _lambda(lam_ref, lam_init)
    for kv in range(n_kv):
        o0 = acc_ref[kv * 2] / l_ref[kv * 2]
        o1 = acc_ref[kv * 2 + 1] / l_ref[kv * 2 + 1]
        o = _subln(o0 - lam * o1, subg_ref[...], lam_init)
        for g in range(grp):
            h = kv * grp + g
            o_ref[:, h * vd:(h + 1) * vd] = o[g * tq:(g + 1) * tq, :]


def _diff_seq2(q_b, k_b, v_b, c_lam, sub_g, lam_init, n_heads, n_kv):
    t = q_b.shape[0]
    tq = min(256, t)
    tk = min(512, t)
    nkb = t // tk
    vd = 2 * HEAD_DIM
    grp = n_heads // n_kv
    kt = _keys_on_lanes(k_b, 2 * n_kv, tk)
    return pl.pallas_call(
        functools.partial(_diff2_kernel, n_heads=n_heads, n_kv=n_kv, lam_init=lam_init, tq=tq, tk=tk),
        out_shape=jax.ShapeDtypeStruct((t, n_heads * vd), F32),
        grid=(t // tq,),
        in_specs=[pl.BlockSpec((tq, q_b.shape[1]), lambda i: (i, 0)),
                  _resident(kt.shape), _resident((nkb, tk, v_b.shape[1])),
                  pl.BlockSpec(c_lam.shape, lambda i: (0, 0)),
                  pl.BlockSpec((1, vd), lambda i: (0, 0))],
        out_specs=pl.BlockSpec((tq, n_heads * vd), lambda i: (i, 0)),
        scratch_shapes=[pltpu.VMEM((2 * n_kv, grp * tq, HEAD_DIM), BF16),
                        pltpu.VMEM((2 * n_kv, grp * tq, 1), F32), pltpu.VMEM((2 * n_kv, grp * tq, 1), F32),
                        pltpu.VMEM((2 * n_kv, grp * tq, vd), F32)],
        compiler_params=_cparams("parallel"),
        name="diff_seq",
    )(q_b, kt, v_b.reshape(nkb, tk, v_b.shape[1]), c_lam, sub_g.reshape(1, vd))


def _dsa2_kernel(iq_ref, iw_ref, dq_ref, ikt_ref, dkt_ref, dv_ref, o_ref,
                 s_ref, iqs_ref, wrep_ref, qs_ref, m_ref, l_ref, acc_ref,
                 *, n_heads, n_kv, n_idx, n_sel, tq, tk):
    i = pl.program_id(0)
    hd = HEAD_DIM
    grp = n_heads // n_kv
    scale = hd ** -0.5
    nblk = (i * tq + tq + tk - 1) // tk
    last = nblk - 1
    ncol = tk // LANES

    iw = iw_ref[...] * ((n_idx ** -0.5) * scale)
    for h in range(n_idx):
        iqs_ref[h] = iq_ref[:, h * hd:(h + 1) * hd]
        wrep_ref[h] = jnp.broadcast_to(iw[:, h:h + 1], (tq, LANES))
    for kv in range(n_kv):
        qs_ref[kv] = jnp.concatenate(
            [dq_ref[:, (kv * grp + g) * hd:(kv * grp + g + 1) * hd] for g in range(grp)], axis=0) * scale

    def idx_scores(kb):
        ikb = ikt_ref[kb]
        score = None
        for h in range(n_idx):
            w = jnp.concatenate([wrep_ref[h]] * ncol, axis=1)
            term = jnp.maximum(_dot(iqs_ref[h], ikb), 0.0) * w
            score = term if score is None else score + term
        return score

    def fill(kb, carry):
        s_ref[kb] = _sort_key(idx_scores(kb))
        return carry

    lax.fori_loop(0, last, fill, 0)
    qpos = i * tq + lax.broadcasted_iota(I32, (tq, tk), 0)
    causal_last = last * tk + lax.broadcasted_iota(I32, (tq, tk), 1) <= qpos
    s_ref[last] = _sort_key(jnp.where(causal_last, idx_scores(last), -jnp.inf))

    shape = (tq, tk)

    def bis_cond(c):
        it, _, _, _, done = c
        return jnp.logical_and(it < 32, jnp.logical_not(done))

    def bis_body(c):
        it, ans, bit, cnt_ans, _ = c
        cand = ans + bit
        cnt = _count_ge(s_ref, nblk, cand, shape)
        take = cnt >= n_sel
        ans = jnp.where(take, cand, ans)
        cnt_ans = jnp.where(take, cnt, cnt_ans)
        done = jnp.min(jnp.where(cnt_ans == n_sel, 1.0, 0.0)) > 0.5
        return it + 1, ans, lax.shift_right_logical(bit, 1), cnt_ans, done

    init = (jnp.int32(0), jnp.full((tq, 1), INT_MIN, I32), jnp.int32(INT_MIN),
            jnp.full((tq, 1), 1.0, F32) * (nblk * tk).astype(F32), jnp.bool_(False))
    _, kth, _, n_ge, _ = lax.while_loop(bis_cond, bis_body, init)
    ties = jnp.max(n_ge) > n_sel

    m_ref[...] = jnp.full(m_ref.shape, -jnp.inf, F32)
    l_ref[...] = jnp.zeros_like(l_ref)
    acc_ref[...] = jnp.zeros_like(acc_ref)

    def attend(kb, sel):
        dvb = dv_ref[kb]
        for kv in range(n_kv):
            s = _dot(qs_ref[kv], dkt_ref[kv, kb]).reshape(grp, tq, tk)
            s = jnp.where(sel[None], s, -jnp.inf).reshape(grp * tq, tk)
            _online_softmax_step(s, m_ref, l_ref, acc_ref, kv, dvb)

    @pl.when(jnp.logical_not(ties))
    def _():
        def body(kb, carry):
            attend(kb, s_ref[kb] >= kth)
            return carry
        lax.fori_loop(0, last, body, 0)
        attend(last, (s_ref[last] >= kth) & causal_last)

    @pl.when(ties)
    def _():
        need = n_sel - _count_ge(s_ref, nblk, kth, shape, strict=True)
        upper = _upper_ones(tk)

        def body(kb, carry):
            sel, carry = _select_ties(s_ref[kb], kth, need, carry, upper)
            attend(kb, sel)
            return carry
        carry = lax.fori_loop(0, last, body, jnp.zeros((tq, 1), F32))
        sel, _ = _select_ties(s_ref[last], kth, need, carry, upper)
        attend(last, sel & causal_last)

    for kv in range(n_kv):
        o = acc_ref[kv] / l_ref[kv]
        for g in range(grp):
            h = kv * grp + g
            o_ref[:, h * hd:(h + 1) * hd] = o[g * tq:(g + 1) * tq, kv * hd:(kv + 1) * hd]


def _dsa_seq2(iq_b, iw, dq_b, ik_b, dk_b, dv_b, n_heads, n_kv, n_idx, n_sel):
    t = iq_b.shape[0]
    tq = min(128, t)
    tk = min(512, t)
    nkb = t // tk
    kvw = n_kv * HEAD_DIM
    grp = n_heads // n_kv
    ikt = _keys_on_lanes(ik_b[:, :HEAD_DIM], 1, tk)[0]
    dkt = _keys_on_lanes(dk_b, n_kv, tk)
    return pl.pallas_call(
        functools.partial(_dsa2_kernel, n_heads=n_heads, n_kv=n_kv, n_idx=n_idx, n_sel=n_sel, tq=tq, tk=tk),
        out_shape=jax.ShapeDtypeStruct((t, n_heads * HEAD_DIM), F32),
        grid=(t // tq,),
        in_specs=[pl.BlockSpec((tq, iq_b.shape[1]), lambda i: (i, 0)),
                  pl.BlockSpec((tq, LANES), lambda i: (i, 0)),
                  pl.BlockSpec((tq, dq_b.shape[1]), lambda i: (i, 0)),
                  _resident(ikt.shape), _resident(dkt.shape), _resident((nkb, tk, kvw))],
        out_specs=pl.BlockSpec((tq, n_heads * HEAD_DIM), lambda i: (i, 0)),
        scratch_shapes=[pltpu.VMEM((nkb, tq, tk), I32),
                        pltpu.VMEM((n_idx, tq, HEAD_DIM), BF16), pltpu.VMEM((n_idx, tq, LANES), F32),
                        pltpu.VMEM((n_kv, grp * tq, HEAD_DIM), BF16),
                        pltpu.VMEM((n_kv, grp * tq, 1), F32), pltpu.VMEM((n_kv, grp * tq, 1), F32),
                        pltpu.VMEM((n_kv, grp * tq, kvw), F32)],
        compiler_params=_cparams("parallel"),
        name="dsa_seq",
    )(iq_b, iw, dq_b, ikt, dkt, dv_b.reshape(nkb, tk, kvw))


def _page_specs2(n, blk, layer, n_pages, ch):
    def spec(p):
        return pl.BlockSpec((None, None) + blk,
                            lambda b, c, pt: (layer, pt[b * n_pages + c * ch + p]) + (0,) * len(blk))
    return [spec(p) for p in range(n)]


def _paged2_kernel(pt_ref, *refs, ch, mode, has_mask, lam_init, n_heads, n_kv):
    qm_ref, knew_ref, vnew_ref, selnew_ref = refs[0:4]
    pos = 4
    mask_ref = None
    if has_mask:
        mask_ref = refs[pos]
        pos += 1
    lam_ref = subg_ref = None
    if mode == "diff":
        lam_ref, subg_ref = refs[pos], refs[pos + 1]
        pos += 2
    k_refs = refs[pos:pos + ch]
    v_refs = refs[pos + ch:pos + 2 * ch]
    o_ref = refs[pos + 2 * ch]
    m_ref, l_ref, acc_ref = refs[pos + 2 * ch + 1:pos + 2 * ch + 4]
    c = pl.program_id(1)
    scale = HEAD_DIM ** -0.5
    dk = qm_ref.shape[2]

    @pl.when(c == 0)
    def _():
        m_ref[...] = jnp.full(m_ref.shape, -jnp.inf, F32)
        l_ref[...] = jnp.zeros_like(l_ref)
        acc_ref[...] = jnp.zeros_like(acc_ref)

    qm = qm_ref[0]
    qb = qm.astype(BF16)
    s = jnp.concatenate([_dot(qb, k_refs[p][...].reshape(dk, PAGE).astype(BF16)) for p in range(ch)],
                        axis=1) * scale
    if has_mask:
        s = jnp.where(mask_ref[0] > 0.0, s, -jnp.inf)
    m_old = m_ref[...]
    m_new = jnp.maximum(m_old, jnp.max(s, axis=1, keepdims=True))
    m_safe = jnp.where(m_new == -jnp.inf, 0.0, m_new)
    alpha = jnp.exp(m_old - m_safe)
    p = jnp.exp(s - m_safe).astype(BF16)
    l_ref[...] = alpha * l_ref[...] + jnp.sum(p.astype(F32), axis=1, keepdims=True)

    def pv_of(j):
        pj = p[:, j * PAGE:(j + 1) * PAGE]
        if mode == "diff":
            return jnp.concatenate(
                [_dot(pj, v_refs[j][pl.ds(kv, PAGE, stride=n_kv), :].astype(BF16)) for kv in range(n_kv)], axis=1)
        return _dot_nt(pj, v_refs[j][...].reshape(-1, PAGE).astype(BF16))

    pv = pv_of(0)
    for j in range(1, ch):
        pv = pv + pv_of(j)
    acc_ref[...] = alpha * acc_ref[...] + pv
    m_ref[...] = m_new

    @pl.when(c == pl.num_programs(1) - 1)
    def _():
        s_new = jnp.sum(qm * knew_ref[0], axis=1, keepdims=True) * scale
        s_new = jnp.where(selnew_ref[0][:, 0:1] > 0.0, s_new, -jnp.inf)
        m_o = m_ref[...]
        m_f = jnp.maximum(m_o, s_new)
        m_s = jnp.where(m_f == -jnp.inf, 0.0, m_f)
        al = jnp.exp(m_o - m_s)
        p_new = jnp.exp(s_new - m_s)
        l = al * l_ref[...] + p_new
        o = (al * acc_ref[...] + p_new * vnew_ref[0]) / l
        rows = o.shape[0]
        r = lax.broadcasted_iota(I32, (rows, 1), 0)
        grp = n_heads // n_kv
        if mode == "diff":
            vd = 2 * HEAD_DIM
            kv_of = (r % n_heads) // grp
            osel = o[:, 0:vd]
            for kv in range(1, n_kv):
                osel = jnp.where(kv_of == kv, o[:, kv * vd:(kv + 1) * vd], osel)
            lam = _lambda(lam_ref, lam_init)
            od = osel[0:n_heads, :] - lam * osel[n_heads:2 * n_heads, :]
            o_ref[0] = _subln(od, subg_ref[...], lam_init)
        else:
            kv_of = r // grp
            osel = o[:, 0:HEAD_DIM]
            for kv in range(1, n_kv):
                osel = jnp.where(kv_of == kv, o[:, kv * HEAD_DIM:(kv + 1) * HEAD_DIM], osel)
            o_ref[0] = osel


def _paged_attn2(page_table, layer, qm, knew, vnew, selnew, mask, cache_kt, cache_v, *, mode, n_heads, n_kv,
                 lam=None, sub_g=None, lam_init=0.0):
    nb, n_pages = page_table.shape
    rows, dk = qm.shape[1:]
    dv = vnew.shape[-1]
    ch = min(16, n_pages)
    nc = n_pages // ch
    has_mask = mask is not None
    per_b = lambda shape: pl.BlockSpec((1,) + shape, lambda b, c, pt: (b,) + (0,) * len(shape))
    in_specs = [per_b((rows, dk)), per_b((1, dk)), per_b((1, dv)), per_b((1, LANES))]
    args = [qm, knew, vnew, selnew]
    if has_mask:
        in_specs.append(pl.BlockSpec((1, 1, ch * PAGE), lambda b, c, pt: (b, 0, c)))
        args.append(mask)
    if mode == "diff":
        in_specs += [pl.BlockSpec(lam.shape, lambda b, c, pt: (0, 0)),
                     pl.BlockSpec((1, 2 * HEAD_DIM), lambda b, c, pt: (0, 0))]
        args += [lam, sub_g.reshape(1, -1)]
        out_w = 2 * HEAD_DIM
    else:
        out_w = HEAD_DIM
    in_specs += (_page_specs2(ch, cache_kt.shape[2:], layer, n_pages, ch)
                 + _page_specs2(ch, cache_v.shape[2:], layer, n_pages, ch))
    args += [cache_kt] * ch + [cache_v] * ch
    grid_spec = pltpu.PrefetchScalarGridSpec(
        num_scalar_prefetch=1, grid=(nb, nc), in_specs=in_specs,
        out_specs=pl.BlockSpec((1, n_heads, out_w), lambda b, c, pt: (b, 0, 0)),
        scratch_shapes=[pltpu.VMEM((rows, 1), F32), pltpu.VMEM((rows, 1), F32), pltpu.VMEM((rows, dv), F32)])
    return pl.pallas_call(
        functools.partial(_paged2_kernel, ch=ch, mode=mode, has_mask=has_mask, lam_init=lam_init,
                          n_heads=n_heads, n_kv=n_kv),
        out_shape=jax.ShapeDtypeStruct((nb, n_heads, out_w), F32),
        grid_spec=grid_spec,
        compiler_params=_cparams("parallel", "arbitrary"),
        name="paged_attn_" + mode,
    )(page_table.reshape(-1), *args)


def _paged_idx2_kernel(pt_ref, iq_ref, w_ref, *refs, ch, n_idx):
    pages = refs[:ch]
    o_ref = refs[ch]
    scale = HEAD_DIM ** -0.5
    iq = iq_ref[0]
    w = w_ref[0] * (n_idx ** -0.5)
    outs = []
    for p in range(ch):
        sc = jnp.maximum(_dot(iq, pages[p][...].astype(BF16)) * scale, 0.0)
        outs.append(jnp.sum(w * sc, axis=0, keepdims=True))
    o_ref[0] = jnp.concatenate(outs, axis=1)


def _paged_idx_scores2(page_table, layer, iq, w, cache_it, n_idx):
    nb, n_pages = page_table.shape
    ch = min(16, n_pages)
    nc = n_pages // ch
    grid_spec = pltpu.PrefetchScalarGridSpec(
        num_scalar_prefetch=1, grid=(nb, nc),
        in_specs=[pl.BlockSpec((1, n_idx, HEAD_DIM), lambda b, c, pt: (b, 0, 0)),
                  pl.BlockSpec((1, n_idx, 1), lambda b, c, pt: (b, 0, 0))]
        + _page_specs2(ch, cache_it.shape[2:], layer, n_pages, ch),
        out_specs=pl.BlockSpec((1, 1, ch * PAGE), lambda b, c, pt: (b, 0, c)))
    return pl.pallas_call(
        functools.partial(_paged_idx2_kernel, ch=ch, n_idx=n_idx),
        out_shape=jax.ShapeDtypeStruct((nb, 1, n_pages * PAGE), F32),
        grid_spec=grid_spec,
        compiler_params=_cparams("parallel", "arbitrary"),
        name="paged_idx_scores",
    )(page_table.reshape(-1), iq, w, *([cache_it] * ch))


def _bf(w):
    return w.astype(BF16)


def _place(q3, slot_of_row, n_slots):
    onehot = (np.asarray(slot_of_row)[:, None] == np.arange(n_slots)[None, :]).astype(np.float32)
    out = q3[:, :, None, :] * jnp.asarray(onehot)[None, :, :, None]
    return out.reshape(q3.shape[0], q3.shape[1], n_slots * q3.shape[2])


def _even_weights(w_in, d_inner, conv_ch, n_a_heads, qw, kvw):
    c = np.cumsum([0, d_inner, conv_ch, n_a_heads, qw, kvw, kvw])
    z, xbc, dt, q, k, v = (w_in[:, c[j]:c[j + 1]] for j in range(6))
    dt = jnp.pad(dt, ((0, 0), (0, LANES - n_a_heads)))
    w = _bf(jnp.concatenate([z, xbc, q, k, v, dt], axis=1))
    off = np.cumsum([0, d_inner, conv_ch, qw, kvw, kvw])
    return w, dict(z=int(off[0]), xbc=int(off[1]), q=int(off[2]), k=int(off[3]), v=int(off[4]), dt=int(off[5]))


def _odd_weights(w_in, sizes):
    c = np.cumsum([0] + list(sizes))
    cq, ck, cv, dq, dk, dv, iq, iw, ik = (w_in[:, c[j]:c[j + 1]] for j in range(9))
    ik = jnp.pad(ik, ((0, 0), (0, LANES - ik.shape[1])))
    iw = jnp.pad(iw, ((0, 0), (0, LANES - iw.shape[1])))
    parts = [cq, dq, iq, ck, cv, dk, dv, ik, iw]
    off = np.cumsum([0] + [p.shape[1] for p in parts])
    names = ["cq", "dq", "iq", "ck", "cv", "dk", "dv", "ik", "iw"]
    return _bf(jnp.concatenate(parts, axis=1)), {n: int(o) for n, o in zip(names, off[:-1])}


def _mixer_even(x, pos, seq_mode, st_conv, st_ssm, win_k, win_v, norm_g, w_in, conv_w, conv_b, dt_bias, a_log,
                d_skip, gain, qn_g, kn_g, sinks, w_out):
    m, _ = x.shape
    n_a_heads = a_log.shape[0]
    d_inner = gain.shape[0]
    conv_ch = conv_w.shape[1]
    n_heads = sinks.shape[0]
    qw = n_heads * HEAD_DIM
    kvw = (w_in.shape[1] - d_inner - conv_ch - n_a_heads - qw) // 2
    n_kv = kvw // HEAD_DIM
    kw = conv_w.shape[0]
    gn = A_GROUPS * A_STATE
    hpg = n_a_heads // A_GROUPS
    w, off = _even_weights(w_in, d_inner, conv_ch, n_a_heads, qw, kvw)
    cos, sin = _rope_tables(pos)
    proj = _mm([x], [w], norm_g=norm_g, name="in_proj_even")
    q_f, q_b = _norm_rope(proj, off["q"], qw, qn_g, cos, sin, name="swa_q_rope")
    k_f, k_b = _norm_rope(proj, off["k"], kvw, kn_g, cos, sin, name="swa_k_rope")
    v_f = proj[:, off["v"]:off["v"] + kvw]
    xbc_raw = proj[:, off["xbc"]:off["xbc"] + conv_ch]
    if seq_mode:
        state8 = jnp.zeros((SUBLANES, conv_ch), F32)
        xbc = _conv_seq(proj, off["xbc"], conv_ch, state8, conv_w, conv_b)
        s0_t = jnp.zeros((A_GROUPS, A_STATE, hpg * A_HEAD_DIM), F32)
        ya, st = _ssd_seq(xbc, proj, off["z"], off["dt"], dt_bias, a_log, d_skip, gain, s0_t, n_a_heads)
        ssm_new = st.reshape(A_GROUPS, A_STATE, hpg, A_HEAD_DIM).transpose(0, 2, 3, 1).reshape(
            1, n_a_heads, A_HEAD_DIM, A_STATE)
        conv_new = xbc_raw[m - (kw - 1):][None]
        ob = _swa_seq(q_b, k_b, _bf(v_f), sinks, n_heads, n_kv)
        wb = min(WINDOW, m)
        new_k = k_f[m - wb:].reshape(1, wb, n_kv, HEAD_DIM)
        new_v = v_f[m - wb:].reshape(1, wb, n_kv, HEAD_DIM)
    else:
        xbc = _conv_step(proj, off["xbc"], conv_ch, st_conv.transpose(1, 0, 2), conv_w, conv_b)
        xdt, dec = _ssd_step_pre(xbc, proj, off["dt"], dt_bias, a_log, d_inner)
        ssm_new, y = _ssd_step(st_ssm, xdt.T, dec.T, xbc[:, d_inner:d_inner + gn], xbc[:, d_inner + gn:])
        ya = _gated_norm_call(y, xbc, proj, off["z"], d_skip, gain)
        conv_new = jnp.concatenate([st_conv[:, 1:], xbc_raw[:, None, :]], axis=1)
        wb = win_k.shape[1]
        grp = n_heads // n_kv
        qm = _place(q_f.reshape(m, n_heads, HEAD_DIM), [h // grp for h in range(n_heads)], n_kv)
        o, new_k, new_v = _swa_step(qm, k_f[:, None, :], v_f[:, None, :], win_k.reshape(m, wb, kvw),
                                    win_v.reshape(m, wb, kvw), sinks, n_heads, n_kv)
        ob = o.reshape(m, qw)
        new_k = new_k.reshape(m, wb, n_kv, HEAD_DIM)
        new_v = new_v.reshape(m, wb, n_kv, HEAD_DIM)
    wo = _bf(w_out)
    y = _mm([ya, ob], [wo[:d_inner], wo[d_inner:]], res=x, name="out_proj_even")
    return y, (conv_new, ssm_new, new_k, new_v)


def _mixer_odd(x, pos, seq_mode, paged, norm_g, w_in, qn_g, kn_g, lam_p, sub_g, dqn_g, dkn_g, w_out, lam_init,
               sizes, n_sel):
    m, _ = x.shape
    hd = HEAD_DIM
    c_heads = sizes[0] // (2 * hd)
    c_kv = sizes[1] // (2 * hd)
    d_heads = sizes[3] // hd
    d_kv = sizes[4] // hd
    n_idx = sizes[7]
    w, off = _odd_weights(w_in, sizes)
    cos, sin = _rope_tables(pos)
    proj = _mm([x], [w], norm_g=norm_g, name="in_proj_odd")
    cq_f, cq_b = _norm_rope(proj, off["cq"], sizes[0], qn_g, cos, sin, name="diff_q_rope")
    ck_f, ck_b = _norm_rope(proj, off["ck"], sizes[1], kn_g, cos, sin, name="diff_k_rope")
    dq_f, dq_b = _norm_rope(proj, off["dq"], sizes[3], dqn_g, cos, sin, name="dsa_q_rope")
    dk_f, dk_b = _norm_rope(proj, off["dk"], sizes[4], dkn_g, cos, sin, name="dsa_k_rope")
    iq_f, iq_b = _norm_rope(proj, off["iq"], sizes[6], None, cos, sin, name="idx_q_rope")
    ik_f, ik_b = _norm_rope(proj, off["ik"], LANES, None, cos, sin, name="idx_k_rope")
    cv_f = proj[:, off["cv"]:off["cv"] + sizes[2]]
    dv_f = proj[:, off["dv"]:off["dv"] + sizes[5]]
    iw = proj[:, off["iw"]:off["iw"] + LANES]
    if seq_mode:
        oc = _diff_seq2(cq_b, ck_b, _bf(cv_f), lam_p, sub_g, lam_init, c_heads, c_kv)
        od = _dsa_seq2(iq_b, iw, dq_b, ik_b, dk_b, _bf(dv_f), d_heads, d_kv, n_idx, n_sel)
    else:
        c_k, c_v, d_k, d_v, d_i, table, layer = paged
        pool = c_k.shape[1]
        ones = jnp.ones((m, 1, LANES), F32)
        grp = c_heads // c_kv
        q4 = cq_f.reshape(m, c_heads, 2, hd).transpose(0, 2, 1, 3).reshape(m, 2 * c_heads, hd)
        slots = [(h // grp) * 2 + c for c in range(2) for h in range(c_heads)]
        qm_c = _place(q4, slots, 2 * c_kv)
        oc = _paged_attn2(table, layer, qm_c, ck_f[:, None, :], cv_f[:, None, :], ones, None,
                          jnp.transpose(c_k, (0, 1, 3, 4, 5, 2)), c_v.reshape(c_v.shape[0], pool, PAGE * c_kv, -1),
                          mode="diff", n_heads=c_heads, n_kv=c_kv, lam=lam_p, sub_g=sub_g, lam_init=lam_init)
        oc = oc.reshape(m, -1)
        scores = _paged_idx_scores2(table, layer, iq_b.reshape(m, n_idx, hd), iw[:, :n_idx, None],
                                    jnp.transpose(d_i, (0, 1, 3, 2)), n_idx)
        ik_tiled = jnp.tile(ik_b[:, :hd], (1, n_idx))
        mask, mnew = _topk_mask(scores.reshape(m, -1), iq_b, ik_tiled, iw, n_idx, n_sel)
        dgrp = d_heads // d_kv
        qm_d = _place(dq_f.reshape(m, d_heads, hd), [h // dgrp for h in range(d_heads)], d_kv)
        od = _paged_attn2(table, layer, qm_d, dk_f[:, None, :], dv_f[:, None, :], mnew[:, None, :], mask[:, None, :],
                          jnp.transpose(d_k, (0, 1, 3, 4, 2)), jnp.transpose(d_v, (0, 1, 3, 4, 2)),
                          mode="gqa", n_heads=d_heads, n_kv=d_kv)
        od = od.reshape(m, -1)
    wo = _bf(w_out)
    y = _mm([oc, od], [wo[:oc.shape[1]], wo[oc.shape[1]:]], res=x, name="out_proj_odd")
    lead = (1, m) if seq_mode else (m, 1)
    caches = (ck_f.reshape(lead + (c_kv, 2, hd)), cv_f.reshape(lead + (c_kv, 2 * hd)),
              dk_f.reshape(lead + (d_kv, hd)), dv_f.reshape(lead + (d_kv, hd)), ik_f[:, :hd].reshape(lead + (hd,)))
    return y, caches


def kernel(x_prompt, x_sample, state_ssm, state_ssm_conv, cache_swa_k, cache_swa_v, cache_c_k, cache_c_v, cache_d_k, cache_d_v, cache_d_idx, state_ffn_conv, page_table, norm_mix_g, norm_ffn_g, a_w_in, a_conv_w, a_conv_b, a_dt_bias, a_A_log, a_D, a_norm_g, b_qn_g, b_kn_g, b_sinks, e_w_out, m_w_in, c_qn_g, c_kn_g, c_lam, c_subln_g, d_qn_g, d_kn_g, m_w_out, ffn_w_gate, ffn_w_up, ffn_conv_w, ffn_conv_b, ffn_w_down):
    bp, seq, d_model = x_prompt.shape
    nb = x_sample.shape[0]
    assert bp == 1 and x_sample.shape[1] == 1
    depth = norm_mix_g.shape[0]
    d_ff = ffn_w_gate.shape[2]
    past = page_table.shape[1] * PAGE
    xp = x_prompt.reshape(seq, d_model)
    xs = x_sample.reshape(nb, d_model)
    pos_p = jnp.arange(seq)
    pos_s = jnp.full((nb,), past, I32)
    hd = HEAD_DIM
    c_kv, d_kv, idx_dim = cache_c_k.shape[3], cache_d_k.shape[3], cache_d_idx.shape[3]
    d_heads = d_model // 128
    c_heads = d_model // 256
    n_idx = m_w_in.shape[2] - (c_heads * 2 * hd + 2 * c_kv * 2 * hd + d_heads * hd + 2 * d_kv * hd
                               + d_heads * hd + idx_dim)
    odd_sizes = (c_heads * 2 * hd, c_kv * 2 * hd, c_kv * 2 * hd, d_heads * hd, d_kv * hd, d_kv * hd,
                 d_heads * hd, n_idx, idx_dim)
    outs_p = {k: [] for k in ("ssm", "cnv", "swk", "swv", "ck", "cv", "dk", "dv", "di", "fc")}
    outs_s = {k: [] for k in outs_p}
    for i in range(depth):
        if i % 2 == 0:
            e = i // 2
            wts = (norm_mix_g[i], a_w_in[e], a_conv_w[e], a_conv_b[e], a_dt_bias[e], a_A_log[e], a_D[e],
                   a_norm_g[e], b_qn_g[e], b_kn_g[e], b_sinks[e], e_w_out[e])
            xp, (c1, s1, k1, v1) = _mixer_even(xp, pos_p, True, None, None, None, None, *wts)
            xs, (c2, s2, k2, v2) = _mixer_even(xs, pos_s, False, state_ssm_conv[e], state_ssm[e],
                                               cache_swa_k[e], cache_swa_v[e], *wts)
            for d, vals in ((outs_p, (c1, s1, k1, v1)), (outs_s, (c2, s2, k2, v2))):
                for key, val in zip(("cnv", "ssm", "swk", "swv"), vals):
                    d[key].append(val)
        else:
            o = i // 2
            lam_init = 0.8 - 0.6 * math.exp(-0.3 * i)
            wts = (norm_mix_g[i], m_w_in[o], c_qn_g[o], c_kn_g[o], c_lam[o], c_subln_g[o], d_qn_g[o], d_kn_g[o],
                   m_w_out[o], lam_init, odd_sizes)
            xp, cp = _mixer_odd(xp, pos_p, True, None, *wts, min(256, seq // 4))
            xs, cs = _mixer_odd(xs, pos_s, False,
                                (cache_c_k, cache_c_v, cache_d_k, cache_d_v, cache_d_idx, page_table, o),
                                *wts, min(256, (past + 1) // 4))
            for d, vals in ((outs_p, cp), (outs_s, cs)):
                for key, val in zip(("ck", "cv", "dk", "dv", "di"), vals):
                    d[key].append(val)
        fw = (norm_ffn_g[i], _bf(ffn_w_gate[i]), _bf(ffn_w_up[i]), ffn_conv_w[i], ffn_conv_b[i], _bf(ffn_w_down[i]))
        zrow = jnp.zeros((1, d_ff), F32)
        xp, gp = _ffn(xp, zrow, zrow, *fw, seq_mode=True)
        outs_p["fc"].append(gp[gp.shape[0] - (ffn_conv_w.shape[1] - 1):][None])
        st = state_ffn_conv[i]
        xs, gs = _ffn(xs, st[:, 0, :], st[:, 1, :], *fw, seq_mode=False)
        outs_s["fc"].append(jnp.stack([st[:, 1, :], gs], axis=1))
    order = ("ssm", "cnv", "swk", "swv", "ck", "cv", "dk", "dv", "di", "fc")
    return ((xp.reshape(1, seq, d_model), xs.reshape(nb, 1, d_model))
            + tuple(jnp.stack(outs_p[k]) for k in order) + tuple(jnp.stack(outs_s[k]) for k in order))
```

```python
import functools
import math

import jax
import jax.numpy as jnp
import numpy as np
from jax import lax
from jax.experimental import pallas as pl
from jax.experimental.pallas import tpu as pltpu

F32 = jnp.float32
BF16 = jnp.bfloat16
I32 = jnp.int32

EPS = 1e-6
ROPE_THETA = 10000.0
HEAD_DIM = 64
LANES = 128
SUBLANES = 8
VMEM_LIMIT = 56 * 1024 * 1024
WINDOW = 128
SSD_CHUNK = 128
PAGE = 128
A_GROUPS = 2
A_HEAD_DIM = 64
A_STATE = 128
INT_MIN = -2147483648


def _cparams(*sem):
    return pltpu.CompilerParams(dimension_semantics=sem, vmem_limit_bytes=VMEM_LIMIT)


def _pick_tile(n, cap):
    best = LANES
    for m in range(1, n // LANES + 1):
        if n % (m * LANES) == 0 and m * LANES <= cap:
            best = m * LANES
    return best


def _row_tile(m, cap):
    t = min(m, cap)
    while m % t:
        t //= 2
    return t


def _split3(x):
    h = x.astype(BF16)
    r = x - h.astype(F32)
    m = r.astype(BF16)
    lo = (r - m.astype(F32)).astype(BF16)
    return h, m, lo


def _dot(a, b):
    return jnp.dot(a, b, preferred_element_type=F32)


def _dot_nt(a, b):
    return lax.dot_general(a, b, (((1,), (1,)), ((), ())), preferred_element_type=F32)


def _dot3(x, w01):
    h, m, lo = _split3(x)
    return _dot(h, w01) + _dot(m, w01) + _dot(lo, w01)


def _dot3_left(w01, x):
    h, m, lo = _split3(x)
    return _dot(w01, h) + _dot(w01, m) + _dot(w01, lo)


def _silu(x):
    return x * (1.0 / (1.0 + jnp.exp(-x)))


def _softplus(x):
    return jnp.maximum(x, 0.0) + jnp.log(1.0 + jnp.exp(-jnp.abs(x)))


def _mm_kernel(*refs, n_lhs, has_norm, has_res):
    xs = refs[:n_lhs]
    pos = n_lhs
    g_ref = None
    if has_norm:
        g_ref = refs[pos]
        pos += 1
    ws = refs[pos:pos + n_lhs]
    pos += n_lhs
    res_ref = None
    if has_res:
        res_ref = refs[pos]
        pos += 1
    o_ref = refs[pos]
    xb = refs[pos + 1:pos + 1 + n_lhs]

    @pl.when(pl.program_id(1) == 0)
    def _():
        for k in range(n_lhs):
            x = xs[k][...]
            if has_norm and k == 0:
                x = x * lax.rsqrt(jnp.mean(x * x, axis=-1, keepdims=True) + EPS) * g_ref[...]
            xb[k][...] = x.astype(BF16)

    acc = _dot(xb[0][...], ws[0][...])
    for k in range(1, n_lhs):
        acc = acc + _dot(xb[k][...], ws[k][...])
    if has_res:
        acc = acc + res_ref[...]
    o_ref[...] = acc


def _mm(xs, ws, *, norm_g=None, res=None, tm_cap=512, tn_cap=1280, name="mm"):
    m = xs[0].shape[0]
    n = ws[0].shape[1]
    tm = _row_tile(m, tm_cap)
    tn = _pick_tile(n, tn_cap)
    n_lhs = len(xs)
    in_specs = [pl.BlockSpec((tm, x.shape[1]), lambda i, j: (i, 0)) for x in xs]
    args = list(xs)
    if norm_g is not None:
        in_specs.append(pl.BlockSpec((1, xs[0].shape[1]), lambda i, j: (0, 0)))
        args.append(norm_g.reshape(1, -1))
    in_specs += [pl.BlockSpec((w.shape[0], tn), lambda i, j: (0, j)) for w in ws]
    args += list(ws)
    if res is not None:
        in_specs.append(pl.BlockSpec((tm, tn), lambda i, j: (i, j)))
        args.append(res)
    return pl.pallas_call(
        functools.partial(_mm_kernel, n_lhs=n_lhs, has_norm=norm_g is not None, has_res=res is not None),
        out_shape=jax.ShapeDtypeStruct((m, n), F32),
        grid=(m // tm, n // tn),
        in_specs=in_specs,
        out_specs=pl.BlockSpec((tm, tn), lambda i, j: (i, j)),
        scratch_shapes=[pltpu.VMEM((tm, x.shape[1]), BF16) for x in xs],
        compiler_params=_cparams("parallel", "arbitrary"),
        name=name,
    )(*args)


def _ffn_kernel(x_ref, halo_ref, g_ref, wg_ref, wu_ref, cw_ref, cb_ref, wd_ref, p0_ref, p1_ref,
                o_ref, gout_ref, xb_ref, hb_ref, gs_ref, acc_ref, *, seq_mode, tm):
    i = pl.program_id(0)
    j = pl.program_id(1)
    nj = pl.num_programs(1)

    def norm(x):
        return (x * lax.rsqrt(jnp.mean(x * x, axis=-1, keepdims=True) + EPS) * g_ref[...]).astype(BF16)

    @pl.when(j == 0)
    def _():
        xb_ref[...] = norm(x_ref[...])
        if seq_mode:
            hb_ref[...] = norm(halo_ref[...])
        acc_ref[...] = jnp.zeros_like(acc_ref)

    g = _dot(xb_ref[...], wg_ref[...])
    u = _dot(xb_ref[...], wu_ref[...])
    cw = cw_ref[...]
    if seq_mode:
        carried = jnp.concatenate([jnp.zeros((SUBLANES - 2, g.shape[1]), F32), p0_ref[...], p1_ref[...]], axis=0)
        prev = jnp.where(i == 0, carried, _dot(hb_ref[...], wg_ref[...]))
        gs_ref[0:SUBLANES, :] = prev
        gs_ref[SUBLANES:, :] = g
        g1 = gs_ref[pl.ds(SUBLANES - 1, tm), :]
        g2 = gs_ref[pl.ds(SUBLANES - 2, tm), :]
        gout_ref[...] = g[tm - SUBLANES:, :]
    else:
        g1 = p1_ref[...]
        g2 = p0_ref[...]
        gout_ref[...] = g
    c = cw[0:1, :] * g2 + cw[1:2, :] * g1 + cw[2:3, :] * g + cb_ref[...]
    act = (_silu(c) * u).astype(BF16)
    acc_ref[...] += _dot(act, wd_ref[...])

    @pl.when(j == nj - 1)
    def _():
        o_ref[...] = x_ref[...] + acc_ref[...]


def _ffn(x, prev0, prev1, norm_g, wg, wu, conv_w, conv_b, wd, *, seq_mode):
    m, d = x.shape
    f = wg.shape[1]
    tm = _row_tile(m, 1024 if seq_mode else 128)
    tn = _pick_tile(f, 1408)
    ni, nj = m // tm, f // tn
    hb = tm // SUBLANES
    if seq_mode:
        prev_spec = pl.BlockSpec((1, tn), lambda i, j: (0, j))
        gout_rows, gout_shape = SUBLANES, (ni * SUBLANES, f)
    else:
        prev_spec = pl.BlockSpec((tm, tn), lambda i, j: (i, j))
        gout_rows, gout_shape = tm, (m, f)
    out, gout = pl.pallas_call(
        functools.partial(_ffn_kernel, seq_mode=seq_mode, tm=tm),
        out_shape=(jax.ShapeDtypeStruct((m, d), F32), jax.ShapeDtypeStruct(gout_shape, F32)),
        grid=(ni, nj),
        in_specs=[
            pl.BlockSpec((tm, d), lambda i, j: (i, 0)),
            pl.BlockSpec((SUBLANES, d), lambda i, j: (jnp.maximum(i * hb - 1, 0), 0)),
            pl.BlockSpec((1, d), lambda i, j: (0, 0)),
            pl.BlockSpec((d, tn), lambda i, j: (0, j)),
            pl.BlockSpec((d, tn), lambda i, j: (0, j)),
            pl.BlockSpec((conv_w.shape[0], tn), lambda i, j: (0, j)),
            pl.BlockSpec((1, tn), lambda i, j: (0, j)),
            pl.BlockSpec((tn, d), lambda i, j: (j, 0)),
            prev_spec, prev_spec,
        ],
        out_specs=(pl.BlockSpec((tm, d), lambda i, j: (i, 0)),
                   pl.BlockSpec((gout_rows, tn), lambda i, j: (i, j))),
        scratch_shapes=[pltpu.VMEM((tm, d), BF16), pltpu.VMEM((SUBLANES, d), BF16),
                        pltpu.VMEM((tm + SUBLANES, tn), F32), pltpu.VMEM((tm, d), F32)],
        compiler_params=_cparams("parallel", "arbitrary"),
        name="conv_ffn",
    )(x, x, norm_g.reshape(1, -1), wg, wu, conv_w, conv_b.reshape(1, -1), wd, prev0, prev1)
    return out, gout


def _conv_seq_kernel(x_ref, halo_ref, st_ref, w_ref, b_ref, o_ref, xs_ref, *, tm, kw):
    i = pl.program_id(0)
    prev = jnp.where(i == 0, st_ref[...], halo_ref[...])
    xs_ref[0:SUBLANES, :] = prev
    xs_ref[SUBLANES:, :] = x_ref[...]
    w = w_ref[...]
    acc = b_ref[...] + w[kw - 1:kw, :] * x_ref[...]
    for t in range(1, kw):
        acc = acc + w[kw - 1 - t:kw - t, :] * xs_ref[pl.ds(SUBLANES - t, tm), :]
    o_ref[...] = _silu(acc)


def _conv_seq(src, col0, width, state8, w, b):
    m = src.shape[0]
    kw = w.shape[0]
    tm = _row_tile(m, 512)
    tn = _pick_tile(math.gcd(width, col0) if col0 else width, 512)
    cb = col0 // tn
    hb = tm // SUBLANES
    return pl.pallas_call(
        functools.partial(_conv_seq_kernel, tm=tm, kw=kw),
        out_shape=jax.ShapeDtypeStruct((m, width), F32),
        grid=(m // tm, width // tn),
        in_specs=[
            pl.BlockSpec((tm, tn), lambda i, j: (i, cb + j)),
            pl.BlockSpec((SUBLANES, tn), lambda i, j: (jnp.maximum(i * hb - 1, 0), cb + j)),
            pl.BlockSpec((SUBLANES, tn), lambda i, j: (0, j)),
            pl.BlockSpec((kw, tn), lambda i, j: (0, j)),
            pl.BlockSpec((1, tn), lambda i, j: (0, j)),
        ],
        out_specs=pl.BlockSpec((tm, tn), lambda i, j: (i, j)),
        scratch_shapes=[pltpu.VMEM((tm + SUBLANES, tn), F32)],
        compiler_params=_cparams("parallel", "parallel"),
        name="ssm_conv_seq",
    )(src, src, state8, w, b.reshape(1, -1))


def _conv_step_kernel(x_ref, s_ref, w_ref, b_ref, o_ref, *, kw):
    w = w_ref[...]
    acc = b_ref[...] + w[kw - 1:kw, :] * x_ref[...]
    for t in range(kw - 1):
        acc = acc + w[t:t + 1, :] * s_ref[t]
    o_ref[...] = _silu(acc)


def _conv_step(src, col0, width, state, w, b):
    m = src.shape[0]
    kw = w.shape[0]
    tn = _pick_tile(math.gcd(width, col0) if col0 else width, 512)
    cb = col0 // tn
    return pl.pallas_call(
        functools.partial(_conv_step_kernel, kw=kw),
        out_shape=jax.ShapeDtypeStruct((m, width), F32),
        grid=(width // tn,),
        in_specs=[
            pl.BlockSpec((m, tn), lambda j: (0, cb + j)),
            pl.BlockSpec((kw - 1, m, tn), lambda j: (0, 0, j)),
            pl.BlockSpec((kw, tn), lambda j: (0, j)),
            pl.BlockSpec((1, tn), lambda j: (0, j)),
        ],
        out_specs=pl.BlockSpec((m, tn), lambda j: (0, j)),
        compiler_params=_cparams("parallel"),
        name="ssm_conv_step",
    )(src, state, w, b.reshape(1, -1))


def _seg_ones(seg):
    r = lax.broadcasted_iota(I32, (LANES, LANES), 0) // seg
    c = lax.broadcasted_iota(I32, (LANES, LANES), 1) // seg
    return (r == c).astype(BF16)


def _rope128(x, cos, sin_signed):
    lane = lax.broadcasted_iota(I32, x.shape, 1)
    rot = jnp.where(lane % HEAD_DIM < HEAD_DIM // 2,
                    pltpu.roll(x, LANES - HEAD_DIM // 2, 1), pltpu.roll(x, HEAD_DIM // 2, 1))
    return x * cos + rot * sin_signed


def _norm_rope_kernel(x_ref, g_ref, cos_ref, sin_ref, o_ref, ob_ref, *, do_norm, width):
    cos = cos_ref[...]
    sin = sin_ref[...]
    ones = _seg_ones(HEAD_DIM)
    for c in range(width // LANES):
        x = x_ref[:, c * LANES:(c + 1) * LANES]
        if do_norm:
            ms = _dot3(x * x, ones) * (1.0 / HEAD_DIM)
            x = x * lax.rsqrt(ms + EPS) * g_ref[...]
        y = _rope128(x, cos, sin)
        o_ref[:, c * LANES:(c + 1) * LANES] = y
        ob_ref[:, c * LANES:(c + 1) * LANES] = y.astype(BF16)


def _norm_rope(src, col0, width, gain, cos, sin, *, name):
    m = src.shape[0]
    tm = _row_tile(m, 512)
    assert col0 % width == 0
    cb = col0 // width
    g = jnp.ones((1, LANES), F32) if gain is None else jnp.tile(gain.reshape(1, HEAD_DIM), (1, LANES // HEAD_DIM))
    return pl.pallas_call(
        functools.partial(_norm_rope_kernel, do_norm=gain is not None, width=width),
        out_shape=(jax.ShapeDtypeStruct((m, width), F32), jax.ShapeDtypeStruct((m, width), BF16)),
        grid=(m // tm,),
        in_specs=[
            pl.BlockSpec((tm, width), lambda i: (i, cb)),
            pl.BlockSpec((1, LANES), lambda i: (0, 0)),
            pl.BlockSpec((tm, LANES), lambda i: (i, 0)),
            pl.BlockSpec((tm, LANES), lambda i: (i, 0)),
        ],
        out_specs=(pl.BlockSpec((tm, width), lambda i: (i, 0)), pl.BlockSpec((tm, width), lambda i: (i, 0))),
        compiler_params=_cparams("parallel"),
        name=name,
    )(src, g, cos, sin)


def _rope_tables(pos):
    half = HEAD_DIM // 2
    inv = ROPE_THETA ** (-jnp.arange(half, dtype=F32) / half)
    ang = pos.astype(F32)[:, None] * inv[None, :]
    cos, sin = jnp.cos(ang), jnp.sin(ang)
    cos128 = jnp.tile(jnp.concatenate([cos, cos], axis=1), (1, LANES // HEAD_DIM))
    sin128 = jnp.tile(jnp.concatenate([-sin, sin], axis=1), (1, LANES // HEAD_DIM))
    return cos128, sin128


def _head_expand(n_heads_pad, width):
    r = lax.broadcasted_iota(I32, (n_heads_pad, width), 0)
    c = lax.broadcasted_iota(I32, (n_heads_pad, width), 1) // A_HEAD_DIM
    return (r == c).astype(BF16)


def _gated_norm(y, xs, z, dskip, gain):
    yz = (y + xs * dskip) * _silu(z)
    gw = yz.shape[1] // A_GROUPS
    parts = []
    for g in range(A_GROUPS):
        p = yz[:, g * gw:(g + 1) * gw]
        parts.append(p * lax.rsqrt(jnp.mean(p * p, axis=-1, keepdims=True) + EPS))
    return jnp.concatenate(parts, axis=1) * gain


def _ssd_seq_kernel(xs_ref, b_ref, c_ref, dt_ref, z_ref, dtb_ref, alog_ref, dskip_ref, gain_ref, s0_ref,
                    ya_ref, sout_ref, st_ref, y_ref, *, n_heads):
    ci = pl.program_id(0)
    q = SSD_CHUNK
    d_inner = xs_ref.shape[1]
    hpg = n_heads // A_GROUPS
    gw = d_inner // A_GROUPS

    @pl.when(ci == 0)
    def _():
        st_ref[...] = s0_ref[...]

    xs = xs_ref[...]
    dt = _softplus(dt_ref[...] + dtb_ref[...])
    a = dt * (-jnp.exp(alog_ref[...]))
    row = lax.broadcasted_iota(I32, (q, q), 0)
    col = lax.broadcasted_iota(I32, (q, q), 1)
    causal = col <= row
    tri = causal.astype(BF16)
    tri_t = (row <= col).astype(BF16)
    acs = _dot3_left(tri, a)
    acs_t = _dot3(a.T, tri_t)
    expand = _head_expand(LANES, d_inner)
    acs_x = _dot3(acs, expand)
    dt_x = _dot3(dt, expand)
    e_acs = jnp.exp(acs_x)
    last = acs_x[q - 1:q, :]
    decay_s = jnp.exp(last - acs_x)
    xdt = xs * dt_x
    xdt_b = xdt.astype(BF16)
    xdec_b = (xdt * decay_s).astype(BF16)
    chunk_decay = e_acs[q - 1:q, :]

    for g in range(A_GROUPS):
        bg = b_ref[:, g * A_STATE:(g + 1) * A_STATE]
        cg = c_ref[:, g * A_STATE:(g + 1) * A_STATE].astype(BF16)
        cb = _dot_nt(cg, bg.astype(BF16))
        st_g = st_ref[g]
        y_off = _dot(cg, st_g.astype(BF16)) * e_acs[:, g * gw:(g + 1) * gw]
        for hh in range(hpg):
            h = g * hpg + hh
            diff = acs[:, h:h + 1] - acs_t[h:h + 1, :]
            m = (cb * jnp.where(causal, jnp.exp(diff), 0.0)).astype(BF16)
            lo = h * A_HEAD_DIM
            y_ref[:, lo:lo + A_HEAD_DIM] = (_dot(m, xdt_b[:, lo:lo + A_HEAD_DIM])
                                            + y_off[:, hh * A_HEAD_DIM:(hh + 1) * A_HEAD_DIM])
        st_ref[g] = st_g * chunk_decay[:, g * gw:(g + 1) * gw] + _dot(bg.T.astype(BF16), xdec_b[:, g * gw:(g + 1) * gw])

    ya_ref[...] = _gated_norm(y_ref[...], xs, z_ref[...], dskip_ref[...], gain_ref[...])

    @pl.when(ci == pl.num_programs(0) - 1)
    def _():
        sout_ref[...] = st_ref[...]


def _ssd_seq(xbc, proj, z_col, dt_col, dt_bias, a_log, d_skip, gain, s0_t, n_heads):
    t = xbc.shape[0]
    q = SSD_CHUNK
    d_inner = n_heads * A_HEAD_DIM
    gn = A_GROUPS * A_STATE
    bcol = d_inner // gn
    pad = lambda v: jnp.pad(v.reshape(1, -1), ((0, 0), (0, LANES - v.shape[-1])))
    dskip_x = jnp.repeat(d_skip, A_HEAD_DIM).reshape(1, d_inner)
    const = lambda shape: pl.BlockSpec(shape, lambda c: (0,) * len(shape))
    return pl.pallas_call(
        functools.partial(_ssd_seq_kernel, n_heads=n_heads),
        out_shape=(jax.ShapeDtypeStruct((t, d_inner), F32), jax.ShapeDtypeStruct(s0_t.shape, F32)),
        grid=(t // q,),
        in_specs=[
            pl.BlockSpec((q, d_inner), lambda c: (c, 0)),
            pl.BlockSpec((q, gn), lambda c: (c, bcol)),
            pl.BlockSpec((q, gn), lambda c: (c, bcol + 1)),
            pl.BlockSpec((q, LANES), lambda c: (c, dt_col // LANES)),
            pl.BlockSpec((q, d_inner), lambda c: (c, z_col // d_inner)),
            const((1, LANES)), const((1, LANES)), const((1, d_inner)), const((1, d_inner)),
            const(s0_t.shape),
        ],
        out_specs=(pl.BlockSpec((q, d_inner), lambda c: (c, 0)), const(s0_t.shape)),
        scratch_shapes=[pltpu.VMEM(s0_t.shape, F32), pltpu.VMEM((q, d_inner), F32)],
        compiler_params=_cparams("arbitrary"),
        name="ssd_seq",
    )(xbc, xbc, xbc, proj, proj, pad(dt_bias), pad(a_log), dskip_x, gain.reshape(1, -1), s0_t)


def _ssd_step_kernel(s_ref, xdt_t_ref, dec_t_ref, b_ref, c_ref, sout_ref, y_ref, *, n_heads):
    b = pl.program_id(0)
    nb = xdt_t_ref.shape[1]
    hp = n_heads * A_HEAD_DIM
    gw = hp // A_GROUPS
    lane = lax.broadcasted_iota(I32, (hp, nb), 1)
    dec = jnp.sum(jnp.where(lane == b, dec_t_ref[...], 0.0), axis=1, keepdims=True)
    rows = lax.broadcasted_iota(I32, (nb, A_STATE), 0)
    s = s_ref[0].reshape(hp, A_STATE)
    xdt_t = xdt_t_ref[...].astype(BF16)
    outs = []
    for g in range(A_GROUPS):
        brow = b_ref[0, :, g * A_STATE:(g + 1) * A_STATE]
        zb = jnp.where(rows == b, jnp.broadcast_to(brow, (nb, A_STATE)), 0.0).astype(BF16)
        upd = _dot(xdt_t[g * gw:(g + 1) * gw, :], zb)
        sn = s[g * gw:(g + 1) * gw, :] * dec[g * gw:(g + 1) * gw, :] + upd
        sout_ref[0, g * (n_heads // A_GROUPS):(g + 1) * (n_heads // A_GROUPS)] = sn.reshape(
            n_heads // A_GROUPS, A_HEAD_DIM, A_STATE)
        crow = c_ref[0, :, g * A_STATE:(g + 1) * A_STATE]
        c8 = jnp.broadcast_to(crow, (SUBLANES, A_STATE)).astype(BF16)
        outs.append(_dot_nt(c8, sn.astype(BF16))[0:1, :])
    y_ref[0] = jnp.concatenate(outs, axis=1)


def _ssd_step(state, xdt_t, dec_t, bmat, cmat):
    nb, n_heads, p, n = state.shape
    hp = n_heads * p
    new_state, y = pl.pallas_call(
        functools.partial(_ssd_step_kernel, n_heads=n_heads),
        out_shape=(jax.ShapeDtypeStruct(state.shape, F32), jax.ShapeDtypeStruct((nb, 1, hp), F32)),
        grid=(nb,),
        in_specs=[
            pl.BlockSpec((1, n_heads, p, n), lambda b: (b, 0, 0, 0)),
            pl.BlockSpec((hp, nb), lambda b: (0, 0)),
            pl.BlockSpec((hp, nb), lambda b: (0, 0)),
            pl.BlockSpec((1, 1, bmat.shape[1]), lambda b: (b, 0, 0)),
            pl.BlockSpec((1, 1, cmat.shape[1]), lambda b: (b, 0, 0)),
        ],
        out_specs=(pl.BlockSpec((1, n_heads, p, n), lambda b: (b, 0, 0, 0)),
                   pl.BlockSpec((1, 1, hp), lambda b: (b, 0, 0))),
        compiler_params=_cparams("arbitrary"),
        name="ssd_step",
    )(state, xdt_t, dec_t, bmat[:, None, :], cmat[:, None, :])
    return new_state, y.reshape(nb, hp)


def _ssd_step_pre_kernel(xs_ref, dt_ref, dtb_ref, alog_ref, xdt_ref, dec_ref):
    dt = _softplus(dt_ref[...] + dtb_ref[...])
    expand = _head_expand(LANES, xs_ref.shape[1])
    xdt_ref[...] = xs_ref[...] * _dot3(dt, expand)
    dec_ref[...] = jnp.exp(_dot3(dt * (-jnp.exp(alog_ref[...])), expand))


def _ssd_step_pre(xbc, proj, dt_col, dt_bias, a_log, d_inner):
    m = xbc.shape[0]
    pad = lambda v: jnp.pad(v.reshape(1, -1), ((0, 0), (0, LANES - v.shape[-1])))
    return pl.pallas_call(
        _ssd_step_pre_kernel,
        out_shape=(jax.ShapeDtypeStruct((m, d_inner), F32), jax.ShapeDtypeStruct((m, d_inner), F32)),
        grid=(1,),
        in_specs=[pl.BlockSpec((m, d_inner), lambda i: (0, 0)),
                  pl.BlockSpec((m, LANES), lambda i: (0, dt_col // LANES)),
                  pl.BlockSpec((1, LANES), lambda i: (0, 0)), pl.BlockSpec((1, LANES), lambda i: (0, 0))],
        out_specs=(pl.BlockSpec((m, d_inner), lambda i: (0, 0)), pl.BlockSpec((m, d_inner), lambda i: (0, 0))),
        compiler_params=_cparams("arbitrary"),
        name="ssd_step_pre",
    )(xbc, proj, pad(dt_bias), pad(a_log))


def _gated_norm_kernel(y_ref, xs_ref, z_ref, dskip_ref, gain_ref, o_ref):
    o_ref[...] = _gated_norm(y_ref[...], xs_ref[...], z_ref[...], dskip_ref[...], gain_ref[...])


def _gated_norm_call(y, xbc, proj, z_col, d_skip, gain):
    m, d_inner = y.shape
    dskip_x = jnp.repeat(d_skip, A_HEAD_DIM).reshape(1, d_inner)
    return pl.pallas_call(
        _gated_norm_kernel,
        out_shape=jax.ShapeDtypeStruct((m, d_inner), F32),
        grid=(1,),
        in_specs=[pl.BlockSpec((m, d_inner), lambda i: (0, 0)),
                  pl.BlockSpec((m, d_inner), lambda i: (0, 0)),
                  pl.BlockSpec((m, d_inner), lambda i: (0, z_col // d_inner)),
                  pl.BlockSpec((1, d_inner), lambda i: (0, 0)), pl.BlockSpec((1, d_inner), lambda i: (0, 0))],
        out_specs=pl.BlockSpec((m, d_inner), lambda i: (0, 0)),
        compiler_params=_cparams("arbitrary"),
        name="gated_norm",
    )(y, xbc, proj, dskip_x, gain.reshape(1, -1))


def _swa_seq_kernel(sink_ref, q_ref, kc_ref, kp_ref, vc_ref, vp_ref, o_ref, *, n_heads, n_kv):
    i = pl.program_id(0)
    w = WINDOW
    grp = n_heads // n_kv
    r = lax.broadcasted_iota(I32, (grp * w, 2 * w), 0) % w
    c = lax.broadcasted_iota(I32, (grp * w, 2 * w), 1)
    ok = (c > r) & (c <= r + w) & ((i > 0) | (c >= w))
    hrow = lax.broadcasted_iota(I32, (grp * w, 1), 0) // w
    scale = HEAD_DIM ** -0.5
    for kv in range(n_kv):
        sl = slice(kv * HEAD_DIM, (kv + 1) * HEAD_DIM)
        kcat = jnp.concatenate([kp_ref[:, sl], kc_ref[:, sl]], axis=0)
        vcat = jnp.concatenate([vp_ref[:, sl], vc_ref[:, sl]], axis=0)
        q4 = jnp.concatenate([q_ref[:, (kv * grp + j) * HEAD_DIM:(kv * grp + j + 1) * HEAD_DIM]
                              for j in range(grp)], axis=0)
        sink = jnp.zeros((grp * w, 1), F32)
        for j in range(grp):
            sink = jnp.where(hrow == j, sink_ref[kv * grp + j], sink)
        s = jnp.where(ok, _dot_nt(q4, kcat) * scale, -jnp.inf)
        m = jnp.maximum(jnp.max(s, axis=1, keepdims=True), sink)
        p = jnp.exp(s - m)
        denom = jnp.sum(p, axis=1, keepdims=True) + jnp.exp(sink - m)
        o = _dot(p.astype(BF16), vcat) / denom
        for j in range(grp):
            h = kv * grp + j
            o_ref[:, h * HEAD_DIM:(h + 1) * HEAD_DIM] = o[j * w:(j + 1) * w, :]


def _swa_seq(q_b, k_b, v_b, sinks, n_heads, n_kv):
    t = q_b.shape[0]
    w = WINDOW
    kvw = n_kv * HEAD_DIM
    cur = lambda i: (i, 0)
    prv = lambda i: (jnp.maximum(i - 1, 0), 0)
    return pl.pallas_call(
        functools.partial(_swa_seq_kernel, n_heads=n_heads, n_kv=n_kv),
        out_shape=jax.ShapeDtypeStruct((t, n_heads * HEAD_DIM), F32),
        grid=(t // w,),
        in_specs=[pl.BlockSpec(memory_space=pltpu.SMEM),
                  pl.BlockSpec((w, n_heads * HEAD_DIM), cur),
                  pl.BlockSpec((w, kvw), cur), pl.BlockSpec((w, kvw), prv),
                  pl.BlockSpec((w, kvw), cur), pl.BlockSpec((w, kvw), prv)],
        out_specs=pl.BlockSpec((w, n_heads * HEAD_DIM), cur),
        compiler_params=_cparams("parallel"),
        name="swa_seq",
    )(sinks, q_b, k_b, k_b, v_b, v_b)


def _swa_step_kernel(sink_ref, qm_ref, knew_ref, vnew_ref, kc_ref, vc_ref, o_ref, ko_ref, vo_ref,
                     *, n_heads, n_kv, bs):
    grp = n_heads // n_kv
    wb = kc_ref.shape[1]
    scale = HEAD_DIM ** -0.5
    hrow = lax.broadcasted_iota(I32, (n_heads, 1), 0)
    sink = jnp.zeros((n_heads, 1), F32)
    for h in range(n_heads):
        sink = jnp.where(hrow == h, sink_ref[h], sink)
    col = lax.broadcasted_iota(I32, (n_heads, wb), 1)
    ok = col > wb - WINDOW
    for bi in range(bs):
        qm = qm_ref[bi]
        kc = kc_ref[bi]
        vc = vc_ref[bi]
        knew = knew_ref[bi]
        vnew = vnew_ref[bi]
        s = jnp.where(ok, _dot_nt(qm.astype(BF16), kc.astype(BF16)) * scale, -jnp.inf)
        s_new = jnp.sum(qm * knew, axis=1, keepdims=True) * scale
        m = jnp.maximum(jnp.maximum(jnp.max(s, axis=1, keepdims=True), s_new), sink)
        p = jnp.exp(s - m)
        p_new = jnp.exp(s_new - m)
        denom = jnp.sum(p, axis=1, keepdims=True) + p_new + jnp.exp(sink - m)
        o = (_dot(p.astype(BF16), vc.astype(BF16)) + p_new * vnew) / denom
        osel = o[:, 0:HEAD_DIM]
        for kv in range(1, n_kv):
            osel = jnp.where(hrow // grp == kv, o[:, kv * HEAD_DIM:(kv + 1) * HEAD_DIM], osel)
        o_ref[bi] = osel
        ko_ref[bi, 0:wb - 1, :] = kc_ref[bi, 1:wb, :]
        ko_ref[bi, wb - 1:wb, :] = knew
        vo_ref[bi, 0:wb - 1, :] = vc_ref[bi, 1:wb, :]
        vo_ref[bi, wb - 1:wb, :] = vnew


def _swa_step(qm, knew, vnew, cache_k, cache_v, sinks, n_heads, n_kv):
    nb, wb, kvw = cache_k.shape
    bs = 8 if nb % 8 == 0 else 1
    blk = lambda shape: pl.BlockSpec((bs,) + shape, lambda b: (b,) + (0,) * len(shape))
    return pl.pallas_call(
        functools.partial(_swa_step_kernel, n_heads=n_heads, n_kv=n_kv, bs=bs),
        out_shape=(jax.ShapeDtypeStruct((nb, n_heads, HEAD_DIM), F32),
                   jax.ShapeDtypeStruct(cache_k.shape, F32), jax.ShapeDtypeStruct(cache_v.shape, F32)),
        grid=(nb // bs,),
        in_specs=[pl.BlockSpec(memory_space=pltpu.SMEM), blk((n_heads, kvw)), blk((1, kvw)), blk((1, kvw)),
                  blk((wb, kvw)), blk((wb, kvw))],
        out_specs=(blk((n_heads, HEAD_DIM)), blk((wb, kvw)), blk((wb, kvw))),
        compiler_params=_cparams("parallel"),
        name="swa_step",
    )(sinks, qm, knew, vnew, cache_k, cache_v)


def _lambda(lam_ref, lam_init):
    lp = lam_ref[...]
    return (jnp.exp(jnp.sum(lp[0:1, :] * lp[1:2, :], axis=1, keepdims=True))
            - jnp.exp(jnp.sum(lp[2:3, :] * lp[3:4, :], axis=1, keepdims=True)) + lam_init)


def _subln(o, subg, lam_init):
    return o * lax.rsqrt(jnp.mean(o * o, axis=-1, keepdims=True) + EPS) * subg * (1.0 - lam_init)


def _diff_seq_kernel(q_ref, k_ref, v_ref, lam_ref, subg_ref, o_ref, m_ref, l_ref, acc_ref,
                     *, n_heads, n_kv, lam_init, tq):
    i = pl.program_id(0)
    grp = n_heads // n_kv
    hd = HEAD_DIM
    vd = 2 * hd
    scale = hd ** -0.5
    m_ref[...] = jnp.full(m_ref.shape, -jnp.inf, F32)
    l_ref[...] = jnp.zeros_like(l_ref)
    acc_ref[...] = jnp.zeros_like(acc_ref)
    row = lax.broadcasted_iota(I32, (grp * tq, tq), 0) % tq
    col = lax.broadcasted_iota(I32, (grp * tq, tq), 1)
    causal = col <= row

    def step(j, masked):
        kb = k_ref[j]
        vb = v_ref[j]
        for kv in range(n_kv):
            for c in range(2):
                idx = kv * 2 + c
                q2 = jnp.concatenate(
                    [q_ref[:, ((kv * grp + g) * 2 + c) * hd:((kv * grp + g) * 2 + c + 1) * hd] for g in range(grp)],
                    axis=0)
                s = _dot_nt(q2, kb[:, idx * hd:(idx + 1) * hd]) * scale
                if masked:
                    s = jnp.where(causal, s, -jnp.inf)
                m_old = m_ref[idx]
                m_new = jnp.maximum(m_old, jnp.max(s, axis=1, keepdims=True))
                alpha = jnp.exp(m_old - m_new)
                p = jnp.exp(s - m_new)
                l_ref[idx] = alpha * l_ref[idx] + jnp.sum(p, axis=1, keepdims=True)
                acc_ref[idx] = alpha * acc_ref[idx] + _dot(p.astype(BF16), vb[:, kv * vd:(kv + 1) * vd])
                m_ref[idx] = m_new

    def body(j, carry):
        step(j, False)
        return carry

    lax.fori_loop(0, i, body, 0)
    step(i, True)
    lam = _lambda(lam_ref, lam_init)
    for kv in range(n_kv):
        o0 = acc_ref[kv * 2] / l_ref[kv * 2]
        o1 = acc_ref[kv * 2 + 1] / l_ref[kv * 2 + 1]
        o = _subln(o0 - lam * o1, subg_ref[...], lam_init)
        for g in range(grp):
            h = kv * grp + g
            o_ref[:, h * vd:(h + 1) * vd] = o[g * tq:(g + 1) * tq, :]


def _resident(shape):
    return pl.BlockSpec(shape, lambda *_: (0,) * len(shape), pipeline_mode=pl.Buffered(1))


def _diff_seq(q_b, k_b, v_b, c_lam, sub_g, lam_init, n_heads, n_kv):
    t = q_b.shape[0]
    tq = min(256, t)
    nb = t // tq
    kw = k_b.shape[1]
    vd = 2 * HEAD_DIM
    grp = n_heads // n_kv
    return pl.pallas_call(
        functools.partial(_diff_seq_kernel, n_heads=n_heads, n_kv=n_kv, lam_init=lam_init, tq=tq),
        out_shape=jax.ShapeDtypeStruct((t, n_heads * vd), F32),
        grid=(nb,),
        in_specs=[pl.BlockSpec((tq, q_b.shape[1]), lambda i: (i, 0)),
                  _resident((nb, tq, kw)), _resident((nb, tq, v_b.shape[1])),
                  pl.BlockSpec(c_lam.shape, lambda i: (0, 0)),
                  pl.BlockSpec((1, vd), lambda i: (0, 0))],
        out_specs=pl.BlockSpec((tq, n_heads * vd), lambda i: (i, 0)),
        scratch_shapes=[pltpu.VMEM((2 * n_kv, grp * tq, 1), F32), pltpu.VMEM((2 * n_kv, grp * tq, 1), F32),
                        pltpu.VMEM((2 * n_kv, grp * tq, vd), F32)],
        compiler_params=_cparams("parallel"),
        name="diff_seq",
    )(q_b, k_b.reshape(nb, tq, kw), v_b.reshape(nb, tq, v_b.shape[1]), c_lam, sub_g.reshape(1, vd))


def _sort_key(score):
    bits = pltpu.bitcast(score, I32)
    bits = jnp.where(score == 0.0, 0, bits)
    return jnp.where(bits < 0, bits ^ 0x7FFFFFFF, bits)


def _count_ge(s_ref, nblk, cand, shape, strict=False):
    rows, width = shape

    def body(kb, acc):
        for cg in range(width // LANES):
            key = s_ref[kb, :, cg * LANES:(cg + 1) * LANES]
            hit = (key > cand) if strict else (key >= cand)
            acc = acc + jnp.where(hit, 1.0, 0.0)
        return acc
    acc = lax.fori_loop(0, nblk, body, jnp.zeros((rows, LANES), F32))
    return jnp.sum(acc, axis=1, keepdims=True)


def _kth_largest(s_ref, nblk, n_sel, shape):
    rows = shape[0]

    def body(_, carry):
        ans, bit = carry
        cand = ans + bit
        cnt = _count_ge(s_ref, nblk, cand, shape)
        return jnp.where(cnt >= n_sel, cand, ans), lax.shift_right_logical(bit, 1)

    ans, _ = lax.fori_loop(0, 32, body, (jnp.full((rows, 1), INT_MIN, I32), jnp.int32(INT_MIN)))
    return ans


def _upper_ones(n):
    r = lax.broadcasted_iota(I32, (n, n), 0)
    c = lax.broadcasted_iota(I32, (n, n), 1)
    return (r <= c).astype(BF16)


def _select_ties(key, kth, need, carry, upper):
    eqf = jnp.where(key == kth, 1.0, 0.0)
    rank = carry + _dot(eqf.astype(BF16), upper)
    sel = (key > kth) | ((key == kth) & (rank <= need))
    return sel, carry + jnp.sum(eqf, axis=1, keepdims=True)


def _dsa_seq_kernel(iq_ref, iw_ref, dq_ref, ik_ref, dk_ref, dv_ref, o_ref, s_ref, m_ref, l_ref, acc_ref,
                    *, n_heads, n_kv, n_idx, n_sel, tq, tk):
    i = pl.program_id(0)
    hd = HEAD_DIM
    grp = n_heads // n_kv
    scale = hd ** -0.5
    nblk = (i * tq + tq + tk - 1) // tk
    last = nblk - 1
    qpos = i * tq + lax.broadcasted_iota(I32, (tq, tk), 0)
    col = lax.broadcasted_iota(I32, (tq, tk), 1)
    iw = iw_ref[...] * (n_idx ** -0.5)

    def idx_scores(kb):
        ikb = ik_ref[kb][:, 0:hd]
        score = None
        for h in range(n_idx):
            sc = jnp.maximum(_dot_nt(iq_ref[:, h * hd:(h + 1) * hd], ikb) * scale, 0.0)
            term = iw[:, h:h + 1] * sc
            score = term if score is None else score + term
        return score

    def fill(kb, carry):
        s_ref[kb] = _sort_key(idx_scores(kb))
        return carry

    lax.fori_loop(0, last, fill, 0)
    causal_last = last * tk + col <= qpos
    s_ref[last] = _sort_key(jnp.where(causal_last, idx_scores(last), -jnp.inf))

    shape = (tq, tk)
    kth = _kth_largest(s_ref, nblk, n_sel, shape)
    n_ge = _count_ge(s_ref, nblk, kth, shape)
    n_gt = _count_ge(s_ref, nblk, kth, shape, strict=True)
    need = n_sel - n_gt
    ties = jnp.max(n_ge) > n_sel

    m_ref[...] = jnp.full(m_ref.shape, -jnp.inf, F32)
    l_ref[...] = jnp.zeros_like(l_ref)
    acc_ref[...] = jnp.zeros_like(acc_ref)

    def attend(kb, sel):
        dkb = dk_ref[kb]
        dvb = dv_ref[kb]
        for h in range(n_heads):
            kv = h // grp
            s = _dot_nt(dq_ref[:, h * hd:(h + 1) * hd], dkb[:, kv * hd:(kv + 1) * hd]) * scale
            s = jnp.where(sel, s, -jnp.inf)
            m_old = m_ref[h]
            m_new = jnp.maximum(m_old, jnp.max(s, axis=1, keepdims=True))
            m_safe = jnp.where(m_new == -jnp.inf, 0.0, m_new)
            alpha = jnp.exp(m_old - m_safe)
            p = jnp.exp(s - m_safe)
            l_ref[h] = alpha * l_ref[h] + jnp.sum(p, axis=1, keepdims=True)
            acc_ref[h] = alpha * acc_ref[h] + _dot(p.astype(BF16), dvb[:, kv * hd:(kv + 1) * hd])
            m_ref[h] = m_new

    @pl.when(jnp.logical_not(ties))
    def _():
        def body(kb, carry):
            attend(kb, s_ref[kb] >= kth)
            return carry
        lax.fori_loop(0, last, body, 0)
        attend(last, (s_ref[last] >= kth) & causal_last)

    @pl.when(ties)
    def _():
        upper = _upper_ones(tk)

        def body(kb, carry):
            sel, carry = _select_ties(s_ref[kb], kth, need, carry, upper)
            attend(kb, sel)
            return carry
        carry = lax.fori_loop(0, last, body, jnp.zeros((tq, 1), F32))
        sel, _ = _select_ties(s_ref[last], kth, need, carry, upper)
        attend(last, sel & causal_last)

    for h in range(n_heads):
        o_ref[:, h * hd:(h + 1) * hd] = acc_ref[h] / l_ref[h]


def _dsa_seq(iq_b, iw, dq_b, ik_b, dk_b, dv_b, n_heads, n_kv, n_idx, n_sel):
    t = iq_b.shape[0]
    tq = min(128, t)
    tk = min(256, t)
    nkb = t // tk
    kvw = n_kv * HEAD_DIM
    return pl.pallas_call(
        functools.partial(_dsa_seq_kernel, n_heads=n_heads, n_kv=n_kv, n_idx=n_idx, n_sel=n_sel, tq=tq, tk=tk),
        out_shape=jax.ShapeDtypeStruct((t, n_heads * HEAD_DIM), F32),
        grid=(t // tq,),
        in_specs=[pl.BlockSpec((tq, iq_b.shape[1]), lambda i: (i, 0)),
                  pl.BlockSpec((tq, LANES), lambda i: (i, 0)),
                  pl.BlockSpec((tq, dq_b.shape[1]), lambda i: (i, 0)),
                  _resident((nkb, tk, LANES)), _resident((nkb, tk, kvw)), _resident((nkb, tk, kvw))],
        out_specs=pl.BlockSpec((tq, n_heads * HEAD_DIM), lambda i: (i, 0)),
        scratch_shapes=[pltpu.VMEM((nkb, tq, tk), I32),
                        pltpu.VMEM((n_heads, tq, 1), F32), pltpu.VMEM((n_heads, tq, 1), F32),
                        pltpu.VMEM((n_heads, tq, HEAD_DIM), F32)],
        compiler_params=_cparams("parallel"),
        name="dsa_seq",
    )(iq_b, iw, dq_b, ik_b.reshape(nkb, tk, LANES), dk_b.reshape(nkb, tk, kvw), dv_b.reshape(nkb, tk, kvw))


def _page_specs(n, width, layer, n_pages, ch):
    def spec(p):
        return pl.BlockSpec((None, None, PAGE, width),
                            lambda b, c, pt: (layer, pt[b * n_pages + c * ch + p], 0, 0))
    return [spec(p) for p in range(n)]


def _paged_attn_kernel(pt_ref, *refs, ch, mode, has_mask, lam_init, n_heads, n_kv):
    pos = 0
    qm_ref, knew_ref, vnew_ref, selnew_ref = refs[0:4]
    pos = 4
    mask_ref = None
    if has_mask:
        mask_ref = refs[pos]
        pos += 1
    lam_ref = subg_ref = None
    if mode == "diff":
        lam_ref, subg_ref = refs[pos], refs[pos + 1]
        pos += 2
    k_refs = refs[pos:pos + ch]
    v_refs = refs[pos + ch:pos + 2 * ch]
    o_ref = refs[pos + 2 * ch]
    m_ref, l_ref, acc_ref = refs[pos + 2 * ch + 1:pos + 2 * ch + 4]
    c = pl.program_id(1)
    scale = HEAD_DIM ** -0.5

    @pl.when(c == 0)
    def _():
        m_ref[...] = jnp.full(m_ref.shape, -jnp.inf, F32)
        l_ref[...] = jnp.zeros_like(l_ref)
        acc_ref[...] = jnp.zeros_like(acc_ref)

    qm = qm_ref[0]
    qb = qm.astype(BF16)
    s = jnp.concatenate([_dot_nt(qb, k_refs[p][...].astype(BF16)) for p in range(ch)], axis=1) * scale
    if has_mask:
        s = jnp.where(mask_ref[0] > 0.0, s, -jnp.inf)
    m_old = m_ref[...]
    m_new = jnp.maximum(m_old, jnp.max(s, axis=1, keepdims=True))
    m_safe = jnp.where(m_new == -jnp.inf, 0.0, m_new)
    alpha = jnp.exp(m_old - m_safe)
    p = jnp.exp(s - m_safe).astype(BF16)
    l_ref[...] = alpha * l_ref[...] + jnp.sum(p.astype(F32), axis=1, keepdims=True)
    pv = _dot(p[:, 0:PAGE], v_refs[0][...].astype(BF16))
    for j in range(1, ch):
        pv = pv + _dot(p[:, j * PAGE:(j + 1) * PAGE], v_refs[j][...].astype(BF16))
    acc_ref[...] = alpha * acc_ref[...] + pv
    m_ref[...] = m_new

    @pl.when(c == pl.num_programs(1) - 1)
    def _():
        s_new = jnp.sum(qm * knew_ref[0], axis=1, keepdims=True) * scale
        s_new = jnp.where(selnew_ref[0][:, 0:1] > 0.0, s_new, -jnp.inf)
        m_o = m_ref[...]
        m_f = jnp.maximum(m_o, s_new)
        m_s = jnp.where(m_f == -jnp.inf, 0.0, m_f)
        al = jnp.exp(m_o - m_s)
        p_new = jnp.exp(s_new - m_s)
        l = al * l_ref[...] + p_new
        o = (al * acc_ref[...] + p_new * vnew_ref[0]) / l
        rows = o.shape[0]
        r = lax.broadcasted_iota(I32, (rows, 1), 0)
        if mode == "diff":
            vd = 2 * HEAD_DIM
            grp = n_heads // n_kv
            kv_of = (r % n_heads) // grp
            osel = o[:, 0:vd]
            for kv in range(1, n_kv):
                osel = jnp.where(kv_of == kv, o[:, kv * vd:(kv + 1) * vd], osel)
            lam = _lambda(lam_ref, lam_init)
            od = osel[0:n_heads, :] - lam * osel[n_heads:2 * n_heads, :]
            o_ref[0] = _subln(od, subg_ref[...], lam_init)
        else:
            grp = n_heads // n_kv
            kv_of = r // grp
            osel = o[:, 0:HEAD_DIM]
            for kv in range(1, n_kv):
                osel = jnp.where(kv_of == kv, o[:, kv * HEAD_DIM:(kv + 1) * HEAD_DIM], osel)
            o_ref[0] = osel


def _paged_attn(page_table, layer, qm, knew, vnew, selnew, mask, cache_k, cache_v, *, mode, n_heads, n_kv,
                lam=None, sub_g=None, lam_init=0.0):
    nb, n_pages = page_table.shape
    rows, dk = qm.shape[1:]
    dv = cache_v.shape[-1]
    ch = min(16, n_pages)
    nc = n_pages // ch
    has_mask = mask is not None
    per_b = lambda shape: pl.BlockSpec((1,) + shape, lambda b, c, pt: (b,) + (0,) * len(shape))
    in_specs = [per_b((rows, dk)), per_b((1, dk)), per_b((1, dv)), per_b((1, LANES))]
    args = [qm, knew, vnew, selnew]
    if has_mask:
        in_specs.append(pl.BlockSpec((1, 1, ch * PAGE), lambda b, c, pt: (b, 0, c)))
        args.append(mask)
    if mode == "diff":
        in_specs += [pl.BlockSpec(lam.shape, lambda b, c, pt: (0, 0)),
                     pl.BlockSpec((1, 2 * HEAD_DIM), lambda b, c, pt: (0, 0))]
        args += [lam, sub_g.reshape(1, -1)]
        out_rows, out_w = n_heads, 2 * HEAD_DIM
    else:
        out_rows, out_w = n_heads, HEAD_DIM
    in_specs += _page_specs(ch, dk, layer, n_pages, ch) + _page_specs(ch, dv, layer, n_pages, ch)
    args += [cache_k] * ch + [cache_v] * ch
    grid_spec = pltpu.PrefetchScalarGridSpec(
        num_scalar_prefetch=1, grid=(nb, nc), in_specs=in_specs,
        out_specs=pl.BlockSpec((1, out_rows, out_w), lambda b, c, pt: (b, 0, 0)),
        scratch_shapes=[pltpu.VMEM((rows, 1), F32), pltpu.VMEM((rows, 1), F32), pltpu.VMEM((rows, dv), F32)])
    return pl.pallas_call(
        functools.partial(_paged_attn_kernel, ch=ch, mode=mode, has_mask=has_mask, lam_init=lam_init,
                          n_heads=n_heads, n_kv=n_kv),
        out_shape=jax.ShapeDtypeStruct((nb, out_rows, out_w), F32),
        grid_spec=grid_spec,
        compiler_params=_cparams("parallel", "arbitrary"),
        name="paged_attn_" + mode,
    )(page_table.reshape(-1), *args)


def _paged_idx_kernel(pt_ref, iq_ref, w_ref, *refs, ch, n_idx):
    pages = refs[:ch]
    o_ref = refs[ch]
    scale = HEAD_DIM ** -0.5
    iq = iq_ref[0]
    w = w_ref[0] * (n_idx ** -0.5)
    outs = []
    for p in range(ch):
        sc = jnp.maximum(_dot_nt(iq, pages[p][...].astype(BF16)) * scale, 0.0)
        outs.append(jnp.sum(w * sc, axis=0, keepdims=True))
    o_ref[0] = jnp.concatenate(outs, axis=1)


def _paged_idx_scores(page_table, layer, iq, w, cache_idx, n_idx):
    nb, n_pages = page_table.shape
    ch = min(16, n_pages)
    nc = n_pages // ch
    grid_spec = pltpu.PrefetchScalarGridSpec(
        num_scalar_prefetch=1, grid=(nb, nc),
        in_specs=[pl.BlockSpec((1, n_idx, HEAD_DIM), lambda b, c, pt: (b, 0, 0)),
                  pl.BlockSpec((1, n_idx, 1), lambda b, c, pt: (b, 0, 0))]
        + _page_specs(ch, cache_idx.shape[-1], layer, n_pages, ch),
        out_specs=pl.BlockSpec((1, 1, ch * PAGE), lambda b, c, pt: (b, 0, c)))
    return pl.pallas_call(
        functools.partial(_paged_idx_kernel, ch=ch, n_idx=n_idx),
        out_shape=jax.ShapeDtypeStruct((nb, 1, n_pages * PAGE), F32),
        grid_spec=grid_spec,
        compiler_params=_cparams("parallel", "arbitrary"),
        name="paged_idx_scores",
    )(page_table.reshape(-1), iq, w, *([cache_idx] * ch))


def _topk_mask_kernel(sc_ref, iq_ref, ik_ref, w_ref, mask_ref, mnew_ref, s_ref, *, n_idx, n_sel, bw):
    nb, t = sc_ref.shape
    nblk = t // bw
    scale = HEAD_DIM ** -0.5
    prod = iq_ref[...].astype(F32) * ik_ref[...].astype(F32)
    seg_r = lax.broadcasted_iota(I32, (prod.shape[1], LANES), 0) // HEAD_DIM
    seg_c = lax.broadcasted_iota(I32, (prod.shape[1], LANES), 1)
    qk = _dot3(prod, (seg_r == seg_c).astype(BF16))
    s_new = jnp.sum(w_ref[...] * (n_idx ** -0.5) * jnp.maximum(qk * scale, 0.0), axis=1, keepdims=True)
    lane = lax.broadcasted_iota(I32, (nb, bw), 1)
    for kb in range(nblk):
        s_ref[kb] = _sort_key(sc_ref[:, kb * bw:(kb + 1) * bw])
    s_ref[nblk] = _sort_key(jnp.where(lane == 0, s_new, -jnp.inf))
    shape = (nb, bw)
    kth = _kth_largest(s_ref, nblk + 1, n_sel, shape)
    need = n_sel - _count_ge(s_ref, nblk + 1, kth, shape, strict=True)
    upper = _upper_ones(bw)
    carry = jnp.zeros((nb, 1), F32)
    for kb in range(nblk):
        sel, carry = _select_ties(s_ref[kb], kth, need, carry, upper)
        mask_ref[:, kb * bw:(kb + 1) * bw] = jnp.where(sel, 1.0, 0.0)
    sel, _ = _select_ties(s_ref[nblk], kth, need, carry, upper)
    mnew_ref[...] = jnp.where(sel & (lane == 0), 1.0, 0.0)[:, 0:LANES]


def _topk_mask(scores, iq_b, ik_new_b, w, n_idx, n_sel):
    nb, t = scores.shape
    bw = min(256, t)
    full = lambda a: pl.BlockSpec(a.shape, lambda i: (0,) * a.ndim)
    return pl.pallas_call(
        functools.partial(_topk_mask_kernel, n_idx=n_idx, n_sel=n_sel, bw=bw),
        out_shape=(jax.ShapeDtypeStruct((nb, t), F32), jax.ShapeDtypeStruct((nb, LANES), F32)),
        grid=(1,),
        in_specs=[full(scores), full(iq_b), full(ik_new_b), full(w)],
        out_specs=(pl.BlockSpec((nb, t), lambda i: (0, 0)), pl.BlockSpec((nb, LANES), lambda i: (0, 0))),
        scratch_shapes=[pltpu.VMEM((t // bw + 1, nb, bw), I32)],
        compiler_params=_cparams("arbitrary"),
        name="topk_mask",
    )(scores, iq_b, ik_new_b, w)


def _keys_on_lanes(k_b, n_slices, tk):
    t = k_b.shape[0]
    return k_b.reshape(t // tk, tk, n_slices, HEAD_DIM).transpose(2, 0, 3, 1)


def _online_softmax_step(s, m_ref, l_ref, acc_ref, idx, v_blk):
    ncol = s.shape[1] // LANES
    cols = lambda a: [a[:, c * LANES:(c + 1) * LANES] for c in range(ncol)]
    m_old = m_ref[idx]
    m_new = jnp.maximum(m_old, jnp.max(functools.reduce(jnp.maximum, cols(s)), axis=1, keepdims=True))
    m_safe = jnp.where(m_new == -jnp.inf, 0.0, m_new)
    alpha = jnp.exp(m_old - m_safe)
    p = jnp.exp(s - m_safe)
    l_ref[idx] = alpha * l_ref[idx] + jnp.sum(functools.reduce(jnp.add, cols(p)), axis=1, keepdims=True)
    acc_ref[idx] = alpha * acc_ref[idx] + _dot(p.astype(BF16), v_blk)
    m_ref[idx] = m_new


def _diff2_kernel(q_ref, kt_ref, v_ref, lam_ref, subg_ref, o_ref, qs_ref, m_ref, l_ref, acc_ref,
                  *, n_heads, n_kv, lam_init, tq, tk):
    i = pl.program_id(0)
    grp = n_heads // n_kv
    hd = HEAD_DIM
    vd = 2 * hd
    scale = hd ** -0.5
    m_ref[...] = jnp.full(m_ref.shape, -jnp.inf, F32)
    l_ref[...] = jnp.zeros_like(l_ref)
    acc_ref[...] = jnp.zeros_like(acc_ref)
    for kv in range(n_kv):
        for c in range(2):
            qs_ref[kv * 2 + c] = jnp.concatenate(
                [q_ref[:, ((kv * grp + g) * 2 + c) * hd:((kv * grp + g) * 2 + c + 1) * hd] for g in range(grp)],
                axis=0) * scale
    n_full = (i * tq) // tk

    def step(j, masked):
        vb = v_ref[j]
        if masked:
            qpos = i * tq + lax.broadcasted_iota(I32, (grp, tq, tk), 1)
            kpos = j * tk + lax.broadcasted_iota(I32, (grp, tq, tk), 2)
            ok = (kpos <= qpos).reshape(grp * tq, tk)
        for kv in range(n_kv):
            for c in range(2):
                idx = kv * 2 + c
                s = _dot(qs_ref[idx], kt_ref[idx, j])
                if masked:
                    s = jnp.where(ok, s, -jnp.inf)
                _online_softmax_step(s, m_ref, l_ref, acc_ref, idx, vb[:, kv * vd:(kv + 1) * vd])

    def body(j, carry):
        step(j, False)
        return carry

    lax.fori_loop(0, n_full, body, 0)
    step(n_full, True)
    lam = _lambda(lam_ref, lam_init)
    for kv in range(n_kv):
        o0 = acc_ref[kv * 2] / l_ref[kv * 2]
        o1 = acc_ref[kv * 2 + 1] / l_ref[kv * 2 + 1]
        o = _subln(o0 - lam * o1, subg_ref[...], lam_init)
        for g in range(grp):
            h = kv * grp + g
            o_ref[:, h * vd:(h + 1) * vd] = o[g * tq:(g + 1) * tq, :]


def _diff_seq2(q_b, k_b, v_b, c_lam, sub_g, lam_init, n_heads, n_kv):
    t = q_b.shape[0]
    tq = min(256, t)
    tk = min(512, t)
    nkb = t // tk
    vd = 2 * HEAD_DIM
    grp = n_heads // n_kv
    kt = _keys_on_lanes(k_b, 2 * n_kv, tk)
    return pl.pallas_call(
        functools.partial(_diff2_kernel, n_heads=n_heads, n_kv=n_kv, lam_init=lam_init, tq=tq, tk=tk),
        out_shape=jax.ShapeDtypeStruct((t, n_heads * vd), F32),
        grid=(t // tq,),
        in_specs=[pl.BlockSpec((tq, q_b.shape[1]), lambda i: (i, 0)),
                  _resident(kt.shape), _resident((nkb, tk, v_b.shape[1])),
                  pl.BlockSpec(c_lam.shape, lambda i: (0, 0)),
                  pl.BlockSpec((1, vd), lambda i: (0, 0))],
        out_specs=pl.BlockSpec((tq, n_heads * vd), lambda i: (i, 0)),
        scratch_shapes=[pltpu.VMEM((2 * n_kv, grp * tq, HEAD_DIM), BF16),
                        pltpu.VMEM((2 * n_kv, grp * tq, 1), F32), pltpu.VMEM((2 * n_kv, grp * tq, 1), F32),
                        pltpu.VMEM((2 * n_kv, grp * tq, vd), F32)],
        compiler_params=_cparams("parallel"),
        name="diff_seq",
    )(q_b, kt, v_b.reshape(nkb, tk, v_b.shape[1]), c_lam, sub_g.reshape(1, vd))


def _dsa2_kernel(iq_ref, iw_ref, dq_ref, ikt_ref, dkt_ref, dv_ref, o_ref,
                 s_ref, iqs_ref, wrep_ref, qs_ref, m_ref, l_ref, acc_ref,
                 *, n_heads, n_kv, n_idx, n_sel, tq, tk):
    i = pl.program_id(0)
    hd = HEAD_DIM
    grp = n_heads // n_kv
    scale = hd ** -0.5
    nblk = (i * tq + tq + tk - 1) // tk
    last = nblk - 1
    ncol = tk // LANES

    iw = iw_ref[...] * ((n_idx ** -0.5) * scale)
    for h in range(n_idx):
        iqs_ref[h] = iq_ref[:, h * hd:(h + 1) * hd]
        wrep_ref[h] = jnp.broadcast_to(iw[:, h:h + 1], (tq, LANES))
    for kv in range(n_kv):
        qs_ref[kv] = jnp.concatenate(
            [dq_ref[:, (kv * grp + g) * hd:(kv * grp + g + 1) * hd] for g in range(grp)], axis=0) * scale

    def idx_scores(kb):
        ikb = ikt_ref[kb]
        score = None
        for h in range(n_idx):
            w = jnp.concatenate([wrep_ref[h]] * ncol, axis=1)
            term = jnp.maximum(_dot(iqs_ref[h], ikb), 0.0) * w
            score = term if score is None else score + term
        return score

    def fill(kb, carry):
        s_ref[kb] = _sort_key(idx_scores(kb))
        return carry

    lax.fori_loop(0, last, fill, 0)
    qpos = i * tq + lax.broadcasted_iota(I32, (tq, tk), 0)
    causal_last = last * tk + lax.broadcasted_iota(I32, (tq, tk), 1) <= qpos
    s_ref[last] = _sort_key(jnp.where(causal_last, idx_scores(last), -jnp.inf))

    shape = (tq, tk)

    def bis_cond(c):
        it, _, _, _, done = c
        return jnp.logical_and(it < 32, jnp.logical_not(done))

    def bis_body(c):
        it, ans, bit, cnt_ans, _ = c
        cand = ans + bit
        cnt = _count_ge(s_ref, nblk, cand, shape)
        take = cnt >= n_sel
        ans = jnp.where(take, cand, ans)
        cnt_ans = jnp.where(take, cnt, cnt_ans)
        done = jnp.min(jnp.where(cnt_ans == n_sel, 1.0, 0.0)) > 0.5
        return it + 1, ans, lax.shift_right_logical(bit, 1), cnt_ans, done

    init = (jnp.int32(0), jnp.full((tq, 1), INT_MIN, I32), jnp.int32(INT_MIN),
            jnp.full((tq, 1), 1.0, F32) * (nblk * tk).astype(F32), jnp.bool_(False))
    _, kth, _, n_ge, _ = lax.while_loop(bis_cond, bis_body, init)
    ties = jnp.max(n_ge) > n_sel

    m_ref[...] = jnp.full(m_ref.shape, -jnp.inf, F32)
    l_ref[...] = jnp.zeros_like(l_ref)
    acc_ref[...] = jnp.zeros_like(acc_ref)

    def attend(kb, sel):
        dvb = dv_ref[kb]
        for kv in range(n_kv):
            s = _dot(qs_ref[kv], dkt_ref[kv, kb]).reshape(grp, tq, tk)
            s = jnp.where(sel[None], s, -jnp.inf).reshape(grp * tq, tk)
            _online_softmax_step(s, m_ref, l_ref, acc_ref, kv, dvb)

    @pl.when(jnp.logical_not(ties))
    def _():
        def body(kb, carry):
            attend(kb, s_ref[kb] >= kth)
            return carry
        lax.fori_loop(0, last, body, 0)
        attend(last, (s_ref[last] >= kth) & causal_last)

    @pl.when(ties)
    def _():
        need = n_sel - _count_ge(s_ref, nblk, kth, shape, strict=True)
        upper = _upper_ones(tk)

        def body(kb, carry):
            sel, carry = _select_ties(s_ref[kb], kth, need, carry, upper)
            attend(kb, sel)
            return carry
        carry = lax.fori_loop(0, last, body, jnp.zeros((tq, 1), F32))
        sel, _ = _select_ties(s_ref[last], kth, need, carry, upper)
        attend(last, sel & causal_last)

    for kv in range(n_kv):
        o = acc_ref[kv] / l_ref[kv]
        for g in range(grp):
            h = kv * grp + g
            o_ref[:, h * hd:(h + 1) * hd] = o[g * tq:(g + 1) * tq, kv * hd:(kv + 1) * hd]


def _dsa_seq2(iq_b, iw, dq_b, ik_b, dk_b, dv_b, n_heads, n_kv, n_idx, n_sel):
    t = iq_b.shape[0]
    tq = min(128, t)
    tk = min(512, t)
    nkb = t // tk
    kvw = n_kv * HEAD_DIM
    grp = n_heads // n_kv
    ikt = _keys_on_lanes(ik_b[:, :HEAD_DIM], 1, tk)[0]
    dkt = _keys_on_lanes(dk_b, n_kv, tk)
    return pl.pallas_call(
        functools.partial(_dsa2_kernel, n_heads=n_heads, n_kv=n_kv, n_idx=n_idx, n_sel=n_sel, tq=tq, tk=tk),
        out_shape=jax.ShapeDtypeStruct((t, n_heads * HEAD_DIM), F32),
        grid=(t // tq,),
        in_specs=[pl.BlockSpec((tq, iq_b.shape[1]), lambda i: (i, 0)),
                  pl.BlockSpec((tq, LANES), lambda i: (i, 0)),
                  pl.BlockSpec((tq, dq_b.shape[1]), lambda i: (i, 0)),
                  _resident(ikt.shape), _resident(dkt.shape), _resident((nkb, tk, kvw))],
        out_specs=pl.BlockSpec((tq, n_heads * HEAD_DIM), lambda i: (i, 0)),
        scratch_shapes=[pltpu.VMEM((nkb, tq, tk), I32),
                        pltpu.VMEM((n_idx, tq, HEAD_DIM), BF16), pltpu.VMEM((n_idx, tq, LANES), F32),
                        pltpu.VMEM((n_kv, grp * tq, HEAD_DIM), BF16),
                        pltpu.VMEM((n_kv, grp * tq, 1), F32), pltpu.VMEM((n_kv, grp * tq, 1), F32),
                        pltpu.VMEM((n_kv, grp * tq, kvw), F32)],
        compiler_params=_cparams("parallel"),
        name="dsa_seq",
    )(iq_b, iw, dq_b, ikt, dkt, dv_b.reshape(nkb, tk, kvw))


LOG2E = 1.4426950408889634


def _flash_t(s_t, m_ref, l_ref, acc_ref, idx, v_t):
    m_old = m_ref[idx]
    m_new = jnp.maximum(m_old, jnp.max(s_t, axis=0, keepdims=True))
    m_safe = jnp.where(m_new == -jnp.inf, 0.0, m_new)
    alpha = jnp.exp2(m_old - m_safe)
    p = jnp.exp2(s_t - m_safe)
    l_ref[idx] = alpha * l_ref[idx] + jnp.sum(p, axis=0, keepdims=True)
    acc_ref[idx] = alpha * acc_ref[idx] + _dot(v_t, p.astype(BF16))
    m_ref[idx] = m_new


def _diff_t_kernel(qt_ref, k_ref, vt_ref, lam_ref, subg_ref, o_ref, qs_ref, m_ref, l_ref, acc_ref,
                   *, n_heads, n_kv, lam_init, tq, tk):
    i = pl.program_id(0)
    grp = n_heads // n_kv
    hd = HEAD_DIM
    vd = 2 * hd
    nq = grp * tq
    m_ref[...] = jnp.full(m_ref.shape, -jnp.inf, F32)
    l_ref[...] = jnp.zeros_like(l_ref)
    acc_ref[...] = jnp.zeros_like(acc_ref)
    qs_ref[...] = jnp.zeros_like(qs_ref)
    for idx in range(2 * n_kv):
        qs_ref[idx, idx * hd:(idx + 1) * hd, :] = (qt_ref[idx] * (hd ** -0.5 * LOG2E)).astype(BF16)
    n_full = (i * tq) // tk

    def step(j, masked):
        kb = k_ref[j]
        vtb = vt_ref[j]
        if masked:
            kpos = j * tk + lax.broadcasted_iota(I32, (tk, tq), 0)
            qpos = i * tq + lax.broadcasted_iota(I32, (tk, tq), 1)
            ok = jnp.concatenate([kpos <= qpos] * grp, axis=1)
        for kv in range(n_kv):
            for c in range(2):
                idx = kv * 2 + c
                s = _dot(kb, qs_ref[idx])
                if masked:
                    s = jnp.where(ok, s, -jnp.inf)
                _flash_t(s, m_ref, l_ref, acc_ref, idx, vtb[kv * vd:(kv + 1) * vd, :])

    def body(j, carry):
        step(j, False)
        return carry

    lax.fori_loop(0, n_full, body, 0)
    step(n_full, True)
    lam = _lambda(lam_ref, lam_init)
    for kv in range(n_kv):
        o0 = acc_ref[kv * 2] / l_ref[kv * 2]
        o1 = acc_ref[kv * 2 + 1] / l_ref[kv * 2 + 1]
        o = o0 - lam * o1
        o = o * lax.rsqrt(jnp.mean(o * o, axis=0, keepdims=True) + EPS) * subg_ref[...] * (1.0 - lam_init)
        for g in range(grp):
            h = kv * grp + g
            o_ref[h * vd:(h + 1) * vd, :] = o[:, g * tq:(g + 1) * tq]


def _diff_seq_t(q_f, k_b, v_b, c_lam, sub_g, lam_init, n_heads, n_kv):
    t = q_f.shape[0]
    tq = min(256, t)
    tk = min(512, t)
    nqb, nkb = t // tq, t // tk
    hd = HEAD_DIM
    vd = 2 * hd
    grp = n_heads // n_kv
    kw = k_b.shape[1]
    qt = q_f.reshape(nqb, tq, n_kv, grp, 2, hd).transpose(2, 4, 0, 5, 3, 1).reshape(2 * n_kv, nqb, hd, grp * tq)
    vt = v_b.reshape(nkb, tk, v_b.shape[1]).transpose(0, 2, 1)
    out_t = pl.pallas_call(
        functools.partial(_diff_t_kernel, n_heads=n_heads, n_kv=n_kv, lam_init=lam_init, tq=tq, tk=tk),
        out_shape=jax.ShapeDtypeStruct((nqb, n_heads * vd, tq), F32),
        grid=(nqb,),
        in_specs=[pl.BlockSpec((2 * n_kv, None, hd, grp * tq), lambda i: (0, i, 0, 0)),
                  _resident((nkb, tk, kw)), _resident(vt.shape),
                  pl.BlockSpec(c_lam.shape, lambda i: (0, 0)),
                  pl.BlockSpec((vd, 1), lambda i: (0, 0))],
        out_specs=pl.BlockSpec((None, n_heads * vd, tq), lambda i: (i, 0, 0)),
        scratch_shapes=[pltpu.VMEM((2 * n_kv, kw, grp * tq), BF16),
                        pltpu.VMEM((2 * n_kv, 1, grp * tq), F32), pltpu.VMEM((2 * n_kv, 1, grp * tq), F32),
                        pltpu.VMEM((2 * n_kv, vd, grp * tq), F32)],
        compiler_params=_cparams("parallel"),
        name="diff_seq",
    )(qt, k_b.reshape(nkb, tk, kw), vt, c_lam, sub_g.reshape(vd, 1))
    return out_t.transpose(0, 2, 1).reshape(t, n_heads * vd)


def _count_ge_t(s_ref, nblk, cand, tk, tq, strict=False):
    rows = 8 * SUBLANES

    def body(kb, acc):
        key = s_ref[kb]
        hit = (key > cand) if strict else (key >= cand)
        return acc + jnp.sum(jnp.where(hit, 1.0, 0.0).reshape(tk // rows, rows, tq), axis=0)
    acc = lax.fori_loop(0, nblk, body, jnp.zeros((rows, tq), F32))
    return jnp.sum(acc, axis=0, keepdims=True)


def _select_ties_t(key, kth, need, carry, lower):
    eqf = jnp.where(key == kth, 1.0, 0.0)
    rank = carry + _dot(lower, eqf.astype(BF16))
    sel = (key > kth) | ((key == kth) & (rank <= need))
    return sel, carry + jnp.sum(eqf, axis=0, keepdims=True)


def _dsa_t_kernel(iqt_ref, iwt_ref, dqt_ref, ik_ref, dk_ref, dvt_ref, o_ref,
                  s_ref, iqs_ref, qs_ref, m_ref, l_ref, acc_ref,
                  *, n_heads, n_kv, n_idx, n_sel, tq, tk):
    i = pl.program_id(0)
    hd = HEAD_DIM
    grp = n_heads // n_kv
    nblk = (i * tq + tq + tk - 1) // tk
    last = nblk - 1

    iqs_ref[...] = jnp.zeros_like(iqs_ref)
    qs_ref[...] = jnp.zeros_like(qs_ref)
    for hp in range(n_idx // 2):
        for j in range(2):
            iqs_ref[hp, 0:hd, j * tq:(j + 1) * tq] = iqt_ref[2 * hp + j]
    for kv in range(n_kv):
        qs_ref[kv, kv * hd:(kv + 1) * hd, :] = (dqt_ref[kv] * (hd ** -0.5 * LOG2E)).astype(BF16)
    iw = iwt_ref[...] * ((n_idx ** -0.5) * (hd ** -0.5))

    def idx_scores(kb):
        ikb = ik_ref[kb]
        score = None
        for hp in range(n_idx // 2):
            sc = jnp.maximum(_dot(ikb, iqs_ref[hp]), 0.0)
            for j in range(2):
                h = 2 * hp + j
                term = sc[:, j * tq:(j + 1) * tq] * iw[h:h + 1, :]
                score = term if score is None else score + term
        return score

    def fill(kb, carry):
        s_ref[kb] = _sort_key(idx_scores(kb))
        return carry

    lax.fori_loop(0, last, fill, 0)
    kpos = last * tk + lax.broadcasted_iota(I32, (tk, tq), 0)
    causal_last = kpos <= i * tq + lax.broadcasted_iota(I32, (tk, tq), 1)
    s_ref[last] = _sort_key(jnp.where(causal_last, idx_scores(last), -jnp.inf))

    def bis_cond(c):
        it, _, _, _, done = c
        return jnp.logical_and(it < 32, jnp.logical_not(done))

    def bis_body(c):
        it, ans, bit, cnt_ans, _ = c
        cand = ans + bit
        cnt = _count_ge_t(s_ref, nblk, cand, tk, tq)
        take = cnt >= n_sel
        ans = jnp.where(take, cand, ans)
        cnt_ans = jnp.where(take, cnt, cnt_ans)
        done = jnp.min(jnp.where(cnt_ans == n_sel, 1.0, 0.0)) > 0.5
        return it + 1, ans, lax.shift_right_logical(bit, 1), cnt_ans, done

    init = (jnp.int32(0), jnp.full((1, tq), INT_MIN, I32), jnp.int32(INT_MIN),
            jnp.full((1, tq), 1.0, F32) * (nblk * tk).astype(F32), jnp.bool_(False))
    _, kth, _, n_ge, _ = lax.while_loop(bis_cond, bis_body, init)
    ties = jnp.max(n_ge) > n_sel

    m_ref[...] = jnp.full(m_ref.shape, -jnp.inf, F32)
    l_ref[...] = jnp.zeros_like(l_ref)
    acc_ref[...] = jnp.zeros_like(acc_ref)

    def attend(kb, sel):
        dkb = dk_ref[kb]
        dvtb = dvt_ref[kb]
        bias = jnp.where(sel, 0.0, -jnp.inf)
        bias_g = jnp.concatenate([bias] * grp, axis=1)
        for kv in range(n_kv):
            s = _dot(dkb, qs_ref[kv]) + bias_g
            _flash_t(s, m_ref, l_ref, acc_ref, kv, dvtb[kv * hd:(kv + 1) * hd, :])

    @pl.when(jnp.logical_not(ties))
    def _():
        def body(kb, carry):
            attend(kb, s_ref[kb] >= kth)
            return carry
        lax.fori_loop(0, last, body, 0)
        attend(last, (s_ref[last] >= kth) & causal_last)

    @pl.when(ties)
    def _():
        need = n_sel - _count_ge_t(s_ref, nblk, kth, tk, tq, strict=True)
        r = lax.broadcasted_iota(I32, (tk, tk), 0)
        cidx = lax.broadcasted_iota(I32, (tk, tk), 1)
        lower = (cidx <= r).astype(BF16)

        def body(kb, carry):
            sel, carry = _select_ties_t(s_ref[kb], kth, need, carry, lower)
            attend(kb, sel)
            return carry
        carry = lax.fori_loop(0, last, body, jnp.zeros((1, tq), F32))
        sel, _ = _select_ties_t(s_ref[last], kth, need, carry, lower)
        attend(last, sel & causal_last)

    for kv in range(n_kv):
        o = acc_ref[kv] / l_ref[kv]
        for g in range(grp):
            h = kv * grp + g
            o_ref[h * hd:(h + 1) * hd, :] = o[:, g * tq:(g + 1) * tq]


def _dsa_seq_t(iq_b, iw, dq_f, ik_b, dk_b, dv_b, n_heads, n_kv, n_idx, n_sel):
    t = iq_b.shape[0]
    tq = min(128, t)
    tk = min(512, t)
    nqb, nkb = t // tq, t // tk
    hd = HEAD_DIM
    kvw = n_kv * hd
    grp = n_heads // n_kv
    assert n_idx % 2 == 0 and kvw == LANES
    iqt = iq_b.reshape(nqb, tq, n_idx, hd).transpose(2, 0, 3, 1)
    iwt = iw[:, :n_idx].reshape(nqb, tq, n_idx).transpose(0, 2, 1)
    dqt = dq_f.reshape(nqb, tq, n_kv, grp, hd).transpose(2, 0, 4, 3, 1).reshape(n_kv, nqb, hd, grp * tq)
    dvt = dv_b.reshape(nkb, tk, kvw).transpose(0, 2, 1)
    out_t = pl.pallas_call(
        functools.partial(_dsa_t_kernel, n_heads=n_heads, n_kv=n_kv, n_idx=n_idx, n_sel=n_sel, tq=tq, tk=tk),
        out_shape=jax.ShapeDtypeStruct((nqb, n_heads * hd, tq), F32),
        grid=(nqb,),
        in_specs=[pl.BlockSpec((n_idx, None, hd, tq), lambda i: (0, i, 0, 0)),
                  pl.BlockSpec((None, n_idx, tq), lambda i: (i, 0, 0)),
                  pl.BlockSpec((n_kv, None, hd, grp * tq), lambda i: (0, i, 0, 0)),
                  _resident((nkb, tk, LANES)), _resident((nkb, tk, kvw)), _resident(dvt.shape)],
        out_specs=pl.BlockSpec((None, n_heads * hd, tq), lambda i: (i, 0, 0)),
        scratch_shapes=[pltpu.VMEM((nkb, tk, tq), I32),
                        pltpu.VMEM((n_idx // 2, LANES, 2 * tq), BF16),
                        pltpu.VMEM((n_kv, kvw, grp * tq), BF16),
                        pltpu.VMEM((n_kv, 1, grp * tq), F32), pltpu.VMEM((n_kv, 1, grp * tq), F32),
                        pltpu.VMEM((n_kv, hd, grp * tq), F32)],
        compiler_params=_cparams("parallel"),
        name="dsa_seq",
    )(iqt, iwt, dqt, ik_b.reshape(nkb, tk, LANES), dk_b.reshape(nkb, tk, kvw), dvt)
    return out_t.transpose(0, 2, 1).reshape(t, n_heads * hd)


def _page_specs2(n, blk, layer, n_pages, ch):
    def spec(p):
        return pl.BlockSpec((None, None) + blk,
                            lambda b, c, pt: (layer, pt[b * n_pages + c * ch + p]) + (0,) * len(blk))
    return [spec(p) for p in range(n)]


def _paged2_kernel(pt_ref, *refs, ch, mode, has_mask, lam_init, n_heads, n_kv):
    qm_ref, knew_ref, vnew_ref, selnew_ref = refs[0:4]
    pos = 4
    mask_ref = None
    if has_mask:
        mask_ref = refs[pos]
        pos += 1
    lam_ref = subg_ref = None
    if mode == "diff":
        lam_ref, subg_ref = refs[pos], refs[pos + 1]
        pos += 2
    k_refs = refs[pos:pos + ch]
    v_refs = refs[pos + ch:pos + 2 * ch]
    o_ref = refs[pos + 2 * ch]
    m_ref, l_ref, acc_ref = refs[pos + 2 * ch + 1:pos + 2 * ch + 4]
    c = pl.program_id(1)
    scale = HEAD_DIM ** -0.5
    dk = qm_ref.shape[2]

    @pl.when(c == 0)
    def _():
        m_ref[...] = jnp.full(m_ref.shape, -jnp.inf, F32)
        l_ref[...] = jnp.zeros_like(l_ref)
        acc_ref[...] = jnp.zeros_like(acc_ref)

    qm = qm_ref[0]
    qb = qm.astype(BF16)
    s = jnp.concatenate([_dot(qb, k_refs[p][...].reshape(dk, PAGE).astype(BF16)) for p in range(ch)],
                        axis=1) * scale
    if has_mask:
        s = jnp.where(mask_ref[0] > 0.0, s, -jnp.inf)
    m_old = m_ref[...]
    m_new = jnp.maximum(m_old, jnp.max(s, axis=1, keepdims=True))
    m_safe = jnp.where(m_new == -jnp.inf, 0.0, m_new)
    alpha = jnp.exp(m_old - m_safe)
    p = jnp.exp(s - m_safe).astype(BF16)
    l_ref[...] = alpha * l_ref[...] + jnp.sum(p.astype(F32), axis=1, keepdims=True)

    def pv_of(j):
        pj = p[:, j * PAGE:(j + 1) * PAGE]
        if mode == "diff":
            return jnp.concatenate(
                [_dot(pj, v_refs[j][pl.ds(kv, PAGE, stride=n_kv), :].astype(BF16)) for kv in range(n_kv)], axis=1)
        return _dot_nt(pj, v_refs[j][...].reshape(-1, PAGE).astype(BF16))

    pv = pv_of(0)
    for j in range(1, ch):
        pv = pv + pv_of(j)
    acc_ref[...] = alpha * acc_ref[...] + pv
    m_ref[...] = m_new

    @pl.when(c == pl.num_programs(1) - 1)
    def _():
        s_new = jnp.sum(qm * knew_ref[0], axis=1, keepdims=True) * scale
        s_new = jnp.where(selnew_ref[0][:, 0:1] > 0.0, s_new, -jnp.inf)
        m_o = m_ref[...]
        m_f = jnp.maximum(m_o, s_new)
        m_s = jnp.where(m_f == -jnp.inf, 0.0, m_f)
        al = jnp.exp(m_o - m_s)
        p_new = jnp.exp(s_new - m_s)
        l = al * l_ref[...] + p_new
        o = (al * acc_ref[...] + p_new * vnew_ref[0]) / l
        rows = o.shape[0]
        r = lax.broadcasted_iota(I32, (rows, 1), 0)
        grp = n_heads // n_kv
        if mode == "diff":
            vd = 2 * HEAD_DIM
            kv_of = (r % n_heads) // grp
            osel = o[:, 0:vd]
            for kv in range(1, n_kv):
                osel = jnp.where(kv_of == kv, o[:, kv * vd:(kv + 1) * vd], osel)
            lam = _lambda(lam_ref, lam_init)
            od = osel[0:n_heads, :] - lam * osel[n_heads:2 * n_heads, :]
            o_ref[0] = _subln(od, subg_ref[...], lam_init)
        else:
            kv_of = r // grp
            osel = o[:, 0:HEAD_DIM]
            for kv in range(1, n_kv):
                osel = jnp.where(kv_of == kv, o[:, kv * HEAD_DIM:(kv + 1) * HEAD_DIM], osel)
            o_ref[0] = osel


def _paged_attn2(page_table, layer, qm, knew, vnew, selnew, mask, cache_kt, cache_v, *, mode, n_heads, n_kv,
                 lam=None, sub_g=None, lam_init=0.0):
    nb, n_pages = page_table.shape
    rows, dk = qm.shape[1:]
    dv = vnew.shape[-1]
    ch = min(16, n_pages)
    nc = n_pages // ch
    has_mask = mask is not None
    per_b = lambda shape: pl.BlockSpec((1,) + shape, lambda b, c, pt: (b,) + (0,) * len(shape))
    in_specs = [per_b((rows, dk)), per_b((1, dk)), per_b((1, dv)), per_b((1, LANES))]
    args = [qm, knew, vnew, selnew]
    if has_mask:
        in_specs.append(pl.BlockSpec((1, 1, ch * PAGE), lambda b, c, pt: (b, 0, c)))
        args.append(mask)
    if mode == "diff":
        in_specs += [pl.BlockSpec(lam.shape, lambda b, c, pt: (0, 0)),
                     pl.BlockSpec((1, 2 * HEAD_DIM), lambda b, c, pt: (0, 0))]
        args += [lam, sub_g.reshape(1, -1)]
        out_w = 2 * HEAD_DIM
    else:
        out_w = HEAD_DIM
    in_specs += (_page_specs2(ch, cache_kt.shape[2:], layer, n_pages, ch)
                 + _page_specs2(ch, cache_v.shape[2:], layer, n_pages, ch))
    args += [cache_kt] * ch + [cache_v] * ch
    grid_spec = pltpu.PrefetchScalarGridSpec(
        num_scalar_prefetch=1, grid=(nb, nc), in_specs=in_specs,
        out_specs=pl.BlockSpec((1, n_heads, out_w), lambda b, c, pt: (b, 0, 0)),
        scratch_shapes=[pltpu.VMEM((rows, 1), F32), pltpu.VMEM((rows, 1), F32), pltpu.VMEM((rows, dv), F32)])
    return pl.pallas_call(
        functools.partial(_paged2_kernel, ch=ch, mode=mode, has_mask=has_mask, lam_init=lam_init,
                          n_heads=n_heads, n_kv=n_kv),
        out_shape=jax.ShapeDtypeStruct((nb, n_heads, out_w), F32),
        grid_spec=grid_spec,
        compiler_params=_cparams("parallel", "arbitrary"),
        name="paged_attn_" + mode,
    )(page_table.reshape(-1), *args)


def _paged_idx2_kernel(pt_ref, iq_ref, w_ref, *refs, ch, n_idx):
    pages = refs[:ch]
    o_ref = refs[ch]
    scale = HEAD_DIM ** -0.5
    iq = iq_ref[0]
    w = w_ref[0] * (n_idx ** -0.5)
    outs = []
    for p in range(ch):
        sc = jnp.maximum(_dot(iq, pages[p][...].astype(BF16)) * scale, 0.0)
        outs.append(jnp.sum(w * sc, axis=0, keepdims=True))
    o_ref[0] = jnp.concatenate(outs, axis=1)


def _paged_idx_scores2(page_table, layer, iq, w, cache_it, n_idx):
    nb, n_pages = page_table.shape
    ch = min(16, n_pages)
    nc = n_pages // ch
    grid_spec = pltpu.PrefetchScalarGridSpec(
        num_scalar_prefetch=1, grid=(nb, nc),
        in_specs=[pl.BlockSpec((1, n_idx, HEAD_DIM), lambda b, c, pt: (b, 0, 0)),
                  pl.BlockSpec((1, n_idx, 1), lambda b, c, pt: (b, 0, 0))]
        + _page_specs2(ch, cache_it.shape[2:], layer, n_pages, ch),
        out_specs=pl.BlockSpec((1, 1, ch * PAGE), lambda b, c, pt: (b, 0, c)))
    return pl.pallas_call(
        functools.partial(_paged_idx2_kernel, ch=ch, n_idx=n_idx),
        out_shape=jax.ShapeDtypeStruct((nb, 1, n_pages * PAGE), F32),
        grid_spec=grid_spec,
        compiler_params=_cparams("parallel", "arbitrary"),
        name="paged_idx_scores",
    )(page_table.reshape(-1), iq, w, *([cache_it] * ch))


def _bf(w):
    return w.astype(BF16)


def _place(q3, slot_of_row, n_slots):
    onehot = (np.asarray(slot_of_row)[:, None] == np.arange(n_slots)[None, :]).astype(np.float32)
    out = q3[:, :, None, :] * jnp.asarray(onehot)[None, :, :, None]
    return out.reshape(q3.shape[0], q3.shape[1], n_slots * q3.shape[2])


def _even_weights(w_in, d_inner, conv_ch, n_a_heads, qw, kvw):
    c = np.cumsum([0, d_inner, conv_ch, n_a_heads, qw, kvw, kvw])
    z, xbc, dt, q, k, v = (w_in[:, c[j]:c[j + 1]] for j in range(6))
    dt = jnp.pad(dt, ((0, 0), (0, LANES - n_a_heads)))
    w = _bf(jnp.concatenate([z, xbc, q, k, v, dt], axis=1))
    off = np.cumsum([0, d_inner, conv_ch, qw, kvw, kvw])
    return w, dict(z=int(off[0]), xbc=int(off[1]), q=int(off[2]), k=int(off[3]), v=int(off[4]), dt=int(off[5]))


def _odd_weights(w_in, sizes):
    c = np.cumsum([0] + list(sizes))
    cq, ck, cv, dq, dk, dv, iq, iw, ik = (w_in[:, c[j]:c[j + 1]] for j in range(9))
    ik = jnp.pad(ik, ((0, 0), (0, LANES - ik.shape[1])))
    iw = jnp.pad(iw, ((0, 0), (0, LANES - iw.shape[1])))
    parts = [cq, dq, iq, ck, cv, dk, dv, ik, iw]
    off = np.cumsum([0] + [p.shape[1] for p in parts])
    names = ["cq", "dq", "iq", "ck", "cv", "dk", "dv", "ik", "iw"]
    return _bf(jnp.concatenate(parts, axis=1)), {n: int(o) for n, o in zip(names, off[:-1])}


def _mixer_even(x, pos, seq_mode, st_conv, st_ssm, win_k, win_v, norm_g, w_in, conv_w, conv_b, dt_bias, a_log,
                d_skip, gain, qn_g, kn_g, sinks, w_out):
    m, _ = x.shape
    n_a_heads = a_log.shape[0]
    d_inner = gain.shape[0]
    conv_ch = conv_w.shape[1]
    n_heads = sinks.shape[0]
    qw = n_heads * HEAD_DIM
    kvw = (w_in.shape[1] - d_inner - conv_ch - n_a_heads - qw) // 2
    n_kv = kvw // HEAD_DIM
    kw = conv_w.shape[0]
    gn = A_GROUPS * A_STATE
    hpg = n_a_heads // A_GROUPS
    w, off = _even_weights(w_in, d_inner, conv_ch, n_a_heads, qw, kvw)
    cos, sin = _rope_tables(pos)
    proj = _mm([x], [w], norm_g=norm_g, name="in_proj_even")
    q_f, q_b = _norm_rope(proj, off["q"], qw, qn_g, cos, sin, name="swa_q_rope")
    k_f, k_b = _norm_rope(proj, off["k"], kvw, kn_g, cos, sin, name="swa_k_rope")
    v_f = proj[:, off["v"]:off["v"] + kvw]
    xbc_raw = proj[:, off["xbc"]:off["xbc"] + conv_ch]
    if seq_mode:
        state8 = jnp.zeros((SUBLANES, conv_ch), F32)
        xbc = _conv_seq(proj, off["xbc"], conv_ch, state8, conv_w, conv_b)
        s0_t = jnp.zeros((A_GROUPS, A_STATE, hpg * A_HEAD_DIM), F32)
        ya, st = _ssd_seq(xbc, proj, off["z"], off["dt"], dt_bias, a_log, d_skip, gain, s0_t, n_a_heads)
        ssm_new = st.reshape(A_GROUPS, A_STATE, hpg, A_HEAD_DIM).transpose(0, 2, 3, 1).reshape(
            1, n_a_heads, A_HEAD_DIM, A_STATE)
        conv_new = xbc_raw[m - (kw - 1):][None]
        ob = _swa_seq(q_b, k_b, _bf(v_f), sinks, n_heads, n_kv)
        wb = min(WINDOW, m)
        new_k = k_f[m - wb:].reshape(1, wb, n_kv, HEAD_DIM)
        new_v = v_f[m - wb:].reshape(1, wb, n_kv, HEAD_DIM)
    else:
        xbc = _conv_step(proj, off["xbc"], conv_ch, st_conv.transpose(1, 0, 2), conv_w, conv_b)
        xdt, dec = _ssd_step_pre(xbc, proj, off["dt"], dt_bias, a_log, d_inner)
        ssm_new, y = _ssd_step(st_ssm, xdt.T, dec.T, xbc[:, d_inner:d_inner + gn], xbc[:, d_inner + gn:])
        ya = _gated_norm_call(y, xbc, proj, off["z"], d_skip, gain)
        conv_new = jnp.concatenate([st_conv[:, 1:], xbc_raw[:, None, :]], axis=1)
        wb = win_k.shape[1]
        grp = n_heads // n_kv
        qm = _place(q_f.reshape(m, n_heads, HEAD_DIM), [h // grp for h in range(n_heads)], n_kv)
        o, new_k, new_v = _swa_step(qm, k_f[:, None, :], v_f[:, None, :], win_k.reshape(m, wb, kvw),
                                    win_v.reshape(m, wb, kvw), sinks, n_heads, n_kv)
        ob = o.reshape(m, qw)
        new_k = new_k.reshape(m, wb, n_kv, HEAD_DIM)
        new_v = new_v.reshape(m, wb, n_kv, HEAD_DIM)
    wo = _bf(w_out)
    y = _mm([ya, ob], [wo[:d_inner], wo[d_inner:]], res=x, name="out_proj_even")
    return y, (conv_new, ssm_new, new_k, new_v)


def _mixer_odd(x, pos, seq_mode, paged, norm_g, w_in, qn_g, kn_g, lam_p, sub_g, dqn_g, dkn_g, w_out, lam_init,
               sizes, n_sel):
    m, _ = x.shape
    hd = HEAD_DIM
    c_heads = sizes[0] // (2 * hd)
    c_kv = sizes[1] // (2 * hd)
    d_heads = sizes[3] // hd
    d_kv = sizes[4] // hd
    n_idx = sizes[7]
    w, off = _odd_weights(w_in, sizes)
    cos, sin = _rope_tables(pos)
    proj = _mm([x], [w], norm_g=norm_g, name="in_proj_odd")
    cq_f, cq_b = _norm_rope(proj, off["cq"], sizes[0], qn_g, cos, sin, name="diff_q_rope")
    ck_f, ck_b = _norm_rope(proj, off["ck"], sizes[1], kn_g, cos, sin, name="diff_k_rope")
    dq_f, dq_b = _norm_rope(proj, off["dq"], sizes[3], dqn_g, cos, sin, name="dsa_q_rope")
    dk_f, dk_b = _norm_rope(proj, off["dk"], sizes[4], dkn_g, cos, sin, name="dsa_k_rope")
    iq_f, iq_b = _norm_rope(proj, off["iq"], sizes[6], None, cos, sin, name="idx_q_rope")
    ik_f, ik_b = _norm_rope(proj, off["ik"], LANES, None, cos, sin, name="idx_k_rope")
    cv_f = proj[:, off["cv"]:off["cv"] + sizes[2]]
    dv_f = proj[:, off["dv"]:off["dv"] + sizes[5]]
    iw = proj[:, off["iw"]:off["iw"] + LANES]
    if seq_mode:
        oc = _diff_seq_t(cq_f, ck_b, _bf(cv_f), lam_p, sub_g, lam_init, c_heads, c_kv)
        od = _dsa_seq_t(iq_b, iw, dq_f, ik_b, dk_b, _bf(dv_f), d_heads, d_kv, n_idx, n_sel)
    else:
        c_k, c_v, d_k, d_v, d_i, table, layer = paged
        pool = c_k.shape[1]
        ones = jnp.ones((m, 1, LANES), F32)
        grp = c_heads // c_kv
        q4 = cq_f.reshape(m, c_heads, 2, hd).transpose(0, 2, 1, 3).reshape(m, 2 * c_heads, hd)
        slots = [(h // grp) * 2 + c for c in range(2) for h in range(c_heads)]
        qm_c = _place(q4, slots, 2 * c_kv)
        oc = _paged_attn2(table, layer, qm_c, ck_f[:, None, :], cv_f[:, None, :], ones, None,
                          jnp.transpose(c_k, (0, 1, 3, 4, 5, 2)), c_v.reshape(c_v.shape[0], pool, PAGE * c_kv, -1),
                          mode="diff", n_heads=c_heads, n_kv=c_kv, lam=lam_p, sub_g=sub_g, lam_init=lam_init)
        oc = oc.reshape(m, -1)
        scores = _paged_idx_scores2(table, layer, iq_b.reshape(m, n_idx, hd), iw[:, :n_idx, None],
                                    jnp.transpose(d_i, (0, 1, 3, 2)), n_idx)
        ik_tiled = jnp.tile(ik_b[:, :hd], (1, n_idx))
        mask, mnew = _topk_mask(scores.reshape(m, -1), iq_b, ik_tiled, iw, n_idx, n_sel)
        dgrp = d_heads // d_kv
        qm_d = _place(dq_f.reshape(m, d_heads, hd), [h // dgrp for h in range(d_heads)], d_kv)
        od = _paged_attn2(table, layer, qm_d, dk_f[:, None, :], dv_f[:, None, :], mnew[:, None, :], mask[:, None, :],
                          jnp.transpose(d_k, (0, 1, 3, 4, 2)), jnp.transpose(d_v, (0, 1, 3, 4, 2)),
                          mode="gqa", n_heads=d_heads, n_kv=d_kv)
        od = od.reshape(m, -1)
    wo = _bf(w_out)
    y = _mm([oc, od], [wo[:oc.shape[1]], wo[oc.shape[1]:]], res=x, name="out_proj_odd")
    lead = (1, m) if seq_mode else (m, 1)
    caches = (ck_f.reshape(lead + (c_kv, 2, hd)), cv_f.reshape(lead + (c_kv, 2 * hd)),
              dk_f.reshape(lead + (d_kv, hd)), dv_f.reshape(lead + (d_kv, hd)), ik_f[:, :hd].reshape(lead + (hd,)))
    return y, caches


def kernel(x_prompt, x_sample, state_ssm, state_ssm_conv, cache_swa_k, cache_swa_v, cache_c_k, cache_c_v, cache_d_k, cache_d_v, cache_d_idx, state_ffn_conv, page_table, norm_mix_g, norm_ffn_g, a_w_in, a_conv_w, a_conv_b, a_dt_bias, a_A_log, a_D, a_norm_g, b_qn_g, b_kn_g, b_sinks, e_w_out, m_w_in, c_qn_g, c_kn_g, c_lam, c_subln_g, d_qn_g, d_kn_g, m_w_out, ffn_w_gate, ffn_w_up, ffn_conv_w, ffn_conv_b, ffn_w_down):
    bp, seq, d_model = x_prompt.shape
    nb = x_sample.shape[0]
    assert bp == 1 and x_sample.shape[1] == 1
    depth = norm_mix_g.shape[0]
    d_ff = ffn_w_gate.shape[2]
    past = page_table.shape[1] * PAGE
    xp = x_prompt.reshape(seq, d_model)
    xs = x_sample.reshape(nb, d_model)
    pos_p = jnp.arange(seq)
    pos_s = jnp.full((nb,), past, I32)
    hd = HEAD_DIM
    c_kv, d_kv, idx_dim = cache_c_k.shape[3], cache_d_k.shape[3], cache_d_idx.shape[3]
    d_heads = d_model // 128
    c_heads = d_model // 256
    n_idx = m_w_in.shape[2] - (c_heads * 2 * hd + 2 * c_kv * 2 * hd + d_heads * hd + 2 * d_kv * hd
                               + d_heads * hd + idx_dim)
    odd_sizes = (c_heads * 2 * hd, c_kv * 2 * hd, c_kv * 2 * hd, d_heads * hd, d_kv * hd, d_kv * hd,
                 d_heads * hd, n_idx, idx_dim)
    outs_p = {k: [] for k in ("ssm", "cnv", "swk", "swv", "ck", "cv", "dk", "dv", "di", "fc")}
    outs_s = {k: [] for k in outs_p}
    for i in range(depth):
        if i % 2 == 0:
            e = i // 2
            wts = (norm_mix_g[i], a_w_in[e], a_conv_w[e], a_conv_b[e], a_dt_bias[e], a_A_log[e], a_D[e],
                   a_norm_g[e], b_qn_g[e], b_kn_g[e], b_sinks[e], e_w_out[e])
            xp, (c1, s1, k1, v1) = _mixer_even(xp, pos_p, True, None, None, None, None, *wts)
            xs, (c2, s2, k2, v2) = _mixer_even(xs, pos_s, False, state_ssm_conv[e], state_ssm[e],
                                               cache_swa_k[e], cache_swa_v[e], *wts)
            for d, vals in ((outs_p, (c1, s1, k1, v1)), (outs_s, (c2, s2, k2, v2))):
                for key, val in zip(("cnv", "ssm", "swk", "swv"), vals):
                    d[key].append(val)
        else:
            o = i // 2
            lam_init = 0.8 - 0.6 * math.exp(-0.3 * i)
            wts = (norm_mix_g[i], m_w_in[o], c_qn_g[o], c_kn_g[o], c_lam[o], c_subln_g[o], d_qn_g[o], d_kn_g[o],
                   m_w_out[o], lam_init, odd_sizes)
            xp, cp = _mixer_odd(xp, pos_p, True, None, *wts, min(256, seq // 4))
            xs, cs = _mixer_odd(xs, pos_s, False,
                                (cache_c_k, cache_c_v, cache_d_k, cache_d_v, cache_d_idx, page_table, o),
                                *wts, min(256, (past + 1) // 4))
            for d, vals in ((outs_p, cp), (outs_s, cs)):
                for key, val in zip(("ck", "cv", "dk", "dv", "di"), vals):
                    d[key].append(val)
        fw = (norm_ffn_g[i], _bf(ffn_w_gate[i]), _bf(ffn_w_up[i]), ffn_conv_w[i], ffn_conv_b[i], _bf(ffn_w_down[i]))
        zrow = jnp.zeros((1, d_ff), F32)
        xp, gp = _ffn(xp, zrow, zrow, *fw, seq_mode=True)
        outs_p["fc"].append(gp[gp.shape[0] - (ffn_conv_w.shape[1] - 1):][None])
        st = state_ffn_conv[i]
        xs, gs = _ffn(xs, st[:, 0, :], st[:, 1, :], *fw, seq_mode=False)
        outs_s["fc"].append(jnp.stack([st[:, 1, :], gs], axis=1))
    order = ("ssm", "cnv", "swk", "swv", "ck", "cv", "dk", "dv", "di", "fc")
    return ((xp.reshape(1, seq, d_model), xs.reshape(nb, 1, d_model))
            + tuple(jnp.stack(outs_p[k]) for k in order) + tuple(jnp.stack(outs_s[k]) for k in order))
```

```python
import functools
import math

import jax
import jax.numpy as jnp
import numpy as np
from jax import lax
from jax.experimental import pallas as pl
from jax.experimental.pallas import tpu as pltpu

F32 = jnp.float32
BF16 = jnp.bfloat16
I32 = jnp.int32

EPS = 1e-6
ROPE_THETA = 10000.0
HEAD_DIM = 64
LANES = 128
SUBLANES = 8
VMEM_LIMIT = 56 * 1024 * 1024
WINDOW = 128
SSD_CHUNK = 128
PAGE = 128
A_GROUPS = 2
A_HEAD_DIM = 64
A_STATE = 128
INT_MIN = -2147483648


def _cparams(*sem):
    return pltpu.CompilerParams(dimension_semantics=sem, vmem_limit_bytes=VMEM_LIMIT)


def _pick_tile(n, cap):
    best = LANES
    for m in range(1, n // LANES + 1):
        if n % (m * LANES) == 0 and m * LANES <= cap:
            best = m * LANES
    return best


def _row_tile(m, cap):
    t = min(m, cap)
    while m % t:
        t //= 2
    return t


def _split3(x):
    h = x.astype(BF16)
    r = x - h.astype(F32)
    m = r.astype(BF16)
    lo = (r - m.astype(F32)).astype(BF16)
    return h, m, lo


def _dot(a, b):
    return jnp.dot(a, b, preferred_element_type=F32)


def _dot_nt(a, b):
    return lax.dot_general(a, b, (((1,), (1,)), ((), ())), preferred_element_type=F32)


def _dot3(x, w01):
    h, m, lo = _split3(x)
    return _dot(h, w01) + _dot(m, w01) + _dot(lo, w01)


def _dot3_left(w01, x):
    h, m, lo = _split3(x)
    return _dot(w01, h) + _dot(w01, m) + _dot(w01, lo)


def _silu(x):
    return x * (1.0 / (1.0 + jnp.exp(-x)))


def _softplus(x):
    return jnp.maximum(x, 0.0) + jnp.log(1.0 + jnp.exp(-jnp.abs(x)))


def _mm_kernel(*refs, n_lhs, has_norm, has_res):
    xs = refs[:n_lhs]
    pos = n_lhs
    g_ref = None
    if has_norm:
        g_ref = refs[pos]
        pos += 1
    ws = refs[pos:pos + n_lhs]
    pos += n_lhs
    res_ref = None
    if has_res:
        res_ref = refs[pos]
        pos += 1
    o_ref = refs[pos]
    xb = refs[pos + 1:pos + 1 + n_lhs]

    @pl.when(pl.program_id(1) == 0)
    def _():
        for k in range(n_lhs):
            x = xs[k][...]
            if has_norm and k == 0:
                x = x * lax.rsqrt(jnp.mean(x * x, axis=-1, keepdims=True) + EPS) * g_ref[...]
            xb[k][...] = x.astype(BF16)

    acc = _dot(xb[0][...], ws[0][...])
    for k in range(1, n_lhs):
        acc = acc + _dot(xb[k][...], ws[k][...])
    if has_res:
        acc = acc + res_ref[...]
    o_ref[...] = acc


def _mm(xs, ws, *, norm_g=None, res=None, tm_cap=512, tn_cap=1280, name="mm"):
    m = xs[0].shape[0]
    n = ws[0].shape[1]
    tm = _row_tile(m, tm_cap)
    tn = _pick_tile(n, tn_cap)
    n_lhs = len(xs)
    in_specs = [pl.BlockSpec((tm, x.shape[1]), lambda i, j: (i, 0)) for x in xs]
    args = list(xs)
    if norm_g is not None:
        in_specs.append(pl.BlockSpec((1, xs[0].shape[1]), lambda i, j: (0, 0)))
        args.append(norm_g.reshape(1, -1))
    in_specs += [pl.BlockSpec((w.shape[0], tn), lambda i, j: (0, j)) for w in ws]
    args += list(ws)
    if res is not None:
        in_specs.append(pl.BlockSpec((tm, tn), lambda i, j: (i, j)))
        args.append(res)
    return pl.pallas_call(
        functools.partial(_mm_kernel, n_lhs=n_lhs, has_norm=norm_g is not None, has_res=res is not None),
        out_shape=jax.ShapeDtypeStruct((m, n), F32),
        grid=(m // tm, n // tn),
        in_specs=in_specs,
        out_specs=pl.BlockSpec((tm, tn), lambda i, j: (i, j)),
        scratch_shapes=[pltpu.VMEM((tm, x.shape[1]), BF16) for x in xs],
        compiler_params=_cparams("parallel", "arbitrary"),
        name=name,
    )(*args)


def _ffn_kernel(x_ref, halo_ref, g_ref, wg_ref, wu_ref, cw_ref, cb_ref, wd_ref, p0_ref, p1_ref,
                o_ref, gout_ref, xb_ref, hb_ref, gs_ref, acc_ref, *, seq_mode, tm):
    i = pl.program_id(0)
    j = pl.program_id(1)
    nj = pl.num_programs(1)

    def norm(x):
        return (x * lax.rsqrt(jnp.mean(x * x, axis=-1, keepdims=True) + EPS) * g_ref[...]).astype(BF16)

    @pl.when(j == 0)
    def _():
        xb_ref[...] = norm(x_ref[...])
        if seq_mode:
            hb_ref[...] = norm(halo_ref[...])
        acc_ref[...] = jnp.zeros_like(acc_ref)

    g = _dot(xb_ref[...], wg_ref[...])
    u = _dot(xb_ref[...], wu_ref[...])
    cw = cw_ref[...]
    if seq_mode:
        carried = jnp.concatenate([jnp.zeros((SUBLANES - 2, g.shape[1]), F32), p0_ref[...], p1_ref[...]], axis=0)
        prev = jnp.where(i == 0, carried, _dot(hb_ref[...], wg_ref[...]))
        gs_ref[0:SUBLANES, :] = prev
        gs_ref[SUBLANES:, :] = g
        g1 = gs_ref[pl.ds(SUBLANES - 1, tm), :]
        g2 = gs_ref[pl.ds(SUBLANES - 2, tm), :]
        gout_ref[...] = g[tm - SUBLANES:, :]
    else:
        g1 = p1_ref[...]
        g2 = p0_ref[...]
        gout_ref[...] = g
    c = cw[0:1, :] * g2 + cw[1:2, :] * g1 + cw[2:3, :] * g + cb_ref[...]
    act = (_silu(c) * u).astype(BF16)
    acc_ref[...] += _dot(act, wd_ref[...])

    @pl.when(j == nj - 1)
    def _():
        o_ref[...] = x_ref[...] + acc_ref[...]


def _ffn(x, prev0, prev1, norm_g, wg, wu, conv_w, conv_b, wd, *, seq_mode):
    m, d = x.shape
    f = wg.shape[1]
    tm = _row_tile(m, 1024 if seq_mode else 128)
    tn = _pick_tile(f, 1408)
    ni, nj = m // tm, f // tn
    hb = tm // SUBLANES
    if seq_mode:
        prev_spec = pl.BlockSpec((1, tn), lambda i, j: (0, j))
        gout_rows, gout_shape = SUBLANES, (ni * SUBLANES, f)
    else:
        prev_spec = pl.BlockSpec((tm, tn), lambda i, j: (i, j))
        gout_rows, gout_shape = tm, (m, f)
    out, gout = pl.pallas_call(
        functools.partial(_ffn_kernel, seq_mode=seq_mode, tm=tm),
        out_shape=(jax.ShapeDtypeStruct((m, d), F32), jax.ShapeDtypeStruct(gout_shape, F32)),
        grid=(ni, nj),
        in_specs=[
            pl.BlockSpec((tm, d), lambda i, j: (i, 0)),
            pl.BlockSpec((SUBLANES, d), lambda i, j: (jnp.maximum(i * hb - 1, 0), 0)),
            pl.BlockSpec((1, d), lambda i, j: (0, 0)),
            pl.BlockSpec((d, tn), lambda i, j: (0, j)),
            pl.BlockSpec((d, tn), lambda i, j: (0, j)),
            pl.BlockSpec((conv_w.shape[0], tn), lambda i, j: (0, j)),
            pl.BlockSpec((1, tn), lambda i, j: (0, j)),
            pl.BlockSpec((tn, d), lambda i, j: (j, 0)),
            prev_spec, prev_spec,
        ],
        out_specs=(pl.BlockSpec((tm, d), lambda i, j: (i, 0)),
                   pl.BlockSpec((gout_rows, tn), lambda i, j: (i, j))),
        scratch_shapes=[pltpu.VMEM((tm, d), BF16), pltpu.VMEM((SUBLANES, d), BF16),
                        pltpu.VMEM((tm + SUBLANES, tn), F32), pltpu.VMEM((tm, d), F32)],
        compiler_params=_cparams("parallel", "arbitrary"),
        name="conv_ffn",
    )(x, x, norm_g.reshape(1, -1), wg, wu, conv_w, conv_b.reshape(1, -1), wd, prev0, prev1)
    return out, gout


def _conv_seq_kernel(x_ref, halo_ref, st_ref, w_ref, b_ref, o_ref, xs_ref, *, tm, kw):
    i = pl.program_id(0)
    prev = jnp.where(i == 0, st_ref[...], halo_ref[...])
    xs_ref[0:SUBLANES, :] = prev
    xs_ref[SUBLANES:, :] = x_ref[...]
    w = w_ref[...]
    acc = b_ref[...] + w[kw - 1:kw, :] * x_ref[...]
    for t in range(1, kw):
        acc = acc + w[kw - 1 - t:kw - t, :] * xs_ref[pl.ds(SUBLANES - t, tm), :]
    o_ref[...] = _silu(acc)


def _conv_seq(src, col0, width, state8, w, b):
    m = src.shape[0]
    kw = w.shape[0]
    tm = _row_tile(m, 512)
    tn = _pick_tile(math.gcd(width, col0) if col0 else width, 512)
    cb = col0 // tn
    hb = tm // SUBLANES
    return pl.pallas_call(
        functools.partial(_conv_seq_kernel, tm=tm, kw=kw),
        out_shape=jax.ShapeDtypeStruct((m, width), F32),
        grid=(m // tm, width // tn),
        in_specs=[
            pl.BlockSpec((tm, tn), lambda i, j: (i, cb + j)),
            pl.BlockSpec((SUBLANES, tn), lambda i, j: (jnp.maximum(i * hb - 1, 0), cb + j)),
            pl.BlockSpec((SUBLANES, tn), lambda i, j: (0, j)),
            pl.BlockSpec((kw, tn), lambda i, j: (0, j)),
            pl.BlockSpec((1, tn), lambda i, j: (0, j)),
        ],
        out_specs=pl.BlockSpec((tm, tn), lambda i, j: (i, j)),
        scratch_shapes=[pltpu.VMEM((tm + SUBLANES, tn), F32)],
        compiler_params=_cparams("parallel", "parallel"),
        name="ssm_conv_seq",
    )(src, src, state8, w, b.reshape(1, -1))


def _conv_step_kernel(x_ref, s_ref, w_ref, b_ref, o_ref, *, kw):
    w = w_ref[...]
    acc = b_ref[...] + w[kw - 1:kw, :] * x_ref[...]
    for t in range(kw - 1):
        acc = acc + w[t:t + 1, :] * s_ref[t]
    o_ref[...] = _silu(acc)


def _conv_step(src, col0, width, state, w, b):
    m = src.shape[0]
    kw = w.shape[0]
    tn = _pick_tile(math.gcd(width, col0) if col0 else width, 512)
    cb = col0 // tn
    return pl.pallas_call(
        functools.partial(_conv_step_kernel, kw=kw),
        out_shape=jax.ShapeDtypeStruct((m, width), F32),
        grid=(width // tn,),
        in_specs=[
            pl.BlockSpec((m, tn), lambda j: (0, cb + j)),
            pl.BlockSpec((kw - 1, m, tn), lambda j: (0, 0, j)),
            pl.BlockSpec((kw, tn), lambda j: (0, j)),
            pl.BlockSpec((1, tn), lambda j: (0, j)),
        ],
        out_specs=pl.BlockSpec((m, tn), lambda j: (0, j)),
        compiler_params=_cparams("parallel"),
        name="ssm_conv_step",
    )(src, state, w, b.reshape(1, -1))


def _seg_ones(seg):
    r = lax.broadcasted_iota(I32, (LANES, LANES), 0) // seg
    c = lax.broadcasted_iota(I32, (LANES, LANES), 1) // seg
    return (r == c).astype(BF16)


def _rope128(x, cos, sin_signed):
    lane = lax.broadcasted_iota(I32, x.shape, 1)
    rot = jnp.where(lane % HEAD_DIM < HEAD_DIM // 2,
                    pltpu.roll(x, LANES - HEAD_DIM // 2, 1), pltpu.roll(x, HEAD_DIM // 2, 1))
    return x * cos + rot * sin_signed


def _norm_rope_kernel(x_ref, g_ref, cos_ref, sin_ref, o_ref, ob_ref, *, do_norm, width):
    cos = cos_ref[...]
    sin = sin_ref[...]
    ones = _seg_ones(HEAD_DIM)
    for c in range(width // LANES):
        x = x_ref[:, c * LANES:(c + 1) * LANES]
        if do_norm:
            ms = _dot3(x * x, ones) * (1.0 / HEAD_DIM)
            x = x * lax.rsqrt(ms + EPS) * g_ref[...]
        y = _rope128(x, cos, sin)
        o_ref[:, c * LANES:(c + 1) * LANES] = y
        ob_ref[:, c * LANES:(c + 1) * LANES] = y.astype(BF16)


def _norm_rope(src, col0, width, gain, cos, sin, *, name):
    m = src.shape[0]
    tm = _row_tile(m, 512)
    assert col0 % width == 0
    cb = col0 // width
    g = jnp.ones((1, LANES), F32) if gain is None else jnp.tile(gain.reshape(1, HEAD_DIM), (1, LANES // HEAD_DIM))
    return pl.pallas_call(
        functools.partial(_norm_rope_kernel, do_norm=gain is not None, width=width),
        out_shape=(jax.ShapeDtypeStruct((m, width), F32), jax.ShapeDtypeStruct((m, width), BF16)),
        grid=(m // tm,),
        in_specs=[
            pl.BlockSpec((tm, width), lambda i: (i, cb)),
            pl.BlockSpec((1, LANES), lambda i: (0, 0)),
            pl.BlockSpec((tm, LANES), lambda i: (i, 0)),
            pl.BlockSpec((tm, LANES), lambda i: (i, 0)),
        ],
        out_specs=(pl.BlockSpec((tm, width), lambda i: (i, 0)), pl.BlockSpec((tm, width), lambda i: (i, 0))),
        compiler_params=_cparams("parallel"),
        name=name,
    )(src, g, cos, sin)


def _rope_tables(pos):
    half = HEAD_DIM // 2
    inv = ROPE_THETA ** (-jnp.arange(half, dtype=F32) / half)
    ang = pos.astype(F32)[:, None] * inv[None, :]
    cos, sin = jnp.cos(ang), jnp.sin(ang)
    cos128 = jnp.tile(jnp.concatenate([cos, cos], axis=1), (1, LANES // HEAD_DIM))
    sin128 = jnp.tile(jnp.concatenate([-sin, sin], axis=1), (1, LANES // HEAD_DIM))
    return cos128, sin128


def _head_expand(n_heads_pad, width):
    r = lax.broadcasted_iota(I32, (n_heads_pad, width), 0)
    c = lax.broadcasted_iota(I32, (n_heads_pad, width), 1) // A_HEAD_DIM
    return (r == c).astype(BF16)


def _gated_norm(y, xs, z, dskip, gain):
    yz = (y + xs * dskip) * _silu(z)
    gw = yz.shape[1] // A_GROUPS
    parts = []
    for g in range(A_GROUPS):
        p = yz[:, g * gw:(g + 1) * gw]
        parts.append(p * lax.rsqrt(jnp.mean(p * p, axis=-1, keepdims=True) + EPS))
    return jnp.concatenate(parts, axis=1) * gain


def _ssd_seq_kernel(xs_ref, b_ref, c_ref, dt_ref, z_ref, dtb_ref, alog_ref, dskip_ref, gain_ref, s0_ref,
                    ya_ref, sout_ref, st_ref, y_ref, *, n_heads):
    ci = pl.program_id(0)
    q = SSD_CHUNK
    d_inner = xs_ref.shape[1]
    hpg = n_heads // A_GROUPS
    gw = d_inner // A_GROUPS

    @pl.when(ci == 0)
    def _():
        st_ref[...] = s0_ref[...]

    xs = xs_ref[...]
    dt = _softplus(dt_ref[...] + dtb_ref[...])
    a = dt * (-jnp.exp(alog_ref[...]))
    row = lax.broadcasted_iota(I32, (q, q), 0)
    col = lax.broadcasted_iota(I32, (q, q), 1)
    causal = col <= row
    tri = causal.astype(BF16)
    tri_t = (row <= col).astype(BF16)
    acs = _dot3_left(tri, a)
    acs_t = _dot3(a.T, tri_t)
    expand = _head_expand(LANES, d_inner)
    acs_x = _dot3(acs, expand)
    dt_x = _dot3(dt, expand)
    e_acs = jnp.exp(acs_x)
    last = acs_x[q - 1:q, :]
    decay_s = jnp.exp(last - acs_x)
    xdt = xs * dt_x
    xdt_b = xdt.astype(BF16)
    xdec_b = (xdt * decay_s).astype(BF16)
    chunk_decay = e_acs[q - 1:q, :]

    for g in range(A_GROUPS):
        bg = b_ref[:, g * A_STATE:(g + 1) * A_STATE]
        cg = c_ref[:, g * A_STATE:(g + 1) * A_STATE].astype(BF16)
        cb = _dot_nt(cg, bg.astype(BF16))
        st_g = st_ref[g]
        y_off = _dot(cg, st_g.astype(BF16)) * e_acs[:, g * gw:(g + 1) * gw]
        for hh in range(hpg):
            h = g * hpg + hh
            diff = acs[:, h:h + 1] - acs_t[h:h + 1, :]
            m = (cb * jnp.where(causal, jnp.exp(diff), 0.0)).astype(BF16)
            lo = h * A_HEAD_DIM
            y_ref[:, lo:lo + A_HEAD_DIM] = (_dot(m, xdt_b[:, lo:lo + A_HEAD_DIM])
                                            + y_off[:, hh * A_HEAD_DIM:(hh + 1) * A_HEAD_DIM])
        st_ref[g] = st_g * chunk_decay[:, g * gw:(g + 1) * gw] + _dot(bg.T.astype(BF16), xdec_b[:, g * gw:(g + 1) * gw])

    ya_ref[...] = _gated_norm(y_ref[...], xs, z_ref[...], dskip_ref[...], gain_ref[...])

    @pl.when(ci == pl.num_programs(0) - 1)
    def _():
        sout_ref[...] = st_ref[...]


def _ssd_seq(xbc, proj, z_col, dt_col, dt_bias, a_log, d_skip, gain, s0_t, n_heads):
    t = xbc.shape[0]
    q = SSD_CHUNK
    d_inner = n_heads * A_HEAD_DIM
    gn = A_GROUPS * A_STATE
    bcol = d_inner // gn
    pad = lambda v: jnp.pad(v.reshape(1, -1), ((0, 0), (0, LANES - v.shape[-1])))
    dskip_x = jnp.repeat(d_skip, A_HEAD_DIM).reshape(1, d_inner)
    const = lambda shape: pl.BlockSpec(shape, lambda c: (0,) * len(shape))
    return pl.pallas_call(
        functools.partial(_ssd_seq_kernel, n_heads=n_heads),
        out_shape=(jax.ShapeDtypeStruct((t, d_inner), F32), jax.ShapeDtypeStruct(s0_t.shape, F32)),
        grid=(t // q,),
        in_specs=[
            pl.BlockSpec((q, d_inner), lambda c: (c, 0)),
            pl.BlockSpec((q, gn), lambda c: (c, bcol)),
            pl.BlockSpec((q, gn), lambda c: (c, bcol + 1)),
            pl.BlockSpec((q, LANES), lambda c: (c, dt_col // LANES)),
            pl.BlockSpec((q, d_inner), lambda c: (c, z_col // d_inner)),
            const((1, LANES)), const((1, LANES)), const((1, d_inner)), const((1, d_inner)),
            const(s0_t.shape),
        ],
        out_specs=(pl.BlockSpec((q, d_inner), lambda c: (c, 0)), const(s0_t.shape)),
        scratch_shapes=[pltpu.VMEM(s0_t.shape, F32), pltpu.VMEM((q, d_inner), F32)],
        compiler_params=_cparams("arbitrary"),
        name="ssd_seq",
    )(xbc, xbc, xbc, proj, proj, pad(dt_bias), pad(a_log), dskip_x, gain.reshape(1, -1), s0_t)


def _ssd_step_kernel(s_ref, xdt_t_ref, dec_t_ref, b_ref, c_ref, sout_ref, y_ref, *, n_heads):
    b = pl.program_id(0)
    nb = xdt_t_ref.shape[1]
    hp = n_heads * A_HEAD_DIM
    gw = hp // A_GROUPS
    lane = lax.broadcasted_iota(I32, (hp, nb), 1)
    dec = jnp.sum(jnp.where(lane == b, dec_t_ref[...], 0.0), axis=1, keepdims=True)
    rows = lax.broadcasted_iota(I32, (nb, A_STATE), 0)
    s = s_ref[0].reshape(hp, A_STATE)
    xdt_t = xdt_t_ref[...].astype(BF16)
    outs = []
    for g in range(A_GROUPS):
        brow = b_ref[0, :, g * A_STATE:(g + 1) * A_STATE]
        zb = jnp.where(rows == b, jnp.broadcast_to(brow, (nb, A_STATE)), 0.0).astype(BF16)
        upd = _dot(xdt_t[g * gw:(g + 1) * gw, :], zb)
        sn = s[g * gw:(g + 1) * gw, :] * dec[g * gw:(g + 1) * gw, :] + upd
        sout_ref[0, g * (n_heads // A_GROUPS):(g + 1) * (n_heads // A_GROUPS)] = sn.reshape(
            n_heads // A_GROUPS, A_HEAD_DIM, A_STATE)
        crow = c_ref[0, :, g * A_STATE:(g + 1) * A_STATE]
        c8 = jnp.broadcast_to(crow, (SUBLANES, A_STATE)).astype(BF16)
        outs.append(_dot_nt(c8, sn.astype(BF16))[0:1, :])
    y_ref[0] = jnp.concatenate(outs, axis=1)


def _ssd_step(state, xdt_t, dec_t, bmat, cmat):
    nb, n_heads, p, n = state.shape
    hp = n_heads * p
    new_state, y = pl.pallas_call(
        functools.partial(_ssd_step_kernel, n_heads=n_heads),
        out_shape=(jax.ShapeDtypeStruct(state.shape, F32), jax.ShapeDtypeStruct((nb, 1, hp), F32)),
        grid=(nb,),
        in_specs=[
            pl.BlockSpec((1, n_heads, p, n), lambda b: (b, 0, 0, 0)),
            pl.BlockSpec((hp, nb), lambda b: (0, 0)),
            pl.BlockSpec((hp, nb), lambda b: (0, 0)),
            pl.BlockSpec((1, 1, bmat.shape[1]), lambda b: (b, 0, 0)),
            pl.BlockSpec((1, 1, cmat.shape[1]), lambda b: (b, 0, 0)),
        ],
        out_specs=(pl.BlockSpec((1, n_heads, p, n), lambda b: (b, 0, 0, 0)),
                   pl.BlockSpec((1, 1, hp), lambda b: (b, 0, 0))),
        compiler_params=_cparams("arbitrary"),
        name="ssd_step",
    )(state, xdt_t, dec_t, bmat[:, None, :], cmat[:, None, :])
    return new_state, y.reshape(nb, hp)


def _ssd_step_pre_kernel(xs_ref, dt_ref, dtb_ref, alog_ref, xdt_ref, dec_ref):
    dt = _softplus(dt_ref[...] + dtb_ref[...])
    expand = _head_expand(LANES, xs_ref.shape[1])
    xdt_ref[...] = xs_ref[...] * _dot3(dt, expand)
    dec_ref[...] = jnp.exp(_dot3(dt * (-jnp.exp(alog_ref[...])), expand))


def _ssd_step_pre(xbc, proj, dt_col, dt_bias, a_log, d_inner):
    m = xbc.shape[0]
    pad = lambda v: jnp.pad(v.reshape(1, -1), ((0, 0), (0, LANES - v.shape[-1])))
    return pl.pallas_call(
        _ssd_step_pre_kernel,
        out_shape=(jax.ShapeDtypeStruct((m, d_inner), F32), jax.ShapeDtypeStruct((m, d_inner), F32)),
        grid=(1,),
        in_specs=[pl.BlockSpec((m, d_inner), lambda i: (0, 0)),
                  pl.BlockSpec((m, LANES), lambda i: (0, dt_col // LANES)),
                  pl.BlockSpec((1, LANES), lambda i: (0, 0)), pl.BlockSpec((1, LANES), lambda i: (0, 0))],
        out_specs=(pl.BlockSpec((m, d_inner), lambda i: (0, 0)), pl.BlockSpec((m, d_inner), lambda i: (0, 0))),
        compiler_params=_cparams("arbitrary"),
        name="ssd_step_pre",
    )(xbc, proj, pad(dt_bias), pad(a_log))


def _gated_norm_kernel(y_ref, xs_ref, z_ref, dskip_ref, gain_ref, o_ref):
    o_ref[...] = _gated_norm(y_ref[...], xs_ref[...], z_ref[...], dskip_ref[...], gain_ref[...])


def _gated_norm_call(y, xbc, proj, z_col, d_skip, gain):
    m, d_inner = y.shape
    dskip_x = jnp.repeat(d_skip, A_HEAD_DIM).reshape(1, d_inner)
    return pl.pallas_call(
        _gated_norm_kernel,
        out_shape=jax.ShapeDtypeStruct((m, d_inner), F32),
        grid=(1,),
        in_specs=[pl.BlockSpec((m, d_inner), lambda i: (0, 0)),
                  pl.BlockSpec((m, d_inner), lambda i: (0, 0)),
                  pl.BlockSpec((m, d_inner), lambda i: (0, z_col // d_inner)),
                  pl.BlockSpec((1, d_inner), lambda i: (0, 0)), pl.BlockSpec((1, d_inner), lambda i: (0, 0))],
        out_specs=pl.BlockSpec((m, d_inner), lambda i: (0, 0)),
        compiler_params=_cparams("arbitrary"),
        name="gated_norm",
    )(y, xbc, proj, dskip_x, gain.reshape(1, -1))


def _swa_seq_kernel(sink_ref, q_ref, kc_ref, kp_ref, vc_ref, vp_ref, o_ref, *, n_heads, n_kv):
    i = pl.program_id(0)
    w = WINDOW
    grp = n_heads // n_kv
    r = lax.broadcasted_iota(I32, (grp * w, 2 * w), 0) % w
    c = lax.broadcasted_iota(I32, (grp * w, 2 * w), 1)
    ok = (c > r) & (c <= r + w) & ((i > 0) | (c >= w))
    hrow = lax.broadcasted_iota(I32, (grp * w, 1), 0) // w
    scale = HEAD_DIM ** -0.5
    for kv in range(n_kv):
        sl = slice(kv * HEAD_DIM, (kv + 1) * HEAD_DIM)
        kcat = jnp.concatenate([kp_ref[:, sl], kc_ref[:, sl]], axis=0)
        vcat = jnp.concatenate([vp_ref[:, sl], vc_ref[:, sl]], axis=0)
        q4 = jnp.concatenate([q_ref[:, (kv * grp + j) * HEAD_DIM:(kv * grp + j + 1) * HEAD_DIM]
                              for j in range(grp)], axis=0)
        sink = jnp.zeros((grp * w, 1), F32)
        for j in range(grp):
            sink = jnp.where(hrow == j, sink_ref[kv * grp + j], sink)
        s = jnp.where(ok, _dot_nt(q4, kcat) * scale, -jnp.inf)
        m = jnp.maximum(jnp.max(s, axis=1, keepdims=True), sink)
        p = jnp.exp(s - m)
        denom = jnp.sum(p, axis=1, keepdims=True) + jnp.exp(sink - m)
        o = _dot(p.astype(BF16), vcat) / denom
        for j in range(grp):
            h = kv * grp + j
            o_ref[:, h * HEAD_DIM:(h + 1) * HEAD_DIM] = o[j * w:(j + 1) * w, :]


def _swa_seq(q_b, k_b, v_b, sinks, n_heads, n_kv):
    t = q_b.shape[0]
    w = WINDOW
    kvw = n_kv * HEAD_DIM
    cur = lambda i: (i, 0)
    prv = lambda i: (jnp.maximum(i - 1, 0), 0)
    return pl.pallas_call(
        functools.partial(_swa_seq_kernel, n_heads=n_heads, n_kv=n_kv),
        out_shape=jax.ShapeDtypeStruct((t, n_heads * HEAD_DIM), F32),
        grid=(t // w,),
        in_specs=[pl.BlockSpec(memory_space=pltpu.SMEM),
                  pl.BlockSpec((w, n_heads * HEAD_DIM), cur),
                  pl.BlockSpec((w, kvw), cur), pl.BlockSpec((w, kvw), prv),
                  pl.BlockSpec((w, kvw), cur), pl.BlockSpec((w, kvw), prv)],
        out_specs=pl.BlockSpec((w, n_heads * HEAD_DIM), cur),
        compiler_params=_cparams("parallel"),
        name="swa_seq",
    )(sinks, q_b, k_b, k_b, v_b, v_b)


def _swa_step_kernel(sink_ref, qm_ref, knew_ref, vnew_ref, kc_ref, vc_ref, o_ref, ko_ref, vo_ref,
                     *, n_heads, n_kv, bs):
    grp = n_heads // n_kv
    wb = kc_ref.shape[1]
    scale = HEAD_DIM ** -0.5
    hrow = lax.broadcasted_iota(I32, (n_heads, 1), 0)
    sink = jnp.zeros((n_heads, 1), F32)
    for h in range(n_heads):
        sink = jnp.where(hrow == h, sink_ref[h], sink)
    col = lax.broadcasted_iota(I32, (n_heads, wb), 1)
    ok = col > wb - WINDOW
    for bi in range(bs):
        qm = qm_ref[bi]
        kc = kc_ref[bi]
        vc = vc_ref[bi]
        knew = knew_ref[bi]
        vnew = vnew_ref[bi]
        s = jnp.where(ok, _dot_nt(qm.astype(BF16), kc.astype(BF16)) * scale, -jnp.inf)
        s_new = jnp.sum(qm * knew, axis=1, keepdims=True) * scale
        m = jnp.maximum(jnp.maximum(jnp.max(s, axis=1, keepdims=True), s_new), sink)
        p = jnp.exp(s - m)
        p_new = jnp.exp(s_new - m)
        denom = jnp.sum(p, axis=1, keepdims=True) + p_new + jnp.exp(sink - m)
        o = (_dot(p.astype(BF16), vc.astype(BF16)) + p_new * vnew) / denom
        osel = o[:, 0:HEAD_DIM]
        for kv in range(1, n_kv):
            osel = jnp.where(hrow // grp == kv, o[:, kv * HEAD_DIM:(kv + 1) * HEAD_DIM], osel)
        o_ref[bi] = osel
        ko_ref[bi, 0:wb - 1, :] = kc_ref[bi, 1:wb, :]
        ko_ref[bi, wb - 1:wb, :] = knew
        vo_ref[bi, 0:wb - 1, :] = vc_ref[bi, 1:wb, :]
        vo_ref[bi, wb - 1:wb, :] = vnew


def _swa_step(qm, knew, vnew, cache_k, cache_v, sinks, n_heads, n_kv):
    nb, wb, kvw = cache_k.shape
    bs = 8 if nb % 8 == 0 else 1
    blk = lambda shape: pl.BlockSpec((bs,) + shape, lambda b: (b,) + (0,) * len(shape))
    return pl.pallas_call(
        functools.partial(_swa_step_kernel, n_heads=n_heads, n_kv=n_kv, bs=bs),
        out_shape=(jax.ShapeDtypeStruct((nb, n_heads, HEAD_DIM), F32),
                   jax.ShapeDtypeStruct(cache_k.shape, F32), jax.ShapeDtypeStruct(cache_v.shape, F32)),
        grid=(nb // bs,),
        in_specs=[pl.BlockSpec(memory_space=pltpu.SMEM), blk((n_heads, kvw)), blk((1, kvw)), blk((1, kvw)),
                  blk((wb, kvw)), blk((wb, kvw))],
        out_specs=(blk((n_heads, HEAD_DIM)), blk((wb, kvw)), blk((wb, kvw))),
        compiler_params=_cparams("parallel"),
        name="swa_step",
    )(sinks, qm, knew, vnew, cache_k, cache_v)


def _lambda(lam_ref, lam_init):
    lp = lam_ref[...]
    return (jnp.exp(jnp.sum(lp[0:1, :] * lp[1:2, :], axis=1, keepdims=True))
            - jnp.exp(jnp.sum(lp[2:3, :] * lp[3:4, :], axis=1, keepdims=True)) + lam_init)


def _subln(o, subg, lam_init):
    return o * lax.rsqrt(jnp.mean(o * o, axis=-1, keepdims=True) + EPS) * subg * (1.0 - lam_init)


def _diff_seq_kernel(q_ref, k_ref, v_ref, lam_ref, subg_ref, o_ref, m_ref, l_ref, acc_ref,
                     *, n_heads, n_kv, lam_init, tq):
    i = pl.program_id(0)
    grp = n_heads // n_kv
    hd = HEAD_DIM
    vd = 2 * hd
    scale = hd ** -0.5
    m_ref[...] = jnp.full(m_ref.shape, -jnp.inf, F32)
    l_ref[...] = jnp.zeros_like(l_ref)
    acc_ref[...] = jnp.zeros_like(acc_ref)
    row = lax.broadcasted_iota(I32, (grp * tq, tq), 0) % tq
    col = lax.broadcasted_iota(I32, (grp * tq, tq), 1)
    causal = col <= row

    def step(j, masked):
        kb = k_ref[j]
        vb = v_ref[j]
        for kv in range(n_kv):
            for c in range(2):
                idx = kv * 2 + c
                q2 = jnp.concatenate(
                    [q_ref[:, ((kv * grp + g) * 2 + c) * hd:((kv * grp + g) * 2 + c + 1) * hd] for g in range(grp)],
                    axis=0)
                s = _dot_nt(q2, kb[:, idx * hd:(idx + 1) * hd]) * scale
                if masked:
                    s = jnp.where(causal, s, -jnp.inf)
                m_old = m_ref[idx]
                m_new = jnp.maximum(m_old, jnp.max(s, axis=1, keepdims=True))
                alpha = jnp.exp(m_old - m_new)
                p = jnp.exp(s - m_new)
                l_ref[idx] = alpha * l_ref[idx] + jnp.sum(p, axis=1, keepdims=True)
                acc_ref[idx] = alpha * acc_ref[idx] + _dot(p.astype(BF16), vb[:, kv * vd:(kv + 1) * vd])
                m_ref[idx] = m_new

    def body(j, carry):
        step(j, False)
        return carry

    lax.fori_loop(0, i, body, 0)
    step(i, True)
    lam = _lambda(lam_ref, lam_init)
    for kv in range(n_kv):
        o0 = acc_ref[kv * 2] / l_ref[kv * 2]
        o1 = acc_ref[kv * 2 + 1] / l_ref[kv * 2 + 1]
        o = _subln(o0 - lam * o1, subg_ref[...], lam_init)
        for g in range(grp):
            h = kv * grp + g
            o_ref[:, h * vd:(h + 1) * vd] = o[g * tq:(g + 1) * tq, :]


def _resident(shape):
    return pl.BlockSpec(shape, lambda *_: (0,) * len(shape), pipeline_mode=pl.Buffered(1))


def _diff_seq(q_b, k_b, v_b, c_lam, sub_g, lam_init, n_heads, n_kv):
    t = q_b.shape[0]
    tq = min(256, t)
    nb = t // tq
    kw = k_b.shape[1]
    vd = 2 * HEAD_DIM
    grp = n_heads // n_kv
    return pl.pallas_call(
        functools.partial(_diff_seq_kernel, n_heads=n_heads, n_kv=n_kv, lam_init=lam_init, tq=tq),
        out_shape=jax.ShapeDtypeStruct((t, n_heads * vd), F32),
        grid=(nb,),
        in_specs=[pl.BlockSpec((tq, q_b.shape[1]), lambda i: (i, 0)),
                  _resident((nb, tq, kw)), _resident((nb, tq, v_b.shape[1])),
                  pl.BlockSpec(c_lam.shape, lambda i: (0, 0)),
                  pl.BlockSpec((1, vd), lambda i: (0, 0))],
        out_specs=pl.BlockSpec((tq, n_heads * vd), lambda i: (i, 0)),
        scratch_shapes=[pltpu.VMEM((2 * n_kv, grp * tq, 1), F32), pltpu.VMEM((2 * n_kv, grp * tq, 1), F32),
                        pltpu.VMEM((2 * n_kv, grp * tq, vd), F32)],
        compiler_params=_cparams("parallel"),
        name="diff_seq",
    )(q_b, k_b.reshape(nb, tq, kw), v_b.reshape(nb, tq, v_b.shape[1]), c_lam, sub_g.reshape(1, vd))


def _sort_key(score):
    bits = pltpu.bitcast(score, I32)
    bits = jnp.where(score == 0.0, 0, bits)
    return jnp.where(bits < 0, bits ^ 0x7FFFFFFF, bits)


def _count_ge(s_ref, nblk, cand, shape, strict=False):
    rows, width = shape

    def body(kb, acc):
        for cg in range(width // LANES):
            key = s_ref[kb, :, cg * LANES:(cg + 1) * LANES]
            hit = (key > cand) if strict else (key >= cand)
            acc = acc + jnp.where(hit, 1.0, 0.0)
        return acc
    acc = lax.fori_loop(0, nblk, body, jnp.zeros((rows, LANES), F32))
    return jnp.sum(acc, axis=1, keepdims=True)


def _kth_largest(s_ref, nblk, n_sel, shape):
    rows = shape[0]

    def body(_, carry):
        ans, bit = carry
        cand = ans + bit
        cnt = _count_ge(s_ref, nblk, cand, shape)
        return jnp.where(cnt >= n_sel, cand, ans), lax.shift_right_logical(bit, 1)

    ans, _ = lax.fori_loop(0, 32, body, (jnp.full((rows, 1), INT_MIN, I32), jnp.int32(INT_MIN)))
    return ans


def _upper_ones(n):
    r = lax.broadcasted_iota(I32, (n, n), 0)
    c = lax.broadcasted_iota(I32, (n, n), 1)
    return (r <= c).astype(BF16)


def _select_ties(key, kth, need, carry, upper):
    eqf = jnp.where(key == kth, 1.0, 0.0)
    rank = carry + _dot(eqf.astype(BF16), upper)
    sel = (key > kth) | ((key == kth) & (rank <= need))
    return sel, carry + jnp.sum(eqf, axis=1, keepdims=True)


def _dsa_seq_kernel(iq_ref, iw_ref, dq_ref, ik_ref, dk_ref, dv_ref, o_ref, s_ref, m_ref, l_ref, acc_ref,
                    *, n_heads, n_kv, n_idx, n_sel, tq, tk):
    i = pl.program_id(0)
    hd = HEAD_DIM
    grp = n_heads // n_kv
    scale = hd ** -0.5
    nblk = (i * tq + tq + tk - 1) // tk
    last = nblk - 1
    qpos = i * tq + lax.broadcasted_iota(I32, (tq, tk), 0)
    col = lax.broadcasted_iota(I32, (tq, tk), 1)
    iw = iw_ref[...] * (n_idx ** -0.5)

    def idx_scores(kb):
        ikb = ik_ref[kb][:, 0:hd]
        score = None
        for h in range(n_idx):
            sc = jnp.maximum(_dot_nt(iq_ref[:, h * hd:(h + 1) * hd], ikb) * scale, 0.0)
            term = iw[:, h:h + 1] * sc
            score = term if score is None else score + term
        return score

    def fill(kb, carry):
        s_ref[kb] = _sort_key(idx_scores(kb))
        return carry

    lax.fori_loop(0, last, fill, 0)
    causal_last = last * tk + col <= qpos
    s_ref[last] = _sort_key(jnp.where(causal_last, idx_scores(last), -jnp.inf))

    shape = (tq, tk)
    kth = _kth_largest(s_ref, nblk, n_sel, shape)
    n_ge = _count_ge(s_ref, nblk, kth, shape)
    n_gt = _count_ge(s_ref, nblk, kth, shape, strict=True)
    need = n_sel - n_gt
    ties = jnp.max(n_ge) > n_sel

    m_ref[...] = jnp.full(m_ref.shape, -jnp.inf, F32)
    l_ref[...] = jnp.zeros_like(l_ref)
    acc_ref[...] = jnp.zeros_like(acc_ref)

    def attend(kb, sel):
        dkb = dk_ref[kb]
        dvb = dv_ref[kb]
        for h in range(n_heads):
            kv = h // grp
            s = _dot_nt(dq_ref[:, h * hd:(h + 1) * hd], dkb[:, kv * hd:(kv + 1) * hd]) * scale
            s = jnp.where(sel, s, -jnp.inf)
            m_old = m_ref[h]
            m_new = jnp.maximum(m_old, jnp.max(s, axis=1, keepdims=True))
            m_safe = jnp.where(m_new == -jnp.inf, 0.0, m_new)
            alpha = jnp.exp(m_old - m_safe)
            p = jnp.exp(s - m_safe)
            l_ref[h] = alpha * l_ref[h] + jnp.sum(p, axis=1, keepdims=True)
            acc_ref[h] = alpha * acc_ref[h] + _dot(p.astype(BF16), dvb[:, kv * hd:(kv + 1) * hd])
            m_ref[h] = m_new

    @pl.when(jnp.logical_not(ties))
    def _():
        def body(kb, carry):
            attend(kb, s_ref[kb] >= kth)
            return carry
        lax.fori_loop(0, last, body, 0)
        attend(last, (s_ref[last] >= kth) & causal_last)

    @pl.when(ties)
    def _():
        upper = _upper_ones(tk)

        def body(kb, carry):
            sel, carry = _select_ties(s_ref[kb], kth, need, carry, upper)
            attend(kb, sel)
            return carry
        carry = lax.fori_loop(0, last, body, jnp.zeros((tq, 1), F32))
        sel, _ = _select_ties(s_ref[last], kth, need, carry, upper)
        attend(last, sel & causal_last)

    for h in range(n_heads):
        o_ref[:, h * hd:(h + 1) * hd] = acc_ref[h] / l_ref[h]


def _dsa_seq(iq_b, iw, dq_b, ik_b, dk_b, dv_b, n_heads, n_kv, n_idx, n_sel):
    t = iq_b.shape[0]
    tq = min(128, t)
    tk = min(256, t)
    nkb = t // tk
    kvw = n_kv * HEAD_DIM
    return pl.pallas_call(
        functools.partial(_dsa_seq_kernel, n_heads=n_heads, n_kv=n_kv, n_idx=n_idx, n_sel=n_sel, tq=tq, tk=tk),
        out_shape=jax.ShapeDtypeStruct((t, n_heads * HEAD_DIM), F32),
        grid=(t // tq,),
        in_specs=[pl.BlockSpec((tq, iq_b.shape[1]), lambda i: (i, 0)),
                  pl.BlockSpec((tq, LANES), lambda i: (i, 0)),
                  pl.BlockSpec((tq, dq_b.shape[1]), lambda i: (i, 0)),
                  _resident((nkb, tk, LANES)), _resident((nkb, tk, kvw)), _resident((nkb, tk, kvw))],
        out_specs=pl.BlockSpec((tq, n_heads * HEAD_DIM), lambda i: (i, 0)),
        scratch_shapes=[pltpu.VMEM((nkb, tq, tk), I32),
                        pltpu.VMEM((n_heads, tq, 1), F32), pltpu.VMEM((n_heads, tq, 1), F32),
                        pltpu.VMEM((n_heads, tq, HEAD_DIM), F32)],
        compiler_params=_cparams("parallel"),
        name="dsa_seq",
    )(iq_b, iw, dq_b, ik_b.reshape(nkb, tk, LANES), dk_b.reshape(nkb, tk, kvw), dv_b.reshape(nkb, tk, kvw))


def _page_specs(n, width, layer, n_pages, ch):
    def spec(p):
        return pl.BlockSpec((None, None, PAGE, width),
                            lambda b, c, pt: (layer, pt[b * n_pages + c * ch + p], 0, 0))
    return [spec(p) for p in range(n)]


def _paged_attn_kernel(pt_ref, *refs, ch, mode, has_mask, lam_init, n_heads, n_kv):
    pos = 0
    qm_ref, knew_ref, vnew_ref, selnew_ref = refs[0:4]
    pos = 4
    mask_ref = None
    if has_mask:
        mask_ref = refs[pos]
        pos += 1
    lam_ref = subg_ref = None
    if mode == "diff":
        lam_ref, subg_ref = refs[pos], refs[pos + 1]
        pos += 2
    k_refs = refs[pos:pos + ch]
    v_refs = refs[pos + ch:pos + 2 * ch]
    o_ref = refs[pos + 2 * ch]
    m_ref, l_ref, acc_ref = refs[pos + 2 * ch + 1:pos + 2 * ch + 4]
    c = pl.program_id(1)
    scale = HEAD_DIM ** -0.5

    @pl.when(c == 0)
    def _():
        m_ref[...] = jnp.full(m_ref.shape, -jnp.inf, F32)
        l_ref[...] = jnp.zeros_like(l_ref)
        acc_ref[...] = jnp.zeros_like(acc_ref)

    qm = qm_ref[0]
    qb = qm.astype(BF16)
    s = jnp.concatenate([_dot_nt(qb, k_refs[p][...].astype(BF16)) for p in range(ch)], axis=1) * scale
    if has_mask:
        s = jnp.where(mask_ref[0] > 0.0, s, -jnp.inf)
    m_old = m_ref[...]
    m_new = jnp.maximum(m_old, jnp.max(s, axis=1, keepdims=True))
    m_safe = jnp.where(m_new == -jnp.inf, 0.0, m_new)
    alpha = jnp.exp(m_old - m_safe)
    p = jnp.exp(s - m_safe).astype(BF16)
    l_ref[...] = alpha * l_ref[...] + jnp.sum(p.astype(F32), axis=1, keepdims=True)
    pv = _dot(p[:, 0:PAGE], v_refs[0][...].astype(BF16))
    for j in range(1, ch):
        pv = pv + _dot(p[:, j * PAGE:(j + 1) * PAGE], v_refs[j][...].astype(BF16))
    acc_ref[...] = alpha * acc_ref[...] + pv
    m_ref[...] = m_new

    @pl.when(c == pl.num_programs(1) - 1)
    def _():
        s_new = jnp.sum(qm * knew_ref[0], axis=1, keepdims=True) * scale
        s_new = jnp.where(selnew_ref[0][:, 0:1] > 0.0, s_new, -jnp.inf)
        m_o = m_ref[...]
        m_f = jnp.maximum(m_o, s_new)
        m_s = jnp.where(m_f == -jnp.inf, 0.0, m_f)
        al = jnp.exp(m_o - m_s)
        p_new = jnp.exp(s_new - m_s)
        l = al * l_ref[...] + p_new
        o = (al * acc_ref[...] + p_new * vnew_ref[0]) / l
        rows = o.shape[0]
        r = lax.broadcasted_iota(I32, (rows, 1), 0)
        if mode == "diff":
            vd = 2 * HEAD_DIM
            grp = n_heads // n_kv
            kv_of = (r % n_heads) // grp
            osel = o[:, 0:vd]
            for kv in range(1, n_kv):
                osel = jnp.where(kv_of == kv, o[:, kv * vd:(kv + 1) * vd], osel)
            lam = _lambda(lam_ref, lam_init)
            od = osel[0:n_heads, :] - lam * osel[n_heads:2 * n_heads, :]
            o_ref[0] = _subln(od, subg_ref[...], lam_init)
        else:
            grp = n_heads // n_kv
            kv_of = r // grp
            osel = o[:, 0:HEAD_DIM]
            for kv in range(1, n_kv):
                osel = jnp.where(kv_of == kv, o[:, kv * HEAD_DIM:(kv + 1) * HEAD_DIM], osel)
            o_ref[0] = osel


def _paged_attn(page_table, layer, qm, knew, vnew, selnew, mask, cache_k, cache_v, *, mode, n_heads, n_kv,
                lam=None, sub_g=None, lam_init=0.0):
    nb, n_pages = page_table.shape
    rows, dk = qm.shape[1:]
    dv = cache_v.shape[-1]
    ch = min(16, n_pages)
    nc = n_pages // ch
    has_mask = mask is not None
    per_b = lambda shape: pl.BlockSpec((1,) + shape, lambda b, c, pt: (b,) + (0,) * len(shape))
    in_specs = [per_b((rows, dk)), per_b((1, dk)), per_b((1, dv)), per_b((1, LANES))]
    args = [qm, knew, vnew, selnew]
    if has_mask:
        in_specs.append(pl.BlockSpec((1, 1, ch * PAGE), lambda b, c, pt: (b, 0, c)))
        args.append(mask)
    if mode == "diff":
        in_specs += [pl.BlockSpec(lam.shape, lambda b, c, pt: (0, 0)),
                     pl.BlockSpec((1, 2 * HEAD_DIM), lambda b, c, pt: (0, 0))]
        args += [lam, sub_g.reshape(1, -1)]
        out_rows, out_w = n_heads, 2 * HEAD_DIM
    else:
        out_rows, out_w = n_heads, HEAD_DIM
    in_specs += _page_specs(ch, dk, layer, n_pages, ch) + _page_specs(ch, dv, layer, n_pages, ch)
    args += [cache_k] * ch + [cache_v] * ch
    grid_spec = pltpu.PrefetchScalarGridSpec(
        num_scalar_prefetch=1, grid=(nb, nc), in_specs=in_specs,
        out_specs=pl.BlockSpec((1, out_rows, out_w), lambda b, c, pt: (b, 0, 0)),
        scratch_shapes=[pltpu.VMEM((rows, 1), F32), pltpu.VMEM((rows, 1), F32), pltpu.VMEM((rows, dv), F32)])
    return pl.pallas_call(
        functools.partial(_paged_attn_kernel, ch=ch, mode=mode, has_mask=has_mask, lam_init=lam_init,
                          n_heads=n_heads, n_kv=n_kv),
        out_shape=jax.ShapeDtypeStruct((nb, out_rows, out_w), F32),
        grid_spec=grid_spec,
        compiler_params=_cparams("parallel", "arbitrary"),
        name="paged_attn_" + mode,
    )(page_table.reshape(-1), *args)


def _paged_idx_kernel(pt_ref, iq_ref, w_ref, *refs, ch, n_idx):
    pages = refs[:ch]
    o_ref = refs[ch]
    scale = HEAD_DIM ** -0.5
    iq = iq_ref[0]
    w = w_ref[0] * (n_idx ** -0.5)
    outs = []
    for p in range(ch):
        sc = jnp.maximum(_dot_nt(iq, pages[p][...].astype(BF16)) * scale, 0.0)
        outs.append(jnp.sum(w * sc, axis=0, keepdims=True))
    o_ref[0] = jnp.concatenate(outs, axis=1)


def _paged_idx_scores(page_table, layer, iq, w, cache_idx, n_idx):
    nb, n_pages = page_table.shape
    ch = min(16, n_pages)
    nc = n_pages // ch
    grid_spec = pltpu.PrefetchScalarGridSpec(
        num_scalar_prefetch=1, grid=(nb, nc),
        in_specs=[pl.BlockSpec((1, n_idx, HEAD_DIM), lambda b, c, pt: (b, 0, 0)),
                  pl.BlockSpec((1, n_idx, 1), lambda b, c, pt: (b, 0, 0))]
        + _page_specs(ch, cache_idx.shape[-1], layer, n_pages, ch),
        out_specs=pl.BlockSpec((1, 1, ch * PAGE), lambda b, c, pt: (b, 0, c)))
    return pl.pallas_call(
        functools.partial(_paged_idx_kernel, ch=ch, n_idx=n_idx),
        out_shape=jax.ShapeDtypeStruct((nb, 1, n_pages * PAGE), F32),
        grid_spec=grid_spec,
        compiler_params=_cparams("parallel", "arbitrary"),
        name="paged_idx_scores",
    )(page_table.reshape(-1), iq, w, *([cache_idx] * ch))


def _topk_mask_kernel(sc_ref, iq_ref, ik_ref, w_ref, mask_ref, mnew_ref, s_ref, *, n_idx, n_sel, bw):
    nb, t = sc_ref.shape
    nblk = t // bw
    scale = HEAD_DIM ** -0.5
    prod = iq_ref[...].astype(F32) * ik_ref[...].astype(F32)
    seg_r = lax.broadcasted_iota(I32, (prod.shape[1], LANES), 0) // HEAD_DIM
    seg_c = lax.broadcasted_iota(I32, (prod.shape[1], LANES), 1)
    qk = _dot3(prod, (seg_r == seg_c).astype(BF16))
    s_new = jnp.sum(w_ref[...] * (n_idx ** -0.5) * jnp.maximum(qk * scale, 0.0), axis=1, keepdims=True)
    lane = lax.broadcasted_iota(I32, (nb, bw), 1)
    for kb in range(nblk):
        s_ref[kb] = _sort_key(sc_ref[:, kb * bw:(kb + 1) * bw])
    s_ref[nblk] = _sort_key(jnp.where(lane == 0, s_new, -jnp.inf))
    shape = (nb, bw)
    kth = _kth_largest(s_ref, nblk + 1, n_sel, shape)
    need = n_sel - _count_ge(s_ref, nblk + 1, kth, shape, strict=True)
    upper = _upper_ones(bw)
    carry = jnp.zeros((nb, 1), F32)
    for kb in range(nblk):
        sel, carry = _select_ties(s_ref[kb], kth, need, carry, upper)
        mask_ref[:, kb * bw:(kb + 1) * bw] = jnp.where(sel, 1.0, 0.0)
    sel, _ = _select_ties(s_ref[nblk], kth, need, carry, upper)
    mnew_ref[...] = jnp.where(sel & (lane == 0), 1.0, 0.0)[:, 0:LANES]


def _topk_mask(scores, iq_b, ik_new_b, w, n_idx, n_sel):
    nb, t = scores.shape
    bw = min(256, t)
    full = lambda a: pl.BlockSpec(a.shape, lambda i: (0,) * a.ndim)
    return pl.pallas_call(
        functools.partial(_topk_mask_kernel, n_idx=n_idx, n_sel=n_sel, bw=bw),
        out_shape=(jax.ShapeDtypeStruct((nb, t), F32), jax.ShapeDtypeStruct((nb, LANES), F32)),
        grid=(1,),
        in_specs=[full(scores), full(iq_b), full(ik_new_b), full(w)],
        out_specs=(pl.BlockSpec((nb, t), lambda i: (0, 0)), pl.BlockSpec((nb, LANES), lambda i: (0, 0))),
        scratch_shapes=[pltpu.VMEM((t // bw + 1, nb, bw), I32)],
        compiler_params=_cparams("arbitrary"),
        name="topk_mask",
    )(scores, iq_b, ik_new_b, w)


def _keys_on_lanes(k_b, n_slices, tk):
    t = k_b.shape[0]
    return k_b.reshape(t // tk, tk, n_slices, HEAD_DIM).transpose(2, 0, 3, 1)


def _online_softmax_step(s, m_ref, l_ref, acc_ref, idx, v_blk):
    ncol = s.shape[1] // LANES
    cols = lambda a: [a[:, c * LANES:(c + 1) * LANES] for c in range(ncol)]
    m_old = m_ref[idx]
    m_new = jnp.maximum(m_old, jnp.max(functools.reduce(jnp.maximum, cols(s)), axis=1, keepdims=True))
    m_safe = jnp.where(m_new == -jnp.inf, 0.0, m_new)
    alpha = jnp.exp(m_old - m_safe)
    p = jnp.exp(s - m_safe)
    l_ref[idx] = alpha * l_ref[idx] + jnp.sum(functools.reduce(jnp.add, cols(p)), axis=1, keepdims=True)
    acc_ref[idx] = alpha * acc_ref[idx] + _dot(p.astype(BF16), v_blk)
    m_ref[idx] = m_new


def _diff2_kernel(q_ref, kt_ref, v_ref, lam_ref, subg_ref, o_ref, qs_ref, m_ref, l_ref, acc_ref,
                  *, n_heads, n_kv, lam_init, tq, tk):
    i = pl.program_id(0)
    grp = n_heads // n_kv
    hd = HEAD_DIM
    vd = 2 * hd
    scale = hd ** -0.5
    m_ref[...] = jnp.full(m_ref.shape, -jnp.inf, F32)
    l_ref[...] = jnp.zeros_like(l_ref)
    acc_ref[...] = jnp.zeros_like(acc_ref)
    for kv in range(n_kv):
        for c in range(2):
            qs_ref[kv * 2 + c] = jnp.concatenate(
                [q_ref[:, ((kv * grp + g) * 2 + c) * hd:((kv * grp + g) * 2 + c + 1) * hd] for g in range(grp)],
                axis=0) * scale
    n_full = (i * tq) // tk

    def step(j, masked):
        vb = v_ref[j]
        if masked:
            qpos = i * tq + lax.broadcasted_iota(I32, (grp, tq, tk), 1)
            kpos = j * tk + lax.broadcasted_iota(I32, (grp, tq, tk), 2)
            ok = (kpos <= qpos).reshape(grp * tq, tk)
        for kv in range(n_kv):
            for c in range(2):
                idx = kv * 2 + c
                s = _dot(qs_ref[idx], kt_ref[idx, j])
                if masked:
                    s = jnp.where(ok, s, -jnp.inf)
                _online_softmax_step(s, m_ref, l_ref, acc_ref, idx, vb[:, kv * vd:(kv + 1) * vd])

    def body(j, carry):
        step(j, False)
        return carry

    lax.fori_loop(0, n_full, body, 0)
    step(n_full, True)
    lam = _lambda(lam_ref, lam_init)
    for kv in range(n_kv):
        o0 = acc_ref[kv * 2] / l_ref[kv * 2]
        o1 = acc_ref[kv * 2 + 1] / l_ref[kv * 2 + 1]
        o = _subln(o0 - lam * o1, subg_ref[...], lam_init)
        for g in range(grp):
            h = kv * grp + g
            o_ref[:, h * vd:(h + 1) * vd] = o[g * tq:(g + 1) * tq, :]


def _diff_seq2(q_b, k_b, v_b, c_lam, sub_g, lam_init, n_heads, n_kv):
    t = q_b.shape[0]
    tq = min(256, t)
    tk = min(512, t)
    nkb = t // tk
    vd = 2 * HEAD_DIM
    grp = n_heads // n_kv
    kt = _keys_on_lanes(k_b, 2 * n_kv, tk)
    return pl.pallas_call(
        functools.partial(_diff2_kernel, n_heads=n_heads, n_kv=n_kv, lam_init=lam_init, tq=tq, tk=tk),
        out_shape=jax.ShapeDtypeStruct((t, n_heads * vd), F32),
        grid=(t // tq,),
        in_specs=[pl.BlockSpec((tq, q_b.shape[1]), lambda i: (i, 0)),
                  _resident(kt.shape), _resident((nkb, tk, v_b.shape[1])),
                  pl.BlockSpec(c_lam.shape, lambda i: (0, 0)),
                  pl.BlockSpec((1, vd), lambda i: (0, 0))],
        out_specs=pl.BlockSpec((tq, n_heads * vd), lambda i: (i, 0)),
        scratch_shapes=[pltpu.VMEM((2 * n_kv, grp * tq, HEAD_DIM), BF16),
                        pltpu.VMEM((2 * n_kv, grp * tq, 1), F32), pltpu.VMEM((2 * n_kv, grp * tq, 1), F32),
                        pltpu.VMEM((2 * n_kv, grp * tq, vd), F32)],
        compiler_params=_cparams("parallel"),
        name="diff_seq",
    )(q_b, kt, v_b.reshape(nkb, tk, v_b.shape[1]), c_lam, sub_g.reshape(1, vd))


def _dsa2_kernel(iq_ref, iw_ref, dq_ref, ikt_ref, dkt_ref, dv_ref, o_ref,
                 s_ref, iqs_ref, wrep_ref, qs_ref, m_ref, l_ref, acc_ref,
                 *, n_heads, n_kv, n_idx, n_sel, tq, tk):
    i = pl.program_id(0)
    hd = HEAD_DIM
    grp = n_heads // n_kv
    scale = hd ** -0.5
    nblk = (i * tq + tq + tk - 1) // tk
    last = nblk - 1
    ncol = tk // LANES

    iw = iw_ref[...] * ((n_idx ** -0.5) * scale)
    for h in range(n_idx):
        iqs_ref[h] = iq_ref[:, h * hd:(h + 1) * hd]
        wrep_ref[h] = jnp.broadcast_to(iw[:, h:h + 1], (tq, LANES))
    for kv in range(n_kv):
        qs_ref[kv] = jnp.concatenate(
            [dq_ref[:, (kv * grp + g) * hd:(kv * grp + g + 1) * hd] for g in range(grp)], axis=0) * scale

    def idx_scores(kb):
        ikb = ikt_ref[kb]
        score = None
        for h in range(n_idx):
            w = jnp.concatenate([wrep_ref[h]] * ncol, axis=1)
            term = jnp.maximum(_dot(iqs_ref[h], ikb), 0.0) * w
            score = term if score is None else score + term
        return score

    def fill(kb, carry):
        s_ref[kb] = _sort_key(idx_scores(kb))
        return carry

    lax.fori_loop(0, last, fill, 0)
    qpos = i * tq + lax.broadcasted_iota(I32, (tq, tk), 0)
    causal_last = last * tk + lax.broadcasted_iota(I32, (tq, tk), 1) <= qpos
    s_ref[last] = _sort_key(jnp.where(causal_last, idx_scores(last), -jnp.inf))

    shape = (tq, tk)

    def bis_cond(c):
        it, _, _, _, done = c
        return jnp.logical_and(it < 32, jnp.logical_not(done))

    def bis_body(c):
        it, ans, bit, cnt_ans, _ = c
        cand = ans + bit
        cnt = _count_ge(s_ref, nblk, cand, shape)
        take = cnt >= n_sel
        ans = jnp.where(take, cand, ans)
        cnt_ans = jnp.where(take, cnt, cnt_ans)
        done = jnp.min(jnp.where(cnt_ans == n_sel, 1.0, 0.0)) > 0.5
        return it + 1, ans, lax.shift_right_logical(bit, 1), cnt_ans, done

    init = (jnp.int32(0), jnp.full((tq, 1), INT_MIN, I32), jnp.int32(INT_MIN),
            jnp.full((tq, 1), 1.0, F32) * (nblk * tk).astype(F32), jnp.bool_(False))
    _, kth, _, n_ge, _ = lax.while_loop(bis_cond, bis_body, init)
    ties = jnp.max(n_ge) > n_sel

    m_ref[...] = jnp.full(m_ref.shape, -jnp.inf, F32)
    l_ref[...] = jnp.zeros_like(l_ref)
    acc_ref[...] = jnp.zeros_like(acc_ref)

    def attend(kb, sel):
        dvb = dv_ref[kb]
        for kv in range(n_kv):
            s = _dot(qs_ref[kv], dkt_ref[kv, kb]).reshape(grp, tq, tk)
            s = jnp.where(sel[None], s, -jnp.inf).reshape(grp * tq, tk)
            _online_softmax_step(s, m_ref, l_ref, acc_ref, kv, dvb)

    @pl.when(jnp.logical_not(ties))
    def _():
        def body(kb, carry):
            attend(kb, s_ref[kb] >= kth)
            return carry
        lax.fori_loop(0, last, body, 0)
        attend(last, (s_ref[last] >= kth) & causal_last)

    @pl.when(ties)
    def _():
        need = n_sel - _count_ge(s_ref, nblk, kth, shape, strict=True)
        upper = _upper_ones(tk)

        def body(kb, carry):
            sel, carry = _select_ties(s_ref[kb], kth, need, carry, upper)
            attend(kb, sel)
            return carry
        carry = lax.fori_loop(0, last, body, jnp.zeros((tq, 1), F32))
        sel, _ = _select_ties(s_ref[last], kth, need, carry, upper)
        attend(last, sel & causal_last)

    for kv in range(n_kv):
        o = acc_ref[kv] / l_ref[kv]
        for g in range(grp):
            h = kv * grp + g
            o_ref[:, h * hd:(h + 1) * hd] = o[g * tq:(g + 1) * tq, kv * hd:(kv + 1) * hd]


def _dsa_seq2(iq_b, iw, dq_b, ik_b, dk_b, dv_b, n_heads, n_kv, n_idx, n_sel):
    t = iq_b.shape[0]
    tq = min(128, t)
    tk = min(512, t)
    nkb = t // tk
    kvw = n_kv * HEAD_DIM
    grp = n_heads // n_kv
    ikt = _keys_on_lanes(ik_b[:, :HEAD_DIM], 1, tk)[0]
    dkt = _keys_on_lanes(dk_b, n_kv, tk)
    return pl.pallas_call(
        functools.partial(_dsa2_kernel, n_heads=n_heads, n_kv=n_kv, n_idx=n_idx, n_sel=n_sel, tq=tq, tk=tk),
        out_shape=jax.ShapeDtypeStruct((t, n_heads * HEAD_DIM), F32),
        grid=(t // tq,),
        in_specs=[pl.BlockSpec((tq, iq_b.shape[1]), lambda i: (i, 0)),
                  pl.BlockSpec((tq, LANES), lambda i: (i, 0)),
                  pl.BlockSpec((tq, dq_b.shape[1]), lambda i: (i, 0)),
                  _resident(ikt.shape), _resident(dkt.shape), _resident((nkb, tk, kvw))],
        out_specs=pl.BlockSpec((tq, n_heads * HEAD_DIM), lambda i: (i, 0)),
        scratch_shapes=[pltpu.VMEM((nkb, tq, tk), I32),
                        pltpu.VMEM((n_idx, tq, HEAD_DIM), BF16), pltpu.VMEM((n_idx, tq, LANES), F32),
                        pltpu.VMEM((n_kv, grp * tq, HEAD_DIM), BF16),
                        pltpu.VMEM((n_kv, grp * tq, 1), F32), pltpu.VMEM((n_kv, grp * tq, 1), F32),
                        pltpu.VMEM((n_kv, grp * tq, kvw), F32)],
        compiler_params=_cparams("parallel"),
        name="dsa_seq",
    )(iq_b, iw, dq_b, ikt, dkt, dv_b.reshape(nkb, tk, kvw))


LOG2E = 1.4426950408889634


def _flash_t(s_t, m_ref, l_ref, acc_ref, idx, v_t):
    m_old = m_ref[idx]
    m_new = jnp.maximum(m_old, jnp.max(s_t, axis=0, keepdims=True))
    m_safe = jnp.where(m_new == -jnp.inf, 0.0, m_new)
    alpha = jnp.exp2(m_old - m_safe)
    p = jnp.exp2(s_t - m_safe)
    l_ref[idx] = alpha * l_ref[idx] + jnp.sum(p, axis=0, keepdims=True)
    acc_ref[idx] = alpha * acc_ref[idx] + _dot(v_t, p.astype(BF16))
    m_ref[idx] = m_new


def _diff_t_kernel(qt_ref, k_ref, vt_ref, lam_ref, subg_ref, o_ref, qs_ref, m_ref, l_ref, acc_ref,
                   *, n_heads, n_kv, lam_init, tq, tk):
    i = pl.program_id(0)
    grp = n_heads // n_kv
    hd = HEAD_DIM
    vd = 2 * hd
    nq = grp * tq
    m_ref[...] = jnp.full(m_ref.shape, -jnp.inf, F32)
    l_ref[...] = jnp.zeros_like(l_ref)
    acc_ref[...] = jnp.zeros_like(acc_ref)
    qs_ref[...] = jnp.zeros_like(qs_ref)
    for idx in range(2 * n_kv):
        qs_ref[idx, idx * hd:(idx + 1) * hd, :] = (qt_ref[idx] * (hd ** -0.5 * LOG2E)).astype(BF16)
    n_full = (i * tq) // tk

    def step(j, masked):
        kb = k_ref[j]
        vtb = vt_ref[j]
        if masked:
            kpos = j * tk + lax.broadcasted_iota(I32, (tk, tq), 0)
            qpos = i * tq + lax.broadcasted_iota(I32, (tk, tq), 1)
            ok = jnp.concatenate([kpos <= qpos] * grp, axis=1)
        for kv in range(n_kv):
            for c in range(2):
                idx = kv * 2 + c
                s = _dot(kb, qs_ref[idx])
                if masked:
                    s = jnp.where(ok, s, -jnp.inf)
                _flash_t(s, m_ref, l_ref, acc_ref, idx, vtb[kv * vd:(kv + 1) * vd, :])

    def body(j, carry):
        step(j, False)
        return carry

    lax.fori_loop(0, n_full, body, 0)
    step(n_full, True)
    lam = _lambda(lam_ref, lam_init)
    for kv in range(n_kv):
        o0 = acc_ref[kv * 2] / l_ref[kv * 2]
        o1 = acc_ref[kv * 2 + 1] / l_ref[kv * 2 + 1]
        o = o0 - lam * o1
        o = o * lax.rsqrt(jnp.mean(o * o, axis=0, keepdims=True) + EPS) * subg_ref[...] * (1.0 - lam_init)
        for g in range(grp):
            h = kv * grp + g
            o_ref[h * vd:(h + 1) * vd, :] = o[:, g * tq:(g + 1) * tq]


def _diff_seq_t(q_f, k_b, v_b, c_lam, sub_g, lam_init, n_heads, n_kv):
    t = q_f.shape[0]
    tq = min(256, t)
    tk = min(512, t)
    nqb, nkb = t // tq, t // tk
    hd = HEAD_DIM
    vd = 2 * hd
    grp = n_heads // n_kv
    kw = k_b.shape[1]
    qt = q_f.reshape(nqb, tq, n_kv, grp, 2, hd).transpose(2, 4, 0, 5, 3, 1).reshape(2 * n_kv, nqb, hd, grp * tq)
    vt = v_b.reshape(nkb, tk, v_b.shape[1]).transpose(0, 2, 1)
    out_t = pl.pallas_call(
        functools.partial(_diff_t_kernel, n_heads=n_heads, n_kv=n_kv, lam_init=lam_init, tq=tq, tk=tk),
        out_shape=jax.ShapeDtypeStruct((nqb, n_heads * vd, tq), F32),
        grid=(nqb,),
        in_specs=[pl.BlockSpec((2 * n_kv, None, hd, grp * tq), lambda i: (0, i, 0, 0)),
                  _resident((nkb, tk, kw)), _resident(vt.shape),
                  pl.BlockSpec(c_lam.shape, lambda i: (0, 0)),
                  pl.BlockSpec((vd, 1), lambda i: (0, 0))],
        out_specs=pl.BlockSpec((None, n_heads * vd, tq), lambda i: (i, 0, 0)),
        scratch_shapes=[pltpu.VMEM((2 * n_kv, kw, grp * tq), BF16),
                        pltpu.VMEM((2 * n_kv, 1, grp * tq), F32), pltpu.VMEM((2 * n_kv, 1, grp * tq), F32),
                        pltpu.VMEM((2 * n_kv, vd, grp * tq), F32)],
        compiler_params=_cparams("parallel"),
        name="diff_seq",
    )(qt, k_b.reshape(nkb, tk, kw), vt, c_lam, sub_g.reshape(vd, 1))
    return out_t.transpose(0, 2, 1).reshape(t, n_heads * vd)


def _count_ge_t(s_ref, nblk, cand, tk, tq, strict=False):
    rows = 8 * SUBLANES

    def body(kb, acc):
        key = s_ref[kb]
        hit = (key > cand) if strict else (key >= cand)
        return acc + jnp.sum(jnp.where(hit, 1.0, 0.0).reshape(tk // rows, rows, tq), axis=0)
    acc = lax.fori_loop(0, nblk, body, jnp.zeros((rows, tq), F32))
    return jnp.sum(acc, axis=0, keepdims=True)


def _select_ties_t(key, kth, need, carry, lower):
    eqf = jnp.where(key == kth, 1.0, 0.0)
    rank = carry + _dot(lower, eqf.astype(BF16))
    sel = (key > kth) | ((key == kth) & (rank <= need))
    return sel, carry + jnp.sum(eqf, axis=0, keepdims=True)


def _dsa_t_kernel(iqt_ref, iwt_ref, dqt_ref, ik_ref, dk_ref, dvt_ref, o_ref,
                  s_ref, iqs_ref, qs_ref, m_ref, l_ref, acc_ref,
                  *, n_heads, n_kv, n_idx, n_sel, tq, tk):
    i = pl.program_id(0)
    hd = HEAD_DIM
    grp = n_heads // n_kv
    nblk = (i * tq + tq + tk - 1) // tk
    last = nblk - 1

    iqs_ref[...] = jnp.zeros_like(iqs_ref)
    qs_ref[...] = jnp.zeros_like(qs_ref)
    for hp in range(n_idx // 2):
        for j in range(2):
            iqs_ref[hp, 0:hd, j * tq:(j + 1) * tq] = iqt_ref[2 * hp + j]
    for kv in range(n_kv):
        qs_ref[kv, kv * hd:(kv + 1) * hd, :] = (dqt_ref[kv] * (hd ** -0.5 * LOG2E)).astype(BF16)
    iw = iwt_ref[...] * ((n_idx ** -0.5) * (hd ** -0.5))

    def idx_scores(kb):
        ikb = ik_ref[kb]
        score = None
        for hp in range(n_idx // 2):
            sc = jnp.maximum(_dot(ikb, iqs_ref[hp]), 0.0)
            for j in range(2):
                h = 2 * hp + j
                term = sc[:, j * tq:(j + 1) * tq] * iw[h:h + 1, :]
                score = term if score is None else score + term
        return score

    def fill(kb, carry):
        s_ref[kb] = _sort_key(idx_scores(kb))
        return carry

    lax.fori_loop(0, last, fill, 0)
    kpos = last * tk + lax.broadcasted_iota(I32, (tk, tq), 0)
    causal_last = kpos <= i * tq + lax.broadcasted_iota(I32, (tk, tq), 1)
    s_ref[last] = _sort_key(jnp.where(causal_last, idx_scores(last), -jnp.inf))

    def bis_cond(c):
        it, _, _, _, done = c
        return jnp.logical_and(it < 32, jnp.logical_not(done))

    def bis_body(c):
        it, ans, bit, cnt_ans, _ = c
        cand = ans + bit
        cnt = _count_ge_t(s_ref, nblk, cand, tk, tq)
        take = cnt >= n_sel
        ans = jnp.where(take, cand, ans)
        cnt_ans = jnp.where(take, cnt, cnt_ans)
        done = jnp.min(jnp.where(cnt_ans == n_sel, 1.0, 0.0)) > 0.5
        return it + 1, ans, lax.shift_right_logical(bit, 1), cnt_ans, done

    init = (jnp.int32(0), jnp.full((1, tq), INT_MIN, I32), jnp.int32(INT_MIN),
            jnp.full((1, tq), 1.0, F32) * (nblk * tk).astype(F32), jnp.bool_(False))
    _, kth, _, n_ge, _ = lax.while_loop(bis_cond, bis_body, init)
    ties = jnp.max(n_ge) > n_sel

    m_ref[...] = jnp.full(m_ref.shape, -jnp.inf, F32)
    l_ref[...] = jnp.zeros_like(l_ref)
    acc_ref[...] = jnp.zeros_like(acc_ref)

    def attend(kb, sel):
        dkb = dk_ref[kb]
        dvtb = dvt_ref[kb]
        bias = jnp.where(sel, 0.0, -jnp.inf)
        bias_g = jnp.concatenate([bias] * grp, axis=1)
        for kv in range(n_kv):
            s = _dot(dkb, qs_ref[kv]) + bias_g
            _flash_t(s, m_ref, l_ref, acc_ref, kv, dvtb[kv * hd:(kv + 1) * hd, :])

    @pl.when(jnp.logical_not(ties))
    def _():
        def body(kb, carry):
            attend(kb, s_ref[kb] >= kth)
            return carry
        lax.fori_loop(0, last, body, 0)
        attend(last, (s_ref[last] >= kth) & causal_last)

    @pl.when(ties)
    def _():
        need = n_sel - _count_ge_t(s_ref, nblk, kth, tk, tq, strict=True)
        r = lax.broadcasted_iota(I32, (tk, tk), 0)
        cidx = lax.broadcasted_iota(I32, (tk, tk), 1)
        lower = (cidx <= r).astype(BF16)

        def body(kb, carry):
            sel, carry = _select_ties_t(s_ref[kb], kth, need, carry, lower)
            attend(kb, sel)
            return carry
        carry = lax.fori_loop(0, last, body, jnp.zeros((1, tq), F32))
        sel, _ = _select_ties_t(s_ref[last], kth, need, carry, lower)
        attend(last, sel & causal_last)

    for kv in range(n_kv):
        o = acc_ref[kv] / l_ref[kv]
        for g in range(grp):
            h = kv * grp + g
            o_ref[h * hd:(h + 1) * hd, :] = o[:, g * tq:(g + 1) * tq]


def _dsa_seq_t(iq_b, iw, dq_f, ik_b, dk_b, dv_b, n_heads, n_kv, n_idx, n_sel):
    t = iq_b.shape[0]
    tq = min(256, t)
    tk = min(512, t)
    nqb, nkb = t // tq, t // tk
    hd = HEAD_DIM
    kvw = n_kv * hd
    grp = n_heads // n_kv
    assert n_idx % 2 == 0 and kvw == LANES
    iqt = iq_b.reshape(nqb, tq, n_idx, hd).transpose(2, 0, 3, 1)
    iwt = iw[:, :n_idx].reshape(nqb, tq, n_idx).transpose(0, 2, 1)
    dqt = dq_f.reshape(nqb, tq, n_kv, grp, hd).transpose(2, 0, 4, 3, 1).reshape(n_kv, nqb, hd, grp * tq)
    dvt = dv_b.reshape(nkb, tk, kvw).transpose(0, 2, 1)
    out_t = pl.pallas_call(
        functools.partial(_dsa_t_kernel, n_heads=n_heads, n_kv=n_kv, n_idx=n_idx, n_sel=n_sel, tq=tq, tk=tk),
        out_shape=jax.ShapeDtypeStruct((nqb, n_heads * hd, tq), F32),
        grid=(nqb,),
        in_specs=[pl.BlockSpec((n_idx, None, hd, tq), lambda i: (0, i, 0, 0)),
                  pl.BlockSpec((None, n_idx, tq), lambda i: (i, 0, 0)),
                  pl.BlockSpec((n_kv, None, hd, grp * tq), lambda i: (0, i, 0, 0)),
                  _resident((nkb, tk, LANES)), _resident((nkb, tk, kvw)), _resident(dvt.shape)],
        out_specs=pl.BlockSpec((None, n_heads * hd, tq), lambda i: (i, 0, 0)),
        scratch_shapes=[pltpu.VMEM((nkb, tk, tq), I32),
                        pltpu.VMEM((n_idx // 2, LANES, 2 * tq), BF16),
                        pltpu.VMEM((n_kv, kvw, grp * tq), BF16),
                        pltpu.VMEM((n_kv, 1, grp * tq), F32), pltpu.VMEM((n_kv, 1, grp * tq), F32),
                        pltpu.VMEM((n_kv, hd, grp * tq), F32)],
        compiler_params=_cparams("parallel"),
        name="dsa_seq",
    )(iqt, iwt, dqt, ik_b.reshape(nkb, tk, LANES), dk_b.reshape(nkb, tk, kvw), dvt)
    return out_t.transpose(0, 2, 1).reshape(t, n_heads * hd)


def _page_specs2(n, blk, layer, n_pages, ch):
    def spec(p):
        return pl.BlockSpec((None, None) + blk,
                            lambda b, c, pt: (layer, pt[b * n_pages + c * ch + p]) + (0,) * len(blk))
    return [spec(p) for p in range(n)]


def _paged2_kernel(pt_ref, *refs, ch, mode, has_mask, lam_init, n_heads, n_kv):
    qm_ref, knew_ref, vnew_ref, selnew_ref = refs[0:4]
    pos = 4
    mask_ref = None
    if has_mask:
        mask_ref = refs[pos]
        pos += 1
    lam_ref = subg_ref = None
    if mode == "diff":
        lam_ref, subg_ref = refs[pos], refs[pos + 1]
        pos += 2
    k_refs = refs[pos:pos + ch]
    v_refs = refs[pos + ch:pos + 2 * ch]
    o_ref = refs[pos + 2 * ch]
    m_ref, l_ref, acc_ref = refs[pos + 2 * ch + 1:pos + 2 * ch + 4]
    c = pl.program_id(1)
    scale = HEAD_DIM ** -0.5
    dk = qm_ref.shape[2]

    @pl.when(c == 0)
    def _():
        m_ref[...] = jnp.full(m_ref.shape, -jnp.inf, F32)
        l_ref[...] = jnp.zeros_like(l_ref)
        acc_ref[...] = jnp.zeros_like(acc_ref)

    qm = qm_ref[0]
    qb = qm.astype(BF16)
    s = jnp.concatenate([_dot(qb, k_refs[p][...].reshape(dk, PAGE).astype(BF16)) for p in range(ch)],
                        axis=1) * scale
    if has_mask:
        s = jnp.where(mask_ref[0] > 0.0, s, -jnp.inf)
    m_old = m_ref[...]
    m_new = jnp.maximum(m_old, jnp.max(s, axis=1, keepdims=True))
    m_safe = jnp.where(m_new == -jnp.inf, 0.0, m_new)
    alpha = jnp.exp(m_old - m_safe)
    p = jnp.exp(s - m_safe).astype(BF16)
    l_ref[...] = alpha * l_ref[...] + jnp.sum(p.astype(F32), axis=1, keepdims=True)

    def pv_of(j):
        pj = p[:, j * PAGE:(j + 1) * PAGE]
        if mode == "diff":
            return jnp.concatenate(
                [_dot(pj, v_refs[j][pl.ds(kv, PAGE, stride=n_kv), :].astype(BF16)) for kv in range(n_kv)], axis=1)
        return _dot_nt(pj, v_refs[j][...].reshape(-1, PAGE).astype(BF16))

    pv = pv_of(0)
    for j in range(1, ch):
        pv = pv + pv_of(j)
    acc_ref[...] = alpha * acc_ref[...] + pv
    m_ref[...] = m_new

    @pl.when(c == pl.num_programs(1) - 1)
    def _():
        s_new = jnp.sum(qm * knew_ref[0], axis=1, keepdims=True) * scale
        s_new = jnp.where(selnew_ref[0][:, 0:1] > 0.0, s_new, -jnp.inf)
        m_o = m_ref[...]
        m_f = jnp.maximum(m_o, s_new)
        m_s = jnp.where(m_f == -jnp.inf, 0.0, m_f)
        al = jnp.exp(m_o - m_s)
        p_new = jnp.exp(s_new - m_s)
        l = al * l_ref[...] + p_new
        o = (al * acc_ref[...] + p_new * vnew_ref[0]) / l
        rows = o.shape[0]
        r = lax.broadcasted_iota(I32, (rows, 1), 0)
        grp = n_heads // n_kv
        if mode == "diff":
            vd = 2 * HEAD_DIM
            kv_of = (r % n_heads) // grp
            osel = o[:, 0:vd]
            for kv in range(1, n_kv):
                osel = jnp.where(kv_of == kv, o[:, kv * vd:(kv + 1) * vd], osel)
            lam = _lambda(lam_ref, lam_init)
            od = osel[0:n_heads, :] - lam * osel[n_heads:2 * n_heads, :]
            o_ref[0] = _subln(od, subg_ref[...], lam_init)
        else:
            kv_of = r // grp
            osel = o[:, 0:HEAD_DIM]
            for kv in range(1, n_kv):
                osel = jnp.where(kv_of == kv, o[:, kv * HEAD_DIM:(kv + 1) * HEAD_DIM], osel)
            o_ref[0] = osel


def _paged_attn2(page_table, layer, qm, knew, vnew, selnew, mask, cache_kt, cache_v, *, mode, n_heads, n_kv,
                 lam=None, sub_g=None, lam_init=0.0):
    nb, n_pages = page_table.shape
    rows, dk = qm.shape[1:]
    dv = vnew.shape[-1]
    ch = min(16, n_pages)
    nc = n_pages // ch
    has_mask = mask is not None
    per_b = lambda shape: pl.BlockSpec((1,) + shape, lambda b, c, pt: (b,) + (0,) * len(shape))
    in_specs = [per_b((rows, dk)), per_b((1, dk)), per_b((1, dv)), per_b((1, LANES))]
    args = [qm, knew, vnew, selnew]
    if has_mask:
        in_specs.append(pl.BlockSpec((1, 1, ch * PAGE), lambda b, c, pt: (b, 0, c)))
        args.append(mask)
    if mode == "diff":
        in_specs += [pl.BlockSpec(lam.shape, lambda b, c, pt: (0, 0)),
                     pl.BlockSpec((1, 2 * HEAD_DIM), lambda b, c, pt: (0, 0))]
        args += [lam, sub_g.reshape(1, -1)]
        out_w = 2 * HEAD_DIM
    else:
        out_w = HEAD_DIM
    in_specs += (_page_specs2(ch, cache_kt.shape[2:], layer, n_pages, ch)
                 + _page_specs2(ch, cache_v.shape[2:], layer, n_pages, ch))
    args += [cache_kt] * ch + [cache_v] * ch
    grid_spec = pltpu.PrefetchScalarGridSpec(
        num_scalar_prefetch=1, grid=(nb, nc), in_specs=in_specs,
        out_specs=pl.BlockSpec((1, n_heads, out_w), lambda b, c, pt: (b, 0, 0)),
        scratch_shapes=[pltpu.VMEM((rows, 1), F32), pltpu.VMEM((rows, 1), F32), pltpu.VMEM((rows, dv), F32)])
    return pl.pallas_call(
        functools.partial(_paged2_kernel, ch=ch, mode=mode, has_mask=has_mask, lam_init=lam_init,
                          n_heads=n_heads, n_kv=n_kv),
        out_shape=jax.ShapeDtypeStruct((nb, n_heads, out_w), F32),
        grid_spec=grid_spec,
        compiler_params=_cparams("parallel", "arbitrary"),
        name="paged_attn_" + mode,
    )(page_table.reshape(-1), *args)


def _paged_idx2_kernel(pt_ref, iq_ref, w_ref, *refs, ch, n_idx):
    pages = refs[:ch]
    o_ref = refs[ch]
    scale = HEAD_DIM ** -0.5
    iq = iq_ref[0]
    w = w_ref[0] * (n_idx ** -0.5)
    outs = []
    for p in range(ch):
        sc = jnp.maximum(_dot(iq, pages[p][...].astype(BF16)) * scale, 0.0)
        outs.append(jnp.sum(w * sc, axis=0, keepdims=True))
    o_ref[0] = jnp.concatenate(outs, axis=1)


def _paged_idx_scores2(page_table, layer, iq, w, cache_it, n_idx):
    nb, n_pages = page_table.shape
    ch = min(16, n_pages)
    nc = n_pages // ch
    grid_spec = pltpu.PrefetchScalarGridSpec(
        num_scalar_prefetch=1, grid=(nb, nc),
        in_specs=[pl.BlockSpec((1, n_idx, HEAD_DIM), lambda b, c, pt: (b, 0, 0)),
                  pl.BlockSpec((1, n_idx, 1), lambda b, c, pt: (b, 0, 0))]
        + _page_specs2(ch, cache_it.shape[2:], layer, n_pages, ch),
        out_specs=pl.BlockSpec((1, 1, ch * PAGE), lambda b, c, pt: (b, 0, c)))
    return pl.pallas_call(
        functools.partial(_paged_idx2_kernel, ch=ch, n_idx=n_idx),
        out_shape=jax.ShapeDtypeStruct((nb, 1, n_pages * PAGE), F32),
        grid_spec=grid_spec,
        compiler_params=_cparams("parallel", "arbitrary"),
        name="paged_idx_scores",
    )(page_table.reshape(-1), iq, w, *([cache_it] * ch))


def _page_copies(pt_ref, step, slot, caches, bufs, sem, *, nc, n_pages, ch, layer):
    b = step // nc
    c = step % nc
    out = []
    for a, (cache, buf) in enumerate(zip(caches, bufs)):
        for p in range(ch):
            page = pt_ref[b * n_pages + c * ch + p]
            out.append(pltpu.make_async_copy(cache.at[layer, page], buf.at[slot, p], sem.at[a, slot]))
    return out


def _pipelined_pages(pt_ref, caches, bufs, sem, *, nc, n_pages, ch, layer):
    step = pl.program_id(0) * nc + pl.program_id(1)
    total = pl.num_programs(0) * nc
    slot = step % 2
    kw = dict(nc=nc, n_pages=n_pages, ch=ch, layer=layer)

    @pl.when(step == 0)
    def _():
        for cp in _page_copies(pt_ref, step, slot, caches, bufs, sem, **kw):
            cp.start()

    @pl.when(step + 1 < total)
    def _():
        for cp in _page_copies(pt_ref, step + 1, 1 - slot, caches, bufs, sem, **kw):
            cp.start()

    for cp in _page_copies(pt_ref, step, slot, caches, bufs, sem, **kw):
        cp.wait()
    return slot


def _paged3_kernel(pt_ref, *refs, ch, nc, n_pages, layer, mode, has_mask, lam_init, n_heads, n_kv):
    qm_ref, knew_ref, vnew_ref, selnew_ref = refs[0:4]
    pos = 4
    mask_ref = None
    if has_mask:
        mask_ref = refs[pos]
        pos += 1
    lam_ref = subg_ref = None
    if mode == "diff":
        lam_ref, subg_ref = refs[pos], refs[pos + 1]
        pos += 2
    ck_ref, cv_ref, o_ref, kbuf, vbuf, sem, m_ref, l_ref, acc_ref = refs[pos:pos + 9]
    c = pl.program_id(1)
    scale = HEAD_DIM ** -0.5
    dk = qm_ref.shape[2]
    slot = _pipelined_pages(pt_ref, (ck_ref, cv_ref), (kbuf, vbuf), sem, nc=nc, n_pages=n_pages, ch=ch, layer=layer)

    @pl.when(c == 0)
    def _():
        m_ref[...] = jnp.full(m_ref.shape, -jnp.inf, F32)
        l_ref[...] = jnp.zeros_like(l_ref)
        acc_ref[...] = jnp.zeros_like(acc_ref)

    qm = qm_ref[0]
    qb = qm.astype(BF16)
    s = jnp.concatenate([_dot(qb, kbuf[slot, p].reshape(dk, PAGE).astype(BF16)) for p in range(ch)],
                        axis=1) * scale
    if has_mask:
        s = jnp.where(mask_ref[0] > 0.0, s, -jnp.inf)
    m_old = m_ref[...]
    m_new = jnp.maximum(m_old, jnp.max(s, axis=1, keepdims=True))
    m_safe = jnp.where(m_new == -jnp.inf, 0.0, m_new)
    alpha = jnp.exp(m_old - m_safe)
    p = jnp.exp(s - m_safe).astype(BF16)
    l_ref[...] = alpha * l_ref[...] + jnp.sum(p.astype(F32), axis=1, keepdims=True)

    def pv_of(j):
        pj = p[:, j * PAGE:(j + 1) * PAGE]
        if mode == "diff":
            return jnp.concatenate(
                [_dot(pj, vbuf[slot, j, pl.ds(kv, PAGE, stride=n_kv), :].astype(BF16)) for kv in range(n_kv)],
                axis=1)
        return _dot_nt(pj, vbuf[slot, j].reshape(-1, PAGE).astype(BF16))

    pv = pv_of(0)
    for j in range(1, ch):
        pv = pv + pv_of(j)
    acc_ref[...] = alpha * acc_ref[...] + pv
    m_ref[...] = m_new

    @pl.when(c == nc - 1)
    def _():
        s_new = jnp.sum(qm * knew_ref[0], axis=1, keepdims=True) * scale
        s_new = jnp.where(selnew_ref[0][:, 0:1] > 0.0, s_new, -jnp.inf)
        m_o = m_ref[...]
        m_f = jnp.maximum(m_o, s_new)
        m_s = jnp.where(m_f == -jnp.inf, 0.0, m_f)
        al = jnp.exp(m_o - m_s)
        p_new = jnp.exp(s_new - m_s)
        l = al * l_ref[...] + p_new
        o = (al * acc_ref[...] + p_new * vnew_ref[0]) / l
        rows = o.shape[0]
        r = lax.broadcasted_iota(I32, (rows, 1), 0)
        grp = n_heads // n_kv
        if mode == "diff":
            vd = 2 * HEAD_DIM
            kv_of = (r % n_heads) // grp
            osel = o[:, 0:vd]
            for kv in range(1, n_kv):
                osel = jnp.where(kv_of == kv, o[:, kv * vd:(kv + 1) * vd], osel)
            lam = _lambda(lam_ref, lam_init)
            od = osel[0:n_heads, :] - lam * osel[n_heads:2 * n_heads, :]
            o_ref[0] = _subln(od, subg_ref[...], lam_init)
        else:
            kv_of = r // grp
            osel = o[:, 0:HEAD_DIM]
            for kv in range(1, n_kv):
                osel = jnp.where(kv_of == kv, o[:, kv * HEAD_DIM:(kv + 1) * HEAD_DIM], osel)
            o_ref[0] = osel


def _paged_attn3(page_table, layer, qm, knew, vnew, selnew, mask, cache_kt, cache_v, *, mode, n_heads, n_kv,
                 lam=None, sub_g=None, lam_init=0.0):
    nb, n_pages = page_table.shape
    rows, dk = qm.shape[1:]
    dv = vnew.shape[-1]
    ch = min(16, n_pages)
    nc = n_pages // ch
    has_mask = mask is not None
    per_b = lambda shape: pl.BlockSpec((1,) + shape, lambda b, c, pt: (b,) + (0,) * len(shape))
    in_specs = [per_b((rows, dk)), per_b((1, dk)), per_b((1, dv)), per_b((1, LANES))]
    args = [qm, knew, vnew, selnew]
    if has_mask:
        in_specs.append(pl.BlockSpec((1, 1, ch * PAGE), lambda b, c, pt: (b, 0, c)))
        args.append(mask)
    if mode == "diff":
        in_specs += [pl.BlockSpec(lam.shape, lambda b, c, pt: (0, 0)),
                     pl.BlockSpec((1, 2 * HEAD_DIM), lambda b, c, pt: (0, 0))]
        args += [lam, sub_g.reshape(1, -1)]
        out_w = 2 * HEAD_DIM
    else:
        out_w = HEAD_DIM
    in_specs += [pl.BlockSpec(memory_space=pl.ANY), pl.BlockSpec(memory_space=pl.ANY)]
    args += [cache_kt, cache_v]
    grid_spec = pltpu.PrefetchScalarGridSpec(
        num_scalar_prefetch=1, grid=(nb, nc), in_specs=in_specs,
        out_specs=pl.BlockSpec((1, n_heads, out_w), lambda b, c, pt: (b, 0, 0)),
        scratch_shapes=[pltpu.VMEM((2, ch) + cache_kt.shape[2:], F32), pltpu.VMEM((2, ch) + cache_v.shape[2:], F32),
                        pltpu.SemaphoreType.DMA((2, 2)),
                        pltpu.VMEM((rows, 1), F32), pltpu.VMEM((rows, 1), F32), pltpu.VMEM((rows, dv), F32)])
    return pl.pallas_call(
        functools.partial(_paged3_kernel, ch=ch, nc=nc, n_pages=n_pages, layer=layer, mode=mode, has_mask=has_mask,
                          lam_init=lam_init, n_heads=n_heads, n_kv=n_kv),
        out_shape=jax.ShapeDtypeStruct((nb, n_heads, out_w), F32),
        grid_spec=grid_spec,
        compiler_params=_cparams("arbitrary", "arbitrary"),
        name="paged_attn_" + mode,
    )(page_table.reshape(-1), *args)


def _paged_idx3_kernel(pt_ref, iq_ref, w_ref, ci_ref, o_ref, ibuf, sem, *, ch, nc, n_pages, layer, n_idx):
    slot = _pipelined_pages(pt_ref, (ci_ref,), (ibuf,), sem, nc=nc, n_pages=n_pages, ch=ch, layer=layer)
    scale = HEAD_DIM ** -0.5
    iq = iq_ref[0]
    w = w_ref[0] * (n_idx ** -0.5)
    outs = []
    for p in range(ch):
        sc = jnp.maximum(_dot(iq, ibuf[slot, p].astype(BF16)) * scale, 0.0)
        outs.append(jnp.sum(w * sc, axis=0, keepdims=True))
    o_ref[0] = jnp.concatenate(outs, axis=1)


def _paged_idx_scores3(page_table, layer, iq, w, cache_it, n_idx):
    nb, n_pages = page_table.shape
    ch = min(32, n_pages)
    nc = n_pages // ch
    grid_spec = pltpu.PrefetchScalarGridSpec(
        num_scalar_prefetch=1, grid=(nb, nc),
        in_specs=[pl.BlockSpec((1, n_idx, HEAD_DIM), lambda b, c, pt: (b, 0, 0)),
                  pl.BlockSpec((1, n_idx, 1), lambda b, c, pt: (b, 0, 0)),
                  pl.BlockSpec(memory_space=pl.ANY)],
        out_specs=pl.BlockSpec((1, 1, ch * PAGE), lambda b, c, pt: (b, 0, c)),
        scratch_shapes=[pltpu.VMEM((2, ch) + cache_it.shape[2:], F32), pltpu.SemaphoreType.DMA((1, 2))])
    return pl.pallas_call(
        functools.partial(_paged_idx3_kernel, ch=ch, nc=nc, n_pages=n_pages, layer=layer, n_idx=n_idx),
        out_shape=jax.ShapeDtypeStruct((nb, 1, n_pages * PAGE), F32),
        grid_spec=grid_spec,
        compiler_params=_cparams("arbitrary", "arbitrary"),
        name="paged_idx_scores",
    )(page_table.reshape(-1), iq, w, cache_it)


def _bf(w):
    return w.astype(BF16)


def _place(q3, slot_of_row, n_slots):
    onehot = (np.asarray(slot_of_row)[:, None] == np.arange(n_slots)[None, :]).astype(np.float32)
    out = q3[:, :, None, :] * jnp.asarray(onehot)[None, :, :, None]
    return out.reshape(q3.shape[0], q3.shape[1], n_slots * q3.shape[2])


def _even_weights(w_in, d_inner, conv_ch, n_a_heads, qw, kvw):
    c = np.cumsum([0, d_inner, conv_ch, n_a_heads, qw, kvw, kvw])
    z, xbc, dt, q, k, v = (w_in[:, c[j]:c[j + 1]] for j in range(6))
    dt = jnp.pad(dt, ((0, 0), (0, LANES - n_a_heads)))
    w = _bf(jnp.concatenate([z, xbc, q, k, v, dt], axis=1))
    off = np.cumsum([0, d_inner, conv_ch, qw, kvw, kvw])
    return w, dict(z=int(off[0]), xbc=int(off[1]), q=int(off[2]), k=int(off[3]), v=int(off[4]), dt=int(off[5]))


def _odd_weights(w_in, sizes):
    c = np.cumsum([0] + list(sizes))
    cq, ck, cv, dq, dk, dv, iq, iw, ik = (w_in[:, c[j]:c[j + 1]] for j in range(9))
    ik = jnp.pad(ik, ((0, 0), (0, LANES - ik.shape[1])))
    iw = jnp.pad(iw, ((0, 0), (0, LANES - iw.shape[1])))
    parts = [cq, dq, iq, ck, cv, dk, dv, ik, iw]
    off = np.cumsum([0] + [p.shape[1] for p in parts])
    names = ["cq", "dq", "iq", "ck", "cv", "dk", "dv", "ik", "iw"]
    return _bf(jnp.concatenate(parts, axis=1)), {n: int(o) for n, o in zip(names, off[:-1])}


def _mixer_even(x, pos, seq_mode, st_conv, st_ssm, win_k, win_v, norm_g, w_in, conv_w, conv_b, dt_bias, a_log,
                d_skip, gain, qn_g, kn_g, sinks, w_out):
    m, _ = x.shape
    n_a_heads = a_log.shape[0]
    d_inner = gain.shape[0]
    conv_ch = conv_w.shape[1]
    n_heads = sinks.shape[0]
    qw = n_heads * HEAD_DIM
    kvw = (w_in.shape[1] - d_inner - conv_ch - n_a_heads - qw) // 2
    n_kv = kvw // HEAD_DIM
    kw = conv_w.shape[0]
    gn = A_GROUPS * A_STATE
    hpg = n_a_heads // A_GROUPS
    w, off = _even_weights(w_in, d_inner, conv_ch, n_a_heads, qw, kvw)
    cos, sin = _rope_tables(pos)
    proj = _mm([x], [w], norm_g=norm_g, name="in_proj_even")
    q_f, q_b = _norm_rope(proj, off["q"], qw, qn_g, cos, sin, name="swa_q_rope")
    k_f, k_b = _norm_rope(proj, off["k"], kvw, kn_g, cos, sin, name="swa_k_rope")
    v_f = proj[:, off["v"]:off["v"] + kvw]
    xbc_raw = proj[:, off["xbc"]:off["xbc"] + conv_ch]
    if seq_mode:
        state8 = jnp.zeros((SUBLANES, conv_ch), F32)
        xbc = _conv_seq(proj, off["xbc"], conv_ch, state8, conv_w, conv_b)
        s0_t = jnp.zeros((A_GROUPS, A_STATE, hpg * A_HEAD_DIM), F32)
        ya, st = _ssd_seq(xbc, proj, off["z"], off["dt"], dt_bias, a_log, d_skip, gain, s0_t, n_a_heads)
        ssm_new = st.reshape(A_GROUPS, A_STATE, hpg, A_HEAD_DIM).transpose(0, 2, 3, 1).reshape(
            1, n_a_heads, A_HEAD_DIM, A_STATE)
        conv_new = xbc_raw[m - (kw - 1):][None]
        ob = _swa_seq(q_b, k_b, _bf(v_f), sinks, n_heads, n_kv)
        wb = min(WINDOW, m)
        new_k = k_f[m - wb:].reshape(1, wb, n_kv, HEAD_DIM)
        new_v = v_f[m - wb:].reshape(1, wb, n_kv, HEAD_DIM)
    else:
        xbc = _conv_step(proj, off["xbc"], conv_ch, st_conv.transpose(1, 0, 2), conv_w, conv_b)
        xdt, dec = _ssd_step_pre(xbc, proj, off["dt"], dt_bias, a_log, d_inner)
        ssm_new, y = _ssd_step(st_ssm, xdt.T, dec.T, xbc[:, d_inner:d_inner + gn], xbc[:, d_inner + gn:])
        ya = _gated_norm_call(y, xbc, proj, off["z"], d_skip, gain)
        conv_new = jnp.concatenate([st_conv[:, 1:], xbc_raw[:, None, :]], axis=1)
        wb = win_k.shape[1]
        grp = n_heads // n_kv
        qm = _place(q_f.reshape(m, n_heads, HEAD_DIM), [h // grp for h in range(n_heads)], n_kv)
        o, new_k, new_v = _swa_step(qm, k_f[:, None, :], v_f[:, None, :], win_k.reshape(m, wb, kvw),
                                    win_v.reshape(m, wb, kvw), sinks, n_heads, n_kv)
        ob = o.reshape(m, qw)
        new_k = new_k.reshape(m, wb, n_kv, HEAD_DIM)
        new_v = new_v.reshape(m, wb, n_kv, HEAD_DIM)
    wo = _bf(w_out)
    y = _mm([ya, ob], [wo[:d_inner], wo[d_inner:]], res=x, name="out_proj_even")
    return y, (conv_new, ssm_new, new_k, new_v)


def _mixer_odd(x, pos, seq_mode, paged, norm_g, w_in, qn_g, kn_g, lam_p, sub_g, dqn_g, dkn_g, w_out, lam_init,
               sizes, n_sel):
    m, _ = x.shape
    hd = HEAD_DIM
    c_heads = sizes[0] // (2 * hd)
    c_kv = sizes[1] // (2 * hd)
    d_heads = sizes[3] // hd
    d_kv = sizes[4] // hd
    n_idx = sizes[7]
    w, off = _odd_weights(w_in, sizes)
    cos, sin = _rope_tables(pos)
    proj = _mm([x], [w], norm_g=norm_g, name="in_proj_odd")
    cq_f, cq_b = _norm_rope(proj, off["cq"], sizes[0], qn_g, cos, sin, name="diff_q_rope")
    ck_f, ck_b = _norm_rope(proj, off["ck"], sizes[1], kn_g, cos, sin, name="diff_k_rope")
    dq_f, dq_b = _norm_rope(proj, off["dq"], sizes[3], dqn_g, cos, sin, name="dsa_q_rope")
    dk_f, dk_b = _norm_rope(proj, off["dk"], sizes[4], dkn_g, cos, sin, name="dsa_k_rope")
    iq_f, iq_b = _norm_rope(proj, off["iq"], sizes[6], None, cos, sin, name="idx_q_rope")
    ik_f, ik_b = _norm_rope(proj, off["ik"], LANES, None, cos, sin, name="idx_k_rope")
    cv_f = proj[:, off["cv"]:off["cv"] + sizes[2]]
    dv_f = proj[:, off["dv"]:off["dv"] + sizes[5]]
    iw = proj[:, off["iw"]:off["iw"] + LANES]
    if seq_mode:
        oc = _diff_seq_t(cq_f, ck_b, _bf(cv_f), lam_p, sub_g, lam_init, c_heads, c_kv)
        od = _dsa_seq_t(iq_b, iw, dq_f, ik_b, dk_b, _bf(dv_f), d_heads, d_kv, n_idx, n_sel)
    else:
        c_k, c_v, d_k, d_v, d_i, table, layer = paged
        pool = c_k.shape[1]
        ones = jnp.ones((m, 1, LANES), F32)
        grp = c_heads // c_kv
        q4 = cq_f.reshape(m, c_heads, 2, hd).transpose(0, 2, 1, 3).reshape(m, 2 * c_heads, hd)
        slots = [(h // grp) * 2 + c for c in range(2) for h in range(c_heads)]
        qm_c = _place(q4, slots, 2 * c_kv)
        oc = _paged_attn3(table, layer, qm_c, ck_f[:, None, :], cv_f[:, None, :], ones, None,
                          jnp.transpose(c_k, (0, 1, 3, 4, 5, 2)), c_v.reshape(c_v.shape[0], pool, PAGE * c_kv, -1),
                          mode="diff", n_heads=c_heads, n_kv=c_kv, lam=lam_p, sub_g=sub_g, lam_init=lam_init)
        oc = oc.reshape(m, -1)
        scores = _paged_idx_scores3(table, layer, iq_b.reshape(m, n_idx, hd), iw[:, :n_idx, None],
                                    jnp.transpose(d_i, (0, 1, 3, 2)), n_idx)
        ik_tiled = jnp.tile(ik_b[:, :hd], (1, n_idx))
        mask, mnew = _topk_mask(scores.reshape(m, -1), iq_b, ik_tiled, iw, n_idx, n_sel)
        dgrp = d_heads // d_kv
        qm_d = _place(dq_f.reshape(m, d_heads, hd), [h // dgrp for h in range(d_heads)], d_kv)
        od = _paged_attn3(table, layer, qm_d, dk_f[:, None, :], dv_f[:, None, :], mnew[:, None, :], mask[:, None, :],
                          jnp.transpose(d_k, (0, 1, 3, 4, 2)), jnp.transpose(d_v, (0, 1, 3, 4, 2)),
                          mode="gqa", n_heads=d_heads, n_kv=d_kv)
        od = od.reshape(m, -1)
    wo = _bf(w_out)
    y = _mm([oc, od], [wo[:oc.shape[1]], wo[oc.shape[1]:]], res=x, name="out_proj_odd")
    lead = (1, m) if seq_mode else (m, 1)
    caches = (ck_f.reshape(lead + (c_kv, 2, hd)), cv_f.reshape(lead + (c_kv, 2 * hd)),
              dk_f.reshape(lead + (d_kv, hd)), dv_f.reshape(lead + (d_kv, hd)), ik_f[:, :hd].reshape(lead + (hd,)))
    return y, caches


def kernel(x_prompt, x_sample, state_ssm, state_ssm_conv, cache_swa_k, cache_swa_v, cache_c_k, cache_c_v, cache_d_k, cache_d_v, cache_d_idx, state_ffn_conv, page_table, norm_mix_g, norm_ffn_g, a_w_in, a_conv_w, a_conv_b, a_dt_bias, a_A_log, a_D, a_norm_g, b_qn_g, b_kn_g, b_sinks, e_w_out, m_w_in, c_qn_g, c_kn_g, c_lam, c_subln_g, d_qn_g, d_kn_g, m_w_out, ffn_w_gate, ffn_w_up, ffn_conv_w, ffn_conv_b, ffn_w_down):
    bp, seq, d_model = x_prompt.shape
    nb = x_sample.shape[0]
    assert bp == 1 and x_sample.shape[1] == 1
    depth = norm_mix_g.shape[0]
    d_ff = ffn_w_gate.shape[2]
    past = page_table.shape[1] * PAGE
    xp = x_prompt.reshape(seq, d_model)
    xs = x_sample.reshape(nb, d_model)
    pos_p = jnp.arange(seq)
    pos_s = jnp.full((nb,), past, I32)
    hd = HEAD_DIM
    c_kv, d_kv, idx_dim = cache_c_k.shape[3], cache_d_k.shape[3], cache_d_idx.shape[3]
    d_heads = d_model // 128
    c_heads = d_model // 256
    n_idx = m_w_in.shape[2] - (c_heads * 2 * hd + 2 * c_kv * 2 * hd + d_heads * hd + 2 * d_kv * hd
                               + d_heads * hd + idx_dim)
    odd_sizes = (c_heads * 2 * hd, c_kv * 2 * hd, c_kv * 2 * hd, d_heads * hd, d_kv * hd, d_kv * hd,
                 d_heads * hd, n_idx, idx_dim)
    outs_p = {k: [] for k in ("ssm", "cnv", "swk", "swv", "ck", "cv", "dk", "dv", "di", "fc")}
    outs_s = {k: [] for k in outs_p}
    for i in range(depth):
        if i % 2 == 0:
            e = i // 2
            wts = (norm_mix_g[i], a_w_in[e], a_conv_w[e], a_conv_b[e], a_dt_bias[e], a_A_log[e], a_D[e],
                   a_norm_g[e], b_qn_g[e], b_kn_g[e], b_sinks[e], e_w_out[e])
            xp, (c1, s1, k1, v1) = _mixer_even(xp, pos_p, True, None, None, None, None, *wts)
            xs, (c2, s2, k2, v2) = _mixer_even(xs, pos_s, False, state_ssm_conv[e], state_ssm[e],
                                               cache_swa_k[e], cache_swa_v[e], *wts)
            for d, vals in ((outs_p, (c1, s1, k1, v1)), (outs_s, (c2, s2, k2, v2))):
                for key, val in zip(("cnv", "ssm", "swk", "swv"), vals):
                    d[key].append(val)
        else:
            o = i // 2
            lam_init = 0.8 - 0.6 * math.exp(-0.3 * i)
            wts = (norm_mix_g[i], m_w_in[o], c_qn_g[o], c_kn_g[o], c_lam[o], c_subln_g[o], d_qn_g[o], d_kn_g[o],
                   m_w_out[o], lam_init, odd_sizes)
            xp, cp = _mixer_odd(xp, pos_p, True, None, *wts, min(256, seq // 4))
            xs, cs = _mixer_odd(xs, pos_s, False,
                                (cache_c_k, cache_c_v, cache_d_k, cache_d_v, cache_d_idx, page_table, o),
                                *wts, min(256, (past + 1) // 4))
            for d, vals in ((outs_p, cp), (outs_s, cs)):
                for key, val in zip(("ck", "cv", "dk", "dv", "di"), vals):
                    d[key].append(val)
        fw = (norm_ffn_g[i], _bf(ffn_w_gate[i]), _bf(ffn_w_up[i]), ffn_conv_w[i], ffn_conv_b[i], _bf(ffn_w_down[i]))
        zrow = jnp.zeros((1, d_ff), F32)
        xp, gp = _ffn(xp, zrow, zrow, *fw, seq_mode=True)
        outs_p["fc"].append(gp[gp.shape[0] - (ffn_conv_w.shape[1] - 1):][None])
        st = state_ffn_conv[i]
        xs, gs = _ffn(xs, st[:, 0, :], st[:, 1, :], *fw, seq_mode=False)
        outs_s["fc"].append(jnp.stack([st[:, 1, :], gs], axis=1))
    order = ("ssm", "cnv", "swk", "swv", "ck", "cv", "dk", "dv", "di", "fc")
    return ((xp.reshape(1, seq, d_model), xs.reshape(nb, 1, d_model))
            + tuple(jnp.stack(outs_p[k]) for k in order) + tuple(jnp.stack(outs_s[k]) for k in order))
```

```python
import functools
import math

import jax
import jax.numpy as jnp
import numpy as np
from jax import lax
from jax.experimental import pallas as pl
from jax.experimental.pallas import tpu as pltpu

F32 = jnp.float32
BF16 = jnp.bfloat16
I32 = jnp.int32

EPS = 1e-6
ROPE_THETA = 10000.0
HEAD_DIM = 64
LANES = 128
SUBLANES = 8
VMEM_LIMIT = 56 * 1024 * 1024
WINDOW = 128
SSD_CHUNK = 128
PAGE = 128
A_GROUPS = 2
A_HEAD_DIM = 64
A_STATE = 128
INT_MIN = -2147483648


def _cparams(*sem):
    return pltpu.CompilerParams(dimension_semantics=sem, vmem_limit_bytes=VMEM_LIMIT)


def _pick_tile(n, cap):
    best = LANES
    for m in range(1, n // LANES + 1):
        if n % (m * LANES) == 0 and m * LANES <= cap:
            best = m * LANES
    return best


def _row_tile(m, cap):
    t = min(m, cap)
    while m % t:
        t //= 2
    return t


def _split3(x):
    h = x.astype(BF16)
    r = x - h.astype(F32)
    m = r.astype(BF16)
    lo = (r - m.astype(F32)).astype(BF16)
    return h, m, lo


def _dot(a, b):
    return jnp.dot(a, b, preferred_element_type=F32)


def _dot_nt(a, b):
    return lax.dot_general(a, b, (((1,), (1,)), ((), ())), preferred_element_type=F32)


def _dot3(x, w01):
    h, m, lo = _split3(x)
    return _dot(h, w01) + _dot(m, w01) + _dot(lo, w01)


def _dot3_left(w01, x):
    h, m, lo = _split3(x)
    return _dot(w01, h) + _dot(w01, m) + _dot(w01, lo)


def _silu(x):
    return x * (1.0 / (1.0 + jnp.exp(-x)))


def _softplus(x):
    return jnp.maximum(x, 0.0) + jnp.log(1.0 + jnp.exp(-jnp.abs(x)))


def _mm_kernel(*refs, n_lhs, has_norm, has_res):
    xs = refs[:n_lhs]
    pos = n_lhs
    g_ref = None
    if has_norm:
        g_ref = refs[pos]
        pos += 1
    ws = refs[pos:pos + n_lhs]
    pos += n_lhs
    res_ref = None
    if has_res:
        res_ref = refs[pos]
        pos += 1
    o_ref = refs[pos]
    xb = refs[pos + 1:pos + 1 + n_lhs]

    @pl.when(pl.program_id(1) == 0)
    def _():
        for k in range(n_lhs):
            x = xs[k][...]
            if has_norm and k == 0:
                x = x * lax.rsqrt(jnp.mean(x * x, axis=-1, keepdims=True) + EPS) * g_ref[...]
            xb[k][...] = x.astype(BF16)

    acc = _dot(xb[0][...], ws[0][...])
    for k in range(1, n_lhs):
        acc = acc + _dot(xb[k][...], ws[k][...])
    if has_res:
        acc = acc + res_ref[...]
    o_ref[...] = acc


def _mm(xs, ws, *, norm_g=None, res=None, tm_cap=512, tn_cap=1280, name="mm"):
    m = xs[0].shape[0]
    n = ws[0].shape[1]
    tm = _row_tile(m, tm_cap)
    tn = _pick_tile(n, tn_cap)
    n_lhs = len(xs)
    in_specs = [pl.BlockSpec((tm, x.shape[1]), lambda i, j: (i, 0)) for x in xs]
    args = list(xs)
    if norm_g is not None:
        in_specs.append(pl.BlockSpec((1, xs[0].shape[1]), lambda i, j: (0, 0)))
        args.append(norm_g.reshape(1, -1))
    in_specs += [pl.BlockSpec((w.shape[0], tn), lambda i, j: (0, j)) for w in ws]
    args += list(ws)
    if res is not None:
        in_specs.append(pl.BlockSpec((tm, tn), lambda i, j: (i, j)))
        args.append(res)
    return pl.pallas_call(
        functools.partial(_mm_kernel, n_lhs=n_lhs, has_norm=norm_g is not None, has_res=res is not None),
        out_shape=jax.ShapeDtypeStruct((m, n), F32),
        grid=(m // tm, n // tn),
        in_specs=in_specs,
        out_specs=pl.BlockSpec((tm, tn), lambda i, j: (i, j)),
        scratch_shapes=[pltpu.VMEM((tm, x.shape[1]), BF16) for x in xs],
        compiler_params=_cparams("parallel", "arbitrary"),
        name=name,
    )(*args)


def _ffn_kernel(x_ref, halo_ref, g_ref, wg_ref, wu_ref, cw_ref, cb_ref, wd_ref, p0_ref, p1_ref,
                o_ref, gout_ref, xb_ref, hb_ref, gs_ref, acc_ref, *, seq_mode, tm):
    i = pl.program_id(0)
    j = pl.program_id(1)
    nj = pl.num_programs(1)

    def norm(x):
        return (x * lax.rsqrt(jnp.mean(x * x, axis=-1, keepdims=True) + EPS) * g_ref[...]).astype(BF16)

    @pl.when(j == 0)
    def _():
        xb_ref[...] = norm(x_ref[...])
        if seq_mode:
            hb_ref[...] = norm(halo_ref[...])
        acc_ref[...] = jnp.zeros_like(acc_ref)

    g = _dot(xb_ref[...], wg_ref[...])
    u = _dot(xb_ref[...], wu_ref[...])
    cw = cw_ref[...]
    if seq_mode:
        carried = jnp.concatenate([jnp.zeros((SUBLANES - 2, g.shape[1]), F32), p0_ref[...], p1_ref[...]], axis=0)
        prev = jnp.where(i == 0, carried, _dot(hb_ref[...], wg_ref[...]))
        gs_ref[0:SUBLANES, :] = prev
        gs_ref[SUBLANES:, :] = g
        g1 = gs_ref[pl.ds(SUBLANES - 1, tm), :]
        g2 = gs_ref[pl.ds(SUBLANES - 2, tm), :]
        gout_ref[...] = g[tm - SUBLANES:, :]
    else:
        g1 = p1_ref[...]
        g2 = p0_ref[...]
        gout_ref[...] = g
    c = cw[0:1, :] * g2 + cw[1:2, :] * g1 + cw[2:3, :] * g + cb_ref[...]
    act = (_silu(c) * u).astype(BF16)
    acc_ref[...] += _dot(act, wd_ref[...])

    @pl.when(j == nj - 1)
    def _():
        o_ref[...] = x_ref[...] + acc_ref[...]


def _ffn(x, prev0, prev1, norm_g, wg, wu, conv_w, conv_b, wd, *, seq_mode):
    m, d = x.shape
    f = wg.shape[1]
    tm = _row_tile(m, 1024 if seq_mode else 128)
    tn = _pick_tile(f, 1408)
    ni, nj = m // tm, f // tn
    hb = tm // SUBLANES
    if seq_mode:
        prev_spec = pl.BlockSpec((1, tn), lambda i, j: (0, j))
        gout_rows, gout_shape = SUBLANES, (ni * SUBLANES, f)
    else:
        prev_spec = pl.BlockSpec((tm, tn), lambda i, j: (i, j))
        gout_rows, gout_shape = tm, (m, f)
    out, gout = pl.pallas_call(
        functools.partial(_ffn_kernel, seq_mode=seq_mode, tm=tm),
        out_shape=(jax.ShapeDtypeStruct((m, d), F32), jax.ShapeDtypeStruct(gout_shape, F32)),
        grid=(ni, nj),
        in_specs=[
            pl.BlockSpec((tm, d), lambda i, j: (i, 0)),
            pl.BlockSpec((SUBLANES, d), lambda i, j: (jnp.maximum(i * hb - 1, 0), 0)),
            pl.BlockSpec((1, d), lambda i, j: (0, 0)),
            pl.BlockSpec((d, tn), lambda i, j: (0, j)),
            pl.BlockSpec((d, tn), lambda i, j: (0, j)),
            pl.BlockSpec((conv_w.shape[0], tn), lambda i, j: (0, j)),
            pl.BlockSpec((1, tn), lambda i, j: (0, j)),
            pl.BlockSpec((tn, d), lambda i, j: (j, 0)),
            prev_spec, prev_spec,
        ],
        out_specs=(pl.BlockSpec((tm, d), lambda i, j: (i, 0)),
                   pl.BlockSpec((gout_rows, tn), lambda i, j: (i, j))),
        scratch_shapes=[pltpu.VMEM((tm, d), BF16), pltpu.VMEM((SUBLANES, d), BF16),
                        pltpu.VMEM((tm + SUBLANES, tn), F32), pltpu.VMEM((tm, d), F32)],
        compiler_params=_cparams("parallel", "arbitrary"),
        name="conv_ffn",
    )(x, x, norm_g.reshape(1, -1), wg, wu, conv_w, conv_b.reshape(1, -1), wd, prev0, prev1)
    return out, gout


def _conv_seq_kernel(x_ref, halo_ref, st_ref, w_ref, b_ref, o_ref, xs_ref, *, tm, kw):
    i = pl.program_id(0)
    prev = jnp.where(i == 0, st_ref[...], halo_ref[...])
    xs_ref[0:SUBLANES, :] = prev
    xs_ref[SUBLANES:, :] = x_ref[...]
    w = w_ref[...]
    acc = b_ref[...] + w[kw - 1:kw, :] * x_ref[...]
    for t in range(1, kw):
        acc = acc + w[kw - 1 - t:kw - t, :] * xs_ref[pl.ds(SUBLANES - t, tm), :]
    o_ref[...] = _silu(acc)


def _conv_seq(src, col0, width, state8, w, b):
    m = src.shape[0]
    kw = w.shape[0]
    tm = _row_tile(m, 512)
    tn = _pick_tile(math.gcd(width, col0) if col0 else width, 512)
    cb = col0 // tn
    hb = tm // SUBLANES
    return pl.pallas_call(
        functools.partial(_conv_seq_kernel, tm=tm, kw=kw),
        out_shape=jax.ShapeDtypeStruct((m, width), F32),
        grid=(m // tm, width // tn),
        in_specs=[
            pl.BlockSpec((tm, tn), lambda i, j: (i, cb + j)),
            pl.BlockSpec((SUBLANES, tn), lambda i, j: (jnp.maximum(i * hb - 1, 0), cb + j)),
            pl.BlockSpec((SUBLANES, tn), lambda i, j: (0, j)),
            pl.BlockSpec((kw, tn), lambda i, j: (0, j)),
            pl.BlockSpec((1, tn), lambda i, j: (0, j)),
        ],
        out_specs=pl.BlockSpec((tm, tn), lambda i, j: (i, j)),
        scratch_shapes=[pltpu.VMEM((tm + SUBLANES, tn), F32)],
        compiler_params=_cparams("parallel", "parallel"),
        name="ssm_conv_seq",
    )(src, src, state8, w, b.reshape(1, -1))


def _conv_step_kernel(x_ref, s_ref, w_ref, b_ref, o_ref, *, kw):
    w = w_ref[...]
    acc = b_ref[...] + w[kw - 1:kw, :] * x_ref[...]
    for t in range(kw - 1):
        acc = acc + w[t:t + 1, :] * s_ref[t]
    o_ref[...] = _silu(acc)


def _conv_step(src, col0, width, state, w, b):
    m = src.shape[0]
    kw = w.shape[0]
    tn = _pick_tile(math.gcd(width, col0) if col0 else width, 512)
    cb = col0 // tn
    return pl.pallas_call(
        functools.partial(_conv_step_kernel, kw=kw),
        out_shape=jax.ShapeDtypeStruct((m, width), F32),
        grid=(width // tn,),
        in_specs=[
            pl.BlockSpec((m, tn), lambda j: (0, cb + j)),
            pl.BlockSpec((kw - 1, m, tn), lambda j: (0, 0, j)),
            pl.BlockSpec((kw, tn), lambda j: (0, j)),
            pl.BlockSpec((1, tn), lambda j: (0, j)),
        ],
        out_specs=pl.BlockSpec((m, tn), lambda j: (0, j)),
        compiler_params=_cparams("parallel"),
        name="ssm_conv_step",
    )(src, state, w, b.reshape(1, -1))


def _seg_ones(seg):
    r = lax.broadcasted_iota(I32, (LANES, LANES), 0) // seg
    c = lax.broadcasted_iota(I32, (LANES, LANES), 1) // seg
    return (r == c).astype(BF16)


def _rope128(x, cos, sin_signed):
    lane = lax.broadcasted_iota(I32, x.shape, 1)
    rot = jnp.where(lane % HEAD_DIM < HEAD_DIM // 2,
                    pltpu.roll(x, LANES - HEAD_DIM // 2, 1), pltpu.roll(x, HEAD_DIM // 2, 1))
    return x * cos + rot * sin_signed


def _norm_rope_kernel(x_ref, g_ref, cos_ref, sin_ref, o_ref, ob_ref, *, do_norm, width):
    cos = cos_ref[...]
    sin = sin_ref[...]
    ones = _seg_ones(HEAD_DIM)
    for c in range(width // LANES):
        x = x_ref[:, c * LANES:(c + 1) * LANES]
        if do_norm:
            ms = _dot3(x * x, ones) * (1.0 / HEAD_DIM)
            x = x * lax.rsqrt(ms + EPS) * g_ref[...]
        y = _rope128(x, cos, sin)
        o_ref[:, c * LANES:(c + 1) * LANES] = y
        ob_ref[:, c * LANES:(c + 1) * LANES] = y.astype(BF16)


def _norm_rope(src, col0, width, gain, cos, sin, *, name):
    m = src.shape[0]
    tm = _row_tile(m, 512)
    assert col0 % width == 0
    cb = col0 // width
    g = jnp.ones((1, LANES), F32) if gain is None else jnp.tile(gain.reshape(1, HEAD_DIM), (1, LANES // HEAD_DIM))
    return pl.pallas_call(
        functools.partial(_norm_rope_kernel, do_norm=gain is not None, width=width),
        out_shape=(jax.ShapeDtypeStruct((m, width), F32), jax.ShapeDtypeStruct((m, width), BF16)),
        grid=(m // tm,),
        in_specs=[
            pl.BlockSpec((tm, width), lambda i: (i, cb)),
            pl.BlockSpec((1, LANES), lambda i: (0, 0)),
            pl.BlockSpec((tm, LANES), lambda i: (i, 0)),
            pl.BlockSpec((tm, LANES), lambda i: (i, 0)),
        ],
        out_specs=(pl.BlockSpec((tm, width), lambda i: (i, 0)), pl.BlockSpec((tm, width), lambda i: (i, 0))),
        compiler_params=_cparams("parallel"),
        name=name,
    )(src, g, cos, sin)


def _rope_tables(pos):
    half = HEAD_DIM // 2
    inv = ROPE_THETA ** (-jnp.arange(half, dtype=F32) / half)
    ang = pos.astype(F32)[:, None] * inv[None, :]
    cos, sin = jnp.cos(ang), jnp.sin(ang)
    cos128 = jnp.tile(jnp.concatenate([cos, cos], axis=1), (1, LANES // HEAD_DIM))
    sin128 = jnp.tile(jnp.concatenate([-sin, sin], axis=1), (1, LANES // HEAD_DIM))
    return cos128, sin128


def _head_expand(n_heads_pad, width):
    r = lax.broadcasted_iota(I32, (n_heads_pad, width), 0)
    c = lax.broadcasted_iota(I32, (n_heads_pad, width), 1) // A_HEAD_DIM
    return (r == c).astype(BF16)


def _gated_norm(y, xs, z, dskip, gain):
    yz = (y + xs * dskip) * _silu(z)
    gw = yz.shape[1] // A_GROUPS
    parts = []
    for g in range(A_GROUPS):
        p = yz[:, g * gw:(g + 1) * gw]
        parts.append(p * lax.rsqrt(jnp.mean(p * p, axis=-1, keepdims=True) + EPS))
    return jnp.concatenate(parts, axis=1) * gain


def _ssd_seq_kernel(xs_ref, b_ref, c_ref, dt_ref, z_ref, dtb_ref, alog_ref, dskip_ref, gain_ref, s0_ref,
                    ya_ref, sout_ref, st_ref, y_ref, *, n_heads):
    ci = pl.program_id(0)
    q = SSD_CHUNK
    d_inner = xs_ref.shape[1]
    hpg = n_heads // A_GROUPS
    gw = d_inner // A_GROUPS

    @pl.when(ci == 0)
    def _():
        st_ref[...] = s0_ref[...]

    xs = xs_ref[...]
    dt = _softplus(dt_ref[...] + dtb_ref[...])
    a = dt * (-jnp.exp(alog_ref[...]))
    row = lax.broadcasted_iota(I32, (q, q), 0)
    col = lax.broadcasted_iota(I32, (q, q), 1)
    causal = col <= row
    tri = causal.astype(BF16)
    tri_t = (row <= col).astype(BF16)
    acs = _dot3_left(tri, a)
    acs_t = _dot3(a.T, tri_t)
    expand = _head_expand(LANES, d_inner)
    acs_x = _dot3(acs, expand)
    dt_x = _dot3(dt, expand)
    e_acs = jnp.exp(acs_x)
    last = acs_x[q - 1:q, :]
    decay_s = jnp.exp(last - acs_x)
    xdt = xs * dt_x
    xdt_b = xdt.astype(BF16)
    xdec_b = (xdt * decay_s).astype(BF16)
    chunk_decay = e_acs[q - 1:q, :]

    for g in range(A_GROUPS):
        bg = b_ref[:, g * A_STATE:(g + 1) * A_STATE]
        cg = c_ref[:, g * A_STATE:(g + 1) * A_STATE].astype(BF16)
        cb = _dot_nt(cg, bg.astype(BF16))
        st_g = st_ref[g]
        y_off = _dot(cg, st_g.astype(BF16)) * e_acs[:, g * gw:(g + 1) * gw]
        for hh in range(hpg):
            h = g * hpg + hh
            diff = acs[:, h:h + 1] - acs_t[h:h + 1, :]
            m = (cb * jnp.where(causal, jnp.exp(diff), 0.0)).astype(BF16)
            lo = h * A_HEAD_DIM
            y_ref[:, lo:lo + A_HEAD_DIM] = (_dot(m, xdt_b[:, lo:lo + A_HEAD_DIM])
                                            + y_off[:, hh * A_HEAD_DIM:(hh + 1) * A_HEAD_DIM])
        st_ref[g] = st_g * chunk_decay[:, g * gw:(g + 1) * gw] + _dot(bg.T.astype(BF16), xdec_b[:, g * gw:(g + 1) * gw])

    ya_ref[...] = _gated_norm(y_ref[...], xs, z_ref[...], dskip_ref[...], gain_ref[...])

    @pl.when(ci == pl.num_programs(0) - 1)
    def _():
        sout_ref[...] = st_ref[...]


def _ssd_seq(xbc, proj, z_col, dt_col, dt_bias, a_log, d_skip, gain, s0_t, n_heads):
    t = xbc.shape[0]
    q = SSD_CHUNK
    d_inner = n_heads * A_HEAD_DIM
    gn = A_GROUPS * A_STATE
    bcol = d_inner // gn
    pad = lambda v: jnp.pad(v.reshape(1, -1), ((0, 0), (0, LANES - v.shape[-1])))
    dskip_x = jnp.repeat(d_skip, A_HEAD_DIM).reshape(1, d_inner)
    const = lambda shape: pl.BlockSpec(shape, lambda c: (0,) * len(shape))
    return pl.pallas_call(
        functools.partial(_ssd_seq_kernel, n_heads=n_heads),
        out_shape=(jax.ShapeDtypeStruct((t, d_inner), F32), jax.ShapeDtypeStruct(s0_t.shape, F32)),
        grid=(t // q,),
        in_specs=[
            pl.BlockSpec((q, d_inner), lambda c: (c, 0)),
            pl.BlockSpec((q, gn), lambda c: (c, bcol)),
            pl.BlockSpec((q, gn), lambda c: (c, bcol + 1)),
            pl.BlockSpec((q, LANES), lambda c: (c, dt_col // LANES)),
            pl.BlockSpec((q, d_inner), lambda c: (c, z_col // d_inner)),
            const((1, LANES)), const((1, LANES)), const((1, d_inner)), const((1, d_inner)),
            const(s0_t.shape),
        ],
        out_specs=(pl.BlockSpec((q, d_inner), lambda c: (c, 0)), const(s0_t.shape)),
        scratch_shapes=[pltpu.VMEM(s0_t.shape, F32), pltpu.VMEM((q, d_inner), F32)],
        compiler_params=_cparams("arbitrary"),
        name="ssd_seq",
    )(xbc, xbc, xbc, proj, proj, pad(dt_bias), pad(a_log), dskip_x, gain.reshape(1, -1), s0_t)


def _ssd_step_kernel(s_ref, xdt_t_ref, dec_t_ref, b_ref, c_ref, sout_ref, y_ref, *, n_heads):
    b = pl.program_id(0)
    nb = xdt_t_ref.shape[1]
    hp = n_heads * A_HEAD_DIM
    gw = hp // A_GROUPS
    lane = lax.broadcasted_iota(I32, (hp, nb), 1)
    dec = jnp.sum(jnp.where(lane == b, dec_t_ref[...], 0.0), axis=1, keepdims=True)
    rows = lax.broadcasted_iota(I32, (nb, A_STATE), 0)
    s = s_ref[0].reshape(hp, A_STATE)
    xdt_t = xdt_t_ref[...].astype(BF16)
    outs = []
    for g in range(A_GROUPS):
        brow = b_ref[0, :, g * A_STATE:(g + 1) * A_STATE]
        zb = jnp.where(rows == b, jnp.broadcast_to(brow, (nb, A_STATE)), 0.0).astype(BF16)
        upd = _dot(xdt_t[g * gw:(g + 1) * gw, :], zb)
        sn = s[g * gw:(g + 1) * gw, :] * dec[g * gw:(g + 1) * gw, :] + upd
        sout_ref[0, g * (n_heads // A_GROUPS):(g + 1) * (n_heads // A_GROUPS)] = sn.reshape(
            n_heads // A_GROUPS, A_HEAD_DIM, A_STATE)
        crow = c_ref[0, :, g * A_STATE:(g + 1) * A_STATE]
        c8 = jnp.broadcast_to(crow, (SUBLANES, A_STATE)).astype(BF16)
        outs.append(_dot_nt(c8, sn.astype(BF16))[0:1, :])
    y_ref[0] = jnp.concatenate(outs, axis=1)


def _ssd_step(state, xdt_t, dec_t, bmat, cmat):
    nb, n_heads, p, n = state.shape
    hp = n_heads * p
    new_state, y = pl.pallas_call(
        functools.partial(_ssd_step_kernel, n_heads=n_heads),
        out_shape=(jax.ShapeDtypeStruct(state.shape, F32), jax.ShapeDtypeStruct((nb, 1, hp), F32)),
        grid=(nb,),
        in_specs=[
            pl.BlockSpec((1, n_heads, p, n), lambda b: (b, 0, 0, 0)),
            pl.BlockSpec((hp, nb), lambda b: (0, 0)),
            pl.BlockSpec((hp, nb), lambda b: (0, 0)),
            pl.BlockSpec((1, 1, bmat.shape[1]), lambda b: (b, 0, 0)),
            pl.BlockSpec((1, 1, cmat.shape[1]), lambda b: (b, 0, 0)),
        ],
        out_specs=(pl.BlockSpec((1, n_heads, p, n), lambda b: (b, 0, 0, 0)),
                   pl.BlockSpec((1, 1, hp), lambda b: (b, 0, 0))),
        compiler_params=_cparams("arbitrary"),
        name="ssd_step",
    )(state, xdt_t, dec_t, bmat[:, None, :], cmat[:, None, :])
    return new_state, y.reshape(nb, hp)


def _ssd_step_pre_kernel(xs_ref, dt_ref, dtb_ref, alog_ref, xdt_ref, dec_ref):
    dt = _softplus(dt_ref[...] + dtb_ref[...])
    expand = _head_expand(LANES, xs_ref.shape[1])
    xdt_ref[...] = xs_ref[...] * _dot3(dt, expand)
    dec_ref[...] = jnp.exp(_dot3(dt * (-jnp.exp(alog_ref[...])), expand))


def _ssd_step_pre(xbc, proj, dt_col, dt_bias, a_log, d_inner):
    m = xbc.shape[0]
    pad = lambda v: jnp.pad(v.reshape(1, -1), ((0, 0), (0, LANES - v.shape[-1])))
    return pl.pallas_call(
        _ssd_step_pre_kernel,
        out_shape=(jax.ShapeDtypeStruct((m, d_inner), F32), jax.ShapeDtypeStruct((m, d_inner), F32)),
        grid=(1,),
        in_specs=[pl.BlockSpec((m, d_inner), lambda i: (0, 0)),
                  pl.BlockSpec((m, LANES), lambda i: (0, dt_col // LANES)),
                  pl.BlockSpec((1, LANES), lambda i: (0, 0)), pl.BlockSpec((1, LANES), lambda i: (0, 0))],
        out_specs=(pl.BlockSpec((m, d_inner), lambda i: (0, 0)), pl.BlockSpec((m, d_inner), lambda i: (0, 0))),
        compiler_params=_cparams("arbitrary"),
        name="ssd_step_pre",
    )(xbc, proj, pad(dt_bias), pad(a_log))


def _gated_norm_kernel(y_ref, xs_ref, z_ref, dskip_ref, gain_ref, o_ref):
    o_ref[...] = _gated_norm(y_ref[...], xs_ref[...], z_ref[...], dskip_ref[...], gain_ref[...])


def _gated_norm_call(y, xbc, proj, z_col, d_skip, gain):
    m, d_inner = y.shape
    dskip_x = jnp.repeat(d_skip, A_HEAD_DIM).reshape(1, d_inner)
    return pl.pallas_call(
        _gated_norm_kernel,
        out_shape=jax.ShapeDtypeStruct((m, d_inner), F32),
        grid=(1,),
        in_specs=[pl.BlockSpec((m, d_inner), lambda i: (0, 0)),
                  pl.BlockSpec((m, d_inner), lambda i: (0, 0)),
                  pl.BlockSpec((m, d_inner), lambda i: (0, z_col // d_inner)),
                  pl.BlockSpec((1, d_inner), lambda i: (0, 0)), pl.BlockSpec((1, d_inner), lambda i: (0, 0))],
        out_specs=pl.BlockSpec((m, d_inner), lambda i: (0, 0)),
        compiler_params=_cparams("arbitrary"),
        name="gated_norm",
    )(y, xbc, proj, dskip_x, gain.reshape(1, -1))


def _swa_seq_kernel(sink_ref, q_ref, kc_ref, kp_ref, vc_ref, vp_ref, o_ref, *, n_heads, n_kv):
    i = pl.program_id(0)
    w = WINDOW
    grp = n_heads // n_kv
    r = lax.broadcasted_iota(I32, (grp * w, 2 * w), 0) % w
    c = lax.broadcasted_iota(I32, (grp * w, 2 * w), 1)
    ok = (c > r) & (c <= r + w) & ((i > 0) | (c >= w))
    hrow = lax.broadcasted_iota(I32, (grp * w, 1), 0) // w
    scale = HEAD_DIM ** -0.5
    for kv in range(n_kv):
        sl = slice(kv * HEAD_DIM, (kv + 1) * HEAD_DIM)
        kcat = jnp.concatenate([kp_ref[:, sl], kc_ref[:, sl]], axis=0)
        vcat = jnp.concatenate([vp_ref[:, sl], vc_ref[:, sl]], axis=0)
        q4 = jnp.concatenate([q_ref[:, (kv * grp + j) * HEAD_DIM:(kv * grp + j + 1) * HEAD_DIM]
                              for j in range(grp)], axis=0)
        sink = jnp.zeros((grp * w, 1), F32)
        for j in range(grp):
            sink = jnp.where(hrow == j, sink_ref[kv * grp + j], sink)
        s = jnp.where(ok, _dot_nt(q4, kcat) * scale, -jnp.inf)
        m = jnp.maximum(jnp.max(s, axis=1, keepdims=True), sink)
        p = jnp.exp(s - m)
        denom = jnp.sum(p, axis=1, keepdims=True) + jnp.exp(sink - m)
        o = _dot(p.astype(BF16), vcat) / denom
        for j in range(grp):
            h = kv * grp + j
            o_ref[:, h * HEAD_DIM:(h + 1) * HEAD_DIM] = o[j * w:(j + 1) * w, :]


def _swa_seq(q_b, k_b, v_b, sinks, n_heads, n_kv):
    t = q_b.shape[0]
    w = WINDOW
    kvw = n_kv * HEAD_DIM
    cur = lambda i: (i, 0)
    prv = lambda i: (jnp.maximum(i - 1, 0), 0)
    return pl.pallas_call(
        functools.partial(_swa_seq_kernel, n_heads=n_heads, n_kv=n_kv),
        out_shape=jax.ShapeDtypeStruct((t, n_heads * HEAD_DIM), F32),
        grid=(t // w,),
        in_specs=[pl.BlockSpec(memory_space=pltpu.SMEM),
                  pl.BlockSpec((w, n_heads * HEAD_DIM), cur),
                  pl.BlockSpec((w, kvw), cur), pl.BlockSpec((w, kvw), prv),
                  pl.BlockSpec((w, kvw), cur), pl.BlockSpec((w, kvw), prv)],
        out_specs=pl.BlockSpec((w, n_heads * HEAD_DIM), cur),
        compiler_params=_cparams("parallel"),
        name="swa_seq",
    )(sinks, q_b, k_b, k_b, v_b, v_b)


def _swa_step_kernel(sink_ref, qm_ref, knew_ref, vnew_ref, kc_ref, vc_ref, o_ref, ko_ref, vo_ref,
                     *, n_heads, n_kv, bs):
    grp = n_heads // n_kv
    wb = kc_ref.shape[1]
    scale = HEAD_DIM ** -0.5
    hrow = lax.broadcasted_iota(I32, (n_heads, 1), 0)
    sink = jnp.zeros((n_heads, 1), F32)
    for h in range(n_heads):
        sink = jnp.where(hrow == h, sink_ref[h], sink)
    col = lax.broadcasted_iota(I32, (n_heads, wb), 1)
    ok = col > wb - WINDOW
    for bi in range(bs):
        qm = qm_ref[bi]
        kc = kc_ref[bi]
        vc = vc_ref[bi]
        knew = knew_ref[bi]
        vnew = vnew_ref[bi]
        s = jnp.where(ok, _dot_nt(qm.astype(BF16), kc.astype(BF16)) * scale, -jnp.inf)
        s_new = jnp.sum(qm * knew, axis=1, keepdims=True) * scale
        m = jnp.maximum(jnp.maximum(jnp.max(s, axis=1, keepdims=True), s_new), sink)
        p = jnp.exp(s - m)
        p_new = jnp.exp(s_new - m)
        denom = jnp.sum(p, axis=1, keepdims=True) + p_new + jnp.exp(sink - m)
        o = (_dot(p.astype(BF16), vc.astype(BF16)) + p_new * vnew) / denom
        osel = o[:, 0:HEAD_DIM]
        for kv in range(1, n_kv):
            osel = jnp.where(hrow // grp == kv, o[:, kv * HEAD_DIM:(kv + 1) * HEAD_DIM], osel)
        o_ref[bi] = osel
        ko_ref[bi, 0:wb - 1, :] = kc_ref[bi, 1:wb, :]
        ko_ref[bi, wb - 1:wb, :] = knew
        vo_ref[bi, 0:wb - 1, :] = vc_ref[bi, 1:wb, :]
        vo_ref[bi, wb - 1:wb, :] = vnew


def _swa_step(qm, knew, vnew, cache_k, cache_v, sinks, n_heads, n_kv):
    nb, wb, kvw = cache_k.shape
    bs = 8 if nb % 8 == 0 else 1
    blk = lambda shape: pl.BlockSpec((bs,) + shape, lambda b: (b,) + (0,) * len(shape))
    return pl.pallas_call(
        functools.partial(_swa_step_kernel, n_heads=n_heads, n_kv=n_kv, bs=bs),
        out_shape=(jax.ShapeDtypeStruct((nb, n_heads, HEAD_DIM), F32),
                   jax.ShapeDtypeStruct(cache_k.shape, F32), jax.ShapeDtypeStruct(cache_v.shape, F32)),
        grid=(nb // bs,),
        in_specs=[pl.BlockSpec(memory_space=pltpu.SMEM), blk((n_heads, kvw)), blk((1, kvw)), blk((1, kvw)),
                  blk((wb, kvw)), blk((wb, kvw))],
        out_specs=(blk((n_heads, HEAD_DIM)), blk((wb, kvw)), blk((wb, kvw))),
        compiler_params=_cparams("parallel"),
        name="swa_step",
    )(sinks, qm, knew, vnew, cache_k, cache_v)


def _lambda(lam_ref, lam_init):
    lp = lam_ref[...]
    return (jnp.exp(jnp.sum(lp[0:1, :] * lp[1:2, :], axis=1, keepdims=True))
            - jnp.exp(jnp.sum(lp[2:3, :] * lp[3:4, :], axis=1, keepdims=True)) + lam_init)


def _subln(o, subg, lam_init):
    return o * lax.rsqrt(jnp.mean(o * o, axis=-1, keepdims=True) + EPS) * subg * (1.0 - lam_init)


def _diff_seq_kernel(q_ref, k_ref, v_ref, lam_ref, subg_ref, o_ref, m_ref, l_ref, acc_ref,
                     *, n_heads, n_kv, lam_init, tq):
    i = pl.program_id(0)
    grp = n_heads // n_kv
    hd = HEAD_DIM
    vd = 2 * hd
    scale = hd ** -0.5
    m_ref[...] = jnp.full(m_ref.shape, -jnp.inf, F32)
    l_ref[...] = jnp.zeros_like(l_ref)
    acc_ref[...] = jnp.zeros_like(acc_ref)
    row = lax.broadcasted_iota(I32, (grp * tq, tq), 0) % tq
    col = lax.broadcasted_iota(I32, (grp * tq, tq), 1)
    causal = col <= row

    def step(j, masked):
        kb = k_ref[j]
        vb = v_ref[j]
        for kv in range(n_kv):
            for c in range(2):
                idx = kv * 2 + c
                q2 = jnp.concatenate(
                    [q_ref[:, ((kv * grp + g) * 2 + c) * hd:((kv * grp + g) * 2 + c + 1) * hd] for g in range(grp)],
                    axis=0)
                s = _dot_nt(q2, kb[:, idx * hd:(idx + 1) * hd]) * scale
                if masked:
                    s = jnp.where(causal, s, -jnp.inf)
                m_old = m_ref[idx]
                m_new = jnp.maximum(m_old, jnp.max(s, axis=1, keepdims=True))
                alpha = jnp.exp(m_old - m_new)
                p = jnp.exp(s - m_new)
                l_ref[idx] = alpha * l_ref[idx] + jnp.sum(p, axis=1, keepdims=True)
                acc_ref[idx] = alpha * acc_ref[idx] + _dot(p.astype(BF16), vb[:, kv * vd:(kv + 1) * vd])
                m_ref[idx] = m_new

    def body(j, carry):
        step(j, False)
        return carry

    lax.fori_loop(0, i, body, 0)
    step(i, True)
    lam = _lambda(lam_ref, lam_init)
    for kv in range(n_kv):
        o0 = acc_ref[kv * 2] / l_ref[kv * 2]
        o1 = acc_ref[kv * 2 + 1] / l_ref[kv * 2 + 1]
        o = _subln(o0 - lam * o1, subg_ref[...], lam_init)
        for g in range(grp):
            h = kv * grp + g
            o_ref[:, h * vd:(h + 1) * vd] = o[g * tq:(g + 1) * tq, :]


def _resident(shape):
    return pl.BlockSpec(shape, lambda *_: (0,) * len(shape), pipeline_mode=pl.Buffered(1))


def _diff_seq(q_b, k_b, v_b, c_lam, sub_g, lam_init, n_heads, n_kv):
    t = q_b.shape[0]
    tq = min(256, t)
    nb = t // tq
    kw = k_b.shape[1]
    vd = 2 * HEAD_DIM
    grp = n_heads // n_kv
    return pl.pallas_call(
        functools.partial(_diff_seq_kernel, n_heads=n_heads, n_kv=n_kv, lam_init=lam_init, tq=tq),
        out_shape=jax.ShapeDtypeStruct((t, n_heads * vd), F32),
        grid=(nb,),
        in_specs=[pl.BlockSpec((tq, q_b.shape[1]), lambda i: (i, 0)),
                  _resident((nb, tq, kw)), _resident((nb, tq, v_b.shape[1])),
                  pl.BlockSpec(c_lam.shape, lambda i: (0, 0)),
                  pl.BlockSpec((1, vd), lambda i: (0, 0))],
        out_specs=pl.BlockSpec((tq, n_heads * vd), lambda i: (i, 0)),
        scratch_shapes=[pltpu.VMEM((2 * n_kv, grp * tq, 1), F32), pltpu.VMEM((2 * n_kv, grp * tq, 1), F32),
                        pltpu.VMEM((2 * n_kv, grp * tq, vd), F32)],
        compiler_params=_cparams("parallel"),
        name="diff_seq",
    )(q_b, k_b.reshape(nb, tq, kw), v_b.reshape(nb, tq, v_b.shape[1]), c_lam, sub_g.reshape(1, vd))


def _sort_key(score):
    bits = pltpu.bitcast(score, I32)
    bits = jnp.where(score == 0.0, 0, bits)
    return jnp.where(bits < 0, bits ^ 0x7FFFFFFF, bits)


def _count_ge(s_ref, nblk, cand, shape, strict=False):
    rows, width = shape

    def body(kb, acc):
        for cg in range(width // LANES):
            key = s_ref[kb, :, cg * LANES:(cg + 1) * LANES]
            hit = (key > cand) if strict else (key >= cand)
            acc = acc + jnp.where(hit, 1.0, 0.0)
        return acc
    acc = lax.fori_loop(0, nblk, body, jnp.zeros((rows, LANES), F32))
    return jnp.sum(acc, axis=1, keepdims=True)


def _kth_largest(s_ref, nblk, n_sel, shape):
    rows = shape[0]

    def body(_, carry):
        ans, bit = carry
        cand = ans + bit
        cnt = _count_ge(s_ref, nblk, cand, shape)
        return jnp.where(cnt >= n_sel, cand, ans), lax.shift_right_logical(bit, 1)

    ans, _ = lax.fori_loop(0, 32, body, (jnp.full((rows, 1), INT_MIN, I32), jnp.int32(INT_MIN)))
    return ans


def _upper_ones(n):
    r = lax.broadcasted_iota(I32, (n, n), 0)
    c = lax.broadcasted_iota(I32, (n, n), 1)
    return (r <= c).astype(BF16)


def _select_ties(key, kth, need, carry, upper):
    eqf = jnp.where(key == kth, 1.0, 0.0)
    rank = carry + _dot(eqf.astype(BF16), upper)
    sel = (key > kth) | ((key == kth) & (rank <= need))
    return sel, carry + jnp.sum(eqf, axis=1, keepdims=True)


def _dsa_seq_kernel(iq_ref, iw_ref, dq_ref, ik_ref, dk_ref, dv_ref, o_ref, s_ref, m_ref, l_ref, acc_ref,
                    *, n_heads, n_kv, n_idx, n_sel, tq, tk):
    i = pl.program_id(0)
    hd = HEAD_DIM
    grp = n_heads // n_kv
    scale = hd ** -0.5
    nblk = (i * tq + tq + tk - 1) // tk
    last = nblk - 1
    qpos = i * tq + lax.broadcasted_iota(I32, (tq, tk), 0)
    col = lax.broadcasted_iota(I32, (tq, tk), 1)
    iw = iw_ref[...] * (n_idx ** -0.5)

    def idx_scores(kb):
        ikb = ik_ref[kb][:, 0:hd]
        score = None
        for h in range(n_idx):
            sc = jnp.maximum(_dot_nt(iq_ref[:, h * hd:(h + 1) * hd], ikb) * scale, 0.0)
            term = iw[:, h:h + 1] * sc
            score = term if score is None else score + term
        return score

    def fill(kb, carry):
        s_ref[kb] = _sort_key(idx_scores(kb))
        return carry

    lax.fori_loop(0, last, fill, 0)
    causal_last = last * tk + col <= qpos
    s_ref[last] = _sort_key(jnp.where(causal_last, idx_scores(last), -jnp.inf))

    shape = (tq, tk)
    kth = _kth_largest(s_ref, nblk, n_sel, shape)
    n_ge = _count_ge(s_ref, nblk, kth, shape)
    n_gt = _count_ge(s_ref, nblk, kth, shape, strict=True)
    need = n_sel - n_gt
    ties = jnp.max(n_ge) > n_sel

    m_ref[...] = jnp.full(m_ref.shape, -jnp.inf, F32)
    l_ref[...] = jnp.zeros_like(l_ref)
    acc_ref[...] = jnp.zeros_like(acc_ref)

    def attend(kb, sel):
        dkb = dk_ref[kb]
        dvb = dv_ref[kb]
        for h in range(n_heads):
            kv = h // grp
            s = _dot_nt(dq_ref[:, h * hd:(h + 1) * hd], dkb[:, kv * hd:(kv + 1) * hd]) * scale
            s = jnp.where(sel, s, -jnp.inf)
            m_old = m_ref[h]
            m_new = jnp.maximum(m_old, jnp.max(s, axis=1, keepdims=True))
            m_safe = jnp.where(m_new == -jnp.inf, 0.0, m_new)
            alpha = jnp.exp(m_old - m_safe)
            p = jnp.exp(s - m_safe)
            l_ref[h] = alpha * l_ref[h] + jnp.sum(p, axis=1, keepdims=True)
            acc_ref[h] = alpha * acc_ref[h] + _dot(p.astype(BF16), dvb[:, kv * hd:(kv + 1) * hd])
            m_ref[h] = m_new

    @pl.when(jnp.logical_not(ties))
    def _():
        def body(kb, carry):
            attend(kb, s_ref[kb] >= kth)
            return carry
        lax.fori_loop(0, last, body, 0)
        attend(last, (s_ref[last] >= kth) & causal_last)

    @pl.when(ties)
    def _():
        upper = _upper_ones(tk)

        def body(kb, carry):
            sel, carry = _select_ties(s_ref[kb], kth, need, carry, upper)
            attend(kb, sel)
            return carry
        carry = lax.fori_loop(0, last, body, jnp.zeros((tq, 1), F32))
        sel, _ = _select_ties(s_ref[last], kth, need, carry, upper)
        attend(last, sel & causal_last)

    for h in range(n_heads):
        o_ref[:, h * hd:(h + 1) * hd] = acc_ref[h] / l_ref[h]


def _dsa_seq(iq_b, iw, dq_b, ik_b, dk_b, dv_b, n_heads, n_kv, n_idx, n_sel):
    t = iq_b.shape[0]
    tq = min(128, t)
    tk = min(256, t)
    nkb = t // tk
    kvw = n_kv * HEAD_DIM
    return pl.pallas_call(
        functools.partial(_dsa_seq_kernel, n_heads=n_heads, n_kv=n_kv, n_idx=n_idx, n_sel=n_sel, tq=tq, tk=tk),
        out_shape=jax.ShapeDtypeStruct((t, n_heads * HEAD_DIM), F32),
        grid=(t // tq,),
        in_specs=[pl.BlockSpec((tq, iq_b.shape[1]), lambda i: (i, 0)),
                  pl.BlockSpec((tq, LANES), lambda i: (i, 0)),
                  pl.BlockSpec((tq, dq_b.shape[1]), lambda i: (i, 0)),
                  _resident((nkb, tk, LANES)), _resident((nkb, tk, kvw)), _resident((nkb, tk, kvw))],
        out_specs=pl.BlockSpec((tq, n_heads * HEAD_DIM), lambda i: (i, 0)),
        scratch_shapes=[pltpu.VMEM((nkb, tq, tk), I32),
                        pltpu.VMEM((n_heads, tq, 1), F32), pltpu.VMEM((n_heads, tq, 1), F32),
                        pltpu.VMEM((n_heads, tq, HEAD_DIM), F32)],
        compiler_params=_cparams("parallel"),
        name="dsa_seq",
    )(iq_b, iw, dq_b, ik_b.reshape(nkb, tk, LANES), dk_b.reshape(nkb, tk, kvw), dv_b.reshape(nkb, tk, kvw))


def _page_specs(n, width, layer, n_pages, ch):
    def spec(p):
        return pl.BlockSpec((None, None, PAGE, width),
                            lambda b, c, pt: (layer, pt[b * n_pages + c * ch + p], 0, 0))
    return [spec(p) for p in range(n)]


def _paged_attn_kernel(pt_ref, *refs, ch, mode, has_mask, lam_init, n_heads, n_kv):
    pos = 0
    qm_ref, knew_ref, vnew_ref, selnew_ref = refs[0:4]
    pos = 4
    mask_ref = None
    if has_mask:
        mask_ref = refs[pos]
        pos += 1
    lam_ref = subg_ref = None
    if mode == "diff":
        lam_ref, subg_ref = refs[pos], refs[pos + 1]
        pos += 2
    k_refs = refs[pos:pos + ch]
    v_refs = refs[pos + ch:pos + 2 * ch]
    o_ref = refs[pos + 2 * ch]
    m_ref, l_ref, acc_ref = refs[pos + 2 * ch + 1:pos + 2 * ch + 4]
    c = pl.program_id(1)
    scale = HEAD_DIM ** -0.5

    @pl.when(c == 0)
    def _():
        m_ref[...] = jnp.full(m_ref.shape, -jnp.inf, F32)
        l_ref[...] = jnp.zeros_like(l_ref)
        acc_ref[...] = jnp.zeros_like(acc_ref)

    qm = qm_ref[0]
    qb = qm.astype(BF16)
    s = jnp.concatenate([_dot_nt(qb, k_refs[p][...].astype(BF16)) for p in range(ch)], axis=1) * scale
    if has_mask:
        s = jnp.where(mask_ref[0] > 0.0, s, -jnp.inf)
    m_old = m_ref[...]
    m_new = jnp.maximum(m_old, jnp.max(s, axis=1, keepdims=True))
    m_safe = jnp.where(m_new == -jnp.inf, 0.0, m_new)
    alpha = jnp.exp(m_old - m_safe)
    p = jnp.exp(s - m_safe).astype(BF16)
    l_ref[...] = alpha * l_ref[...] + jnp.sum(p.astype(F32), axis=1, keepdims=True)
    pv = _dot(p[:, 0:PAGE], v_refs[0][...].astype(BF16))
    for j in range(1, ch):
        pv = pv + _dot(p[:, j * PAGE:(j + 1) * PAGE], v_refs[j][...].astype(BF16))
    acc_ref[...] = alpha * acc_ref[...] + pv
    m_ref[...] = m_new

    @pl.when(c == pl.num_programs(1) - 1)
    def _():
        s_new = jnp.sum(qm * knew_ref[0], axis=1, keepdims=True) * scale
        s_new = jnp.where(selnew_ref[0][:, 0:1] > 0.0, s_new, -jnp.inf)
        m_o = m_ref[...]
        m_f = jnp.maximum(m_o, s_new)
        m_s = jnp.where(m_f == -jnp.inf, 0.0, m_f)
        al = jnp.exp(m_o - m_s)
        p_new = jnp.exp(s_new - m_s)
        l = al * l_ref[...] + p_new
        o = (al * acc_ref[...] + p_new * vnew_ref[0]) / l
        rows = o.shape[0]
        r = lax.broadcasted_iota(I32, (rows, 1), 0)
        if mode == "diff":
            vd = 2 * HEAD_DIM
            grp = n_heads // n_kv
            kv_of = (r % n_heads) // grp
            osel = o[:, 0:vd]
            for kv in range(1, n_kv):
                osel = jnp.where(kv_of == kv, o[:, kv * vd:(kv + 1) * vd], osel)
            lam = _lambda(lam_ref, lam_init)
            od = osel[0:n_heads, :] - lam * osel[n_heads:2 * n_heads, :]
            o_ref[0] = _subln(od, subg_ref[...], lam_init)
        else:
            grp = n_heads // n_kv
            kv_of = r // grp
            osel = o[:, 0:HEAD_DIM]
            for kv in range(1, n_kv):
                osel = jnp.where(kv_of == kv, o[:, kv * HEAD_DIM:(kv + 1) * HEAD_DIM], osel)
            o_ref[0] = osel


def _paged_attn(page_table, layer, qm, knew, vnew, selnew, mask, cache_k, cache_v, *, mode, n_heads, n_kv,
                lam=None, sub_g=None, lam_init=0.0):
    nb, n_pages = page_table.shape
    rows, dk = qm.shape[1:]
    dv = cache_v.shape[-1]
    ch = min(16, n_pages)
    nc = n_pages // ch
    has_mask = mask is not None
    per_b = lambda shape: pl.BlockSpec((1,) + shape, lambda b, c, pt: (b,) + (0,) * len(shape))
    in_specs = [per_b((rows, dk)), per_b((1, dk)), per_b((1, dv)), per_b((1, LANES))]
    args = [qm, knew, vnew, selnew]
    if has_mask:
        in_specs.append(pl.BlockSpec((1, 1, ch * PAGE), lambda b, c, pt: (b, 0, c)))
        args.append(mask)
    if mode == "diff":
        in_specs += [pl.BlockSpec(lam.shape, lambda b, c, pt: (0, 0)),
                     pl.BlockSpec((1, 2 * HEAD_DIM), lambda b, c, pt: (0, 0))]
        args += [lam, sub_g.reshape(1, -1)]
        out_rows, out_w = n_heads, 2 * HEAD_DIM
    else:
        out_rows, out_w = n_heads, HEAD_DIM
    in_specs += _page_specs(ch, dk, layer, n_pages, ch) + _page_specs(ch, dv, layer, n_pages, ch)
    args += [cache_k] * ch + [cache_v] * ch
    grid_spec = pltpu.PrefetchScalarGridSpec(
        num_scalar_prefetch=1, grid=(nb, nc), in_specs=in_specs,
        out_specs=pl.BlockSpec((1, out_rows, out_w), lambda b, c, pt: (b, 0, 0)),
        scratch_shapes=[pltpu.VMEM((rows, 1), F32), pltpu.VMEM((rows, 1), F32), pltpu.VMEM((rows, dv), F32)])
    return pl.pallas_call(
        functools.partial(_paged_attn_kernel, ch=ch, mode=mode, has_mask=has_mask, lam_init=lam_init,
                          n_heads=n_heads, n_kv=n_kv),
        out_shape=jax.ShapeDtypeStruct((nb, out_rows, out_w), F32),
        grid_spec=grid_spec,
        compiler_params=_cparams("parallel", "arbitrary"),
        name="paged_attn_" + mode,
    )(page_table.reshape(-1), *args)


def _paged_idx_kernel(pt_ref, iq_ref, w_ref, *refs, ch, n_idx):
    pages = refs[:ch]
    o_ref = refs[ch]
    scale = HEAD_DIM ** -0.5
    iq = iq_ref[0]
    w = w_ref[0] * (n_idx ** -0.5)
    outs = []
    for p in range(ch):
        sc = jnp.maximum(_dot_nt(iq, pages[p][...].astype(BF16)) * scale, 0.0)
        outs.append(jnp.sum(w * sc, axis=0, keepdims=True))
    o_ref[0] = jnp.concatenate(outs, axis=1)


def _paged_idx_scores(page_table, layer, iq, w, cache_idx, n_idx):
    nb, n_pages = page_table.shape
    ch = min(16, n_pages)
    nc = n_pages // ch
    grid_spec = pltpu.PrefetchScalarGridSpec(
        num_scalar_prefetch=1, grid=(nb, nc),
        in_specs=[pl.BlockSpec((1, n_idx, HEAD_DIM), lambda b, c, pt: (b, 0, 0)),
                  pl.BlockSpec((1, n_idx, 1), lambda b, c, pt: (b, 0, 0))]
        + _page_specs(ch, cache_idx.shape[-1], layer, n_pages, ch),
        out_specs=pl.BlockSpec((1, 1, ch * PAGE), lambda b, c, pt: (b, 0, c)))
    return pl.pallas_call(
        functools.partial(_paged_idx_kernel, ch=ch, n_idx=n_idx),
        out_shape=jax.ShapeDtypeStruct((nb, 1, n_pages * PAGE), F32),
        grid_spec=grid_spec,
        compiler_params=_cparams("parallel", "arbitrary"),
        name="paged_idx_scores",
    )(page_table.reshape(-1), iq, w, *([cache_idx] * ch))


def _topk_mask_kernel(sc_ref, iq_ref, ik_ref, w_ref, mask_ref, mnew_ref, s_ref, *, n_idx, n_sel, bw):
    nb, t = sc_ref.shape
    nblk = t // bw
    scale = HEAD_DIM ** -0.5
    prod = iq_ref[...].astype(F32) * ik_ref[...].astype(F32)
    seg_r = lax.broadcasted_iota(I32, (prod.shape[1], LANES), 0) // HEAD_DIM
    seg_c = lax.broadcasted_iota(I32, (prod.shape[1], LANES), 1)
    qk = _dot3(prod, (seg_r == seg_c).astype(BF16))
    s_new = jnp.sum(w_ref[...] * (n_idx ** -0.5) * jnp.maximum(qk * scale, 0.0), axis=1, keepdims=True)
    lane = lax.broadcasted_iota(I32, (nb, bw), 1)
    for kb in range(nblk):
        s_ref[kb] = _sort_key(sc_ref[:, kb * bw:(kb + 1) * bw])
    s_ref[nblk] = _sort_key(jnp.where(lane == 0, s_new, -jnp.inf))
    shape = (nb, bw)
    kth = _kth_largest(s_ref, nblk + 1, n_sel, shape)
    need = n_sel - _count_ge(s_ref, nblk + 1, kth, shape, strict=True)
    upper = _upper_ones(bw)
    carry = jnp.zeros((nb, 1), F32)
    for kb in range(nblk):
        sel, carry = _select_ties(s_ref[kb], kth, need, carry, upper)
        mask_ref[:, kb * bw:(kb + 1) * bw] = jnp.where(sel, 1.0, 0.0)
    sel, _ = _select_ties(s_ref[nblk], kth, need, carry, upper)
    mnew_ref[...] = jnp.where(sel & (lane == 0), 1.0, 0.0)[:, 0:LANES]


def _topk_mask(scores, iq_b, ik_new_b, w, n_idx, n_sel):
    nb, t = scores.shape
    bw = min(256, t)
    full = lambda a: pl.BlockSpec(a.shape, lambda i: (0,) * a.ndim)
    return pl.pallas_call(
        functools.partial(_topk_mask_kernel, n_idx=n_idx, n_sel=n_sel, bw=bw),
        out_shape=(jax.ShapeDtypeStruct((nb, t), F32), jax.ShapeDtypeStruct((nb, LANES), F32)),
        grid=(1,),
        in_specs=[full(scores), full(iq_b), full(ik_new_b), full(w)],
        out_specs=(pl.BlockSpec((nb, t), lambda i: (0, 0)), pl.BlockSpec((nb, LANES), lambda i: (0, 0))),
        scratch_shapes=[pltpu.VMEM((t // bw + 1, nb, bw), I32)],
        compiler_params=_cparams("arbitrary"),
        name="topk_mask",
    )(scores, iq_b, ik_new_b, w)


def _keys_on_lanes(k_b, n_slices, tk):
    t = k_b.shape[0]
    return k_b.reshape(t // tk, tk, n_slices, HEAD_DIM).transpose(2, 0, 3, 1)


def _online_softmax_step(s, m_ref, l_ref, acc_ref, idx, v_blk):
    ncol = s.shape[1] // LANES
    cols = lambda a: [a[:, c * LANES:(c + 1) * LANES] for c in range(ncol)]
    m_old = m_ref[idx]
    m_new = jnp.maximum(m_old, jnp.max(functools.reduce(jnp.maximum, cols(s)), axis=1, keepdims=True))
    m_safe = jnp.where(m_new == -jnp.inf, 0.0, m_new)
    alpha = jnp.exp(m_old - m_safe)
    p = jnp.exp(s - m_safe)
    l_ref[idx] = alpha * l_ref[idx] + jnp.sum(functools.reduce(jnp.add, cols(p)), axis=1, keepdims=True)
    acc_ref[idx] = alpha * acc_ref[idx] + _dot(p.astype(BF16), v_blk)
    m_ref[idx] = m_new


def _diff2_kernel(q_ref, kt_ref, v_ref, lam_ref, subg_ref, o_ref, qs_ref, m_ref, l_ref, acc_ref,
                  *, n_heads, n_kv, lam_init, tq, tk):
    i = pl.program_id(0)
    grp = n_heads // n_kv
    hd = HEAD_DIM
    vd = 2 * hd
    scale = hd ** -0.5
    m_ref[...] = jnp.full(m_ref.shape, -jnp.inf, F32)
    l_ref[...] = jnp.zeros_like(l_ref)
    acc_ref[...] = jnp.zeros_like(acc_ref)
    for kv in range(n_kv):
        for c in range(2):
            qs_ref[kv * 2 + c] = jnp.concatenate(
                [q_ref[:, ((kv * grp + g) * 2 + c) * hd:((kv * grp + g) * 2 + c + 1) * hd] for g in range(grp)],
                axis=0) * scale
    n_full = (i * tq) // tk

    def step(j, masked):
        vb = v_ref[j]
        if masked:
            qpos = i * tq + lax.broadcasted_iota(I32, (grp, tq, tk), 1)
            kpos = j * tk + lax.broadcasted_iota(I32, (grp, tq, tk), 2)
            ok = (kpos <= qpos).reshape(grp * tq, tk)
        for kv in range(n_kv):
            for c in range(2):
                idx = kv * 2 + c
                s = _dot(qs_ref[idx], kt_ref[idx, j])
                if masked:
                    s = jnp.where(ok, s, -jnp.inf)
                _online_softmax_step(s, m_ref, l_ref, acc_ref, idx, vb[:, kv * vd:(kv + 1) * vd])

    def body(j, carry):
        step(j, False)
        return carry

    lax.fori_loop(0, n_full, body, 0)
    step(n_full, True)
    lam = _lambda(lam_ref, lam_init)
    for kv in range(n_kv):
        o0 = acc_ref[kv * 2] / l_ref[kv * 2]
        o1 = acc_ref[kv * 2 + 1] / l_ref[kv * 2 + 1]
        o = _subln(o0 - lam * o1, subg_ref[...], lam_init)
        for g in range(grp):
            h = kv * grp + g
            o_ref[:, h * vd:(h + 1) * vd] = o[g * tq:(g + 1) * tq, :]


def _diff_seq2(q_b, k_b, v_b, c_lam, sub_g, lam_init, n_heads, n_kv):
    t = q_b.shape[0]
    tq = min(256, t)
    tk = min(512, t)
    nkb = t // tk
    vd = 2 * HEAD_DIM
    grp = n_heads // n_kv
    kt = _keys_on_lanes(k_b, 2 * n_kv, tk)
    return pl.pallas_call(
        functools.partial(_diff2_kernel, n_heads=n_heads, n_kv=n_kv, lam_init=lam_init, tq=tq, tk=tk),
        out_shape=jax.ShapeDtypeStruct((t, n_heads * vd), F32),
        grid=(t // tq,),
        in_specs=[pl.BlockSpec((tq, q_b.shape[1]), lambda i: (i, 0)),
                  _resident(kt.shape), _resident((nkb, tk, v_b.shape[1])),
                  pl.BlockSpec(c_lam.shape, lambda i: (0, 0)),
                  pl.BlockSpec((1, vd), lambda i: (0, 0))],
        out_specs=pl.BlockSpec((tq, n_heads * vd), lambda i: (i, 0)),
        scratch_shapes=[pltpu.VMEM((2 * n_kv, grp * tq, HEAD_DIM), BF16),
                        pltpu.VMEM((2 * n_kv, grp * tq, 1), F32), pltpu.VMEM((2 * n_kv, grp * tq, 1), F32),
                        pltpu.VMEM((2 * n_kv, grp * tq, vd), F32)],
        compiler_params=_cparams("parallel"),
        name="diff_seq",
    )(q_b, kt, v_b.reshape(nkb, tk, v_b.shape[1]), c_lam, sub_g.reshape(1, vd))


def _dsa2_kernel(iq_ref, iw_ref, dq_ref, ikt_ref, dkt_ref, dv_ref, o_ref,
                 s_ref, iqs_ref, wrep_ref, qs_ref, m_ref, l_ref, acc_ref,
                 *, n_heads, n_kv, n_idx, n_sel, tq, tk):
    i = pl.program_id(0)
    hd = HEAD_DIM
    grp = n_heads // n_kv
    scale = hd ** -0.5
    nblk = (i * tq + tq + tk - 1) // tk
    last = nblk - 1
    ncol = tk // LANES

    iw = iw_ref[...] * ((n_idx ** -0.5) * scale)
    for h in range(n_idx):
        iqs_ref[h] = iq_ref[:, h * hd:(h + 1) * hd]
        wrep_ref[h] = jnp.broadcast_to(iw[:, h:h + 1], (tq, LANES))
    for kv in range(n_kv):
        qs_ref[kv] = jnp.concatenate(
            [dq_ref[:, (kv * grp + g) * hd:(kv * grp + g + 1) * hd] for g in range(grp)], axis=0) * scale

    def idx_scores(kb):
        ikb = ikt_ref[kb]
        score = None
        for h in range(n_idx):
            w = jnp.concatenate([wrep_ref[h]] * ncol, axis=1)
            term = jnp.maximum(_dot(iqs_ref[h], ikb), 0.0) * w
            score = term if score is None else score + term
        return score

    def fill(kb, carry):
        s_ref[kb] = _sort_key(idx_scores(kb))
        return carry

    lax.fori_loop(0, last, fill, 0)
    qpos = i * tq + lax.broadcasted_iota(I32, (tq, tk), 0)
    causal_last = last * tk + lax.broadcasted_iota(I32, (tq, tk), 1) <= qpos
    s_ref[last] = _sort_key(jnp.where(causal_last, idx_scores(last), -jnp.inf))

    shape = (tq, tk)

    def bis_cond(c):
        it, _, _, _, done = c
        return jnp.logical_and(it < 32, jnp.logical_not(done))

    def bis_body(c):
        it, ans, bit, cnt_ans, _ = c
        cand = ans + bit
        cnt = _count_ge(s_ref, nblk, cand, shape)
        take = cnt >= n_sel
        ans = jnp.where(take, cand, ans)
        cnt_ans = jnp.where(take, cnt, cnt_ans)
        done = jnp.min(jnp.where(cnt_ans == n_sel, 1.0, 0.0)) > 0.5
        return it + 1, ans, lax.shift_right_logical(bit, 1), cnt_ans, done

    init = (jnp.int32(0), jnp.full((tq, 1), INT_MIN, I32), jnp.int32(INT_MIN),
            jnp.full((tq, 1), 1.0, F32) * (nblk * tk).astype(F32), jnp.bool_(False))
    _, kth, _, n_ge, _ = lax.while_loop(bis_cond, bis_body, init)
    ties = jnp.max(n_ge) > n_sel

    m_ref[...] = jnp.full(m_ref.shape, -jnp.inf, F32)
    l_ref[...] = jnp.zeros_like(l_ref)
    acc_ref[...] = jnp.zeros_like(acc_ref)

    def attend(kb, sel):
        dvb = dv_ref[kb]
        for kv in range(n_kv):
            s = _dot(qs_ref[kv], dkt_ref[kv, kb]).reshape(grp, tq, tk)
            s = jnp.where(sel[None], s, -jnp.inf).reshape(grp * tq, tk)
            _online_softmax_step(s, m_ref, l_ref, acc_ref, kv, dvb)

    @pl.when(jnp.logical_not(ties))
    def _():
        def body(kb, carry):
            attend(kb, s_ref[kb] >= kth)
            return carry
        lax.fori_loop(0, last, body, 0)
        attend(last, (s_ref[last] >= kth) & causal_last)

    @pl.when(ties)
    def _():
        need = n_sel - _count_ge(s_ref, nblk, kth, shape, strict=True)
        upper = _upper_ones(tk)

        def body(kb, carry):
            sel, carry = _select_ties(s_ref[kb], kth, need, carry, upper)
            attend(kb, sel)
            return carry
        carry = lax.fori_loop(0, last, body, jnp.zeros((tq, 1), F32))
        sel, _ = _select_ties(s_ref[last], kth, need, carry, upper)
        attend(last, sel & causal_last)

    for kv in range(n_kv):
        o = acc_ref[kv] / l_ref[kv]
        for g in range(grp):
            h = kv * grp + g
            o_ref[:, h * hd:(h + 1) * hd] = o[g * tq:(g + 1) * tq, kv * hd:(kv + 1) * hd]


def _dsa_seq2(iq_b, iw, dq_b, ik_b, dk_b, dv_b, n_heads, n_kv, n_idx, n_sel):
    t = iq_b.shape[0]
    tq = min(128, t)
    tk = min(512, t)
    nkb = t // tk
    kvw = n_kv * HEAD_DIM
    grp = n_heads // n_kv
    ikt = _keys_on_lanes(ik_b[:, :HEAD_DIM], 1, tk)[0]
    dkt = _keys_on_lanes(dk_b, n_kv, tk)
    return pl.pallas_call(
        functools.partial(_dsa2_kernel, n_heads=n_heads, n_kv=n_kv, n_idx=n_idx, n_sel=n_sel, tq=tq, tk=tk),
        out_shape=jax.ShapeDtypeStruct((t, n_heads * HEAD_DIM), F32),
        grid=(t // tq,),
        in_specs=[pl.BlockSpec((tq, iq_b.shape[1]), lambda i: (i, 0)),
                  pl.BlockSpec((tq, LANES), lambda i: (i, 0)),
                  pl.BlockSpec((tq, dq_b.shape[1]), lambda i: (i, 0)),
                  _resident(ikt.shape), _resident(dkt.shape), _resident((nkb, tk, kvw))],
        out_specs=pl.BlockSpec((tq, n_heads * HEAD_DIM), lambda i: (i, 0)),
        scratch_shapes=[pltpu.VMEM((nkb, tq, tk), I32),
                        pltpu.VMEM((n_idx, tq, HEAD_DIM), BF16), pltpu.VMEM((n_idx, tq, LANES), F32),
                        pltpu.VMEM((n_kv, grp * tq, HEAD_DIM), BF16),
                        pltpu.VMEM((n_kv, grp * tq, 1), F32), pltpu.VMEM((n_kv, grp * tq, 1), F32),
                        pltpu.VMEM((n_kv, grp * tq, kvw), F32)],
        compiler_params=_cparams("parallel"),
        name="dsa_seq",
    )(iq_b, iw, dq_b, ikt, dkt, dv_b.reshape(nkb, tk, kvw))


LOG2E = 1.4426950408889634


def _flash_t(k_blk, qs_ref, bias, m_ref, l_ref, acc_ref, idx, v_t):
    s_t = _dot(k_blk, qs_ref[idx])
    if bias is not None:
        s_t = s_t + jnp.concatenate([bias] * (s_t.shape[1] // bias.shape[1]), axis=1)
    m_old = m_ref[idx]
    m_new = jnp.maximum(m_old, jnp.max(s_t, axis=0, keepdims=True))
    m_safe = jnp.where(m_new == -jnp.inf, 0.0, m_new)
    alpha = jnp.exp2(m_old - m_safe)
    p = jnp.exp2(s_t - m_safe)
    l_ref[idx] = alpha * l_ref[idx] + jnp.sum(p, axis=0, keepdims=True)
    acc_ref[idx] = alpha * acc_ref[idx] + _dot(v_t, p.astype(BF16))
    m_ref[idx] = m_new


def _diff_t_kernel(qt_ref, k_ref, vt_ref, lam_ref, subg_ref, o_ref, qs_ref, m_ref, l_ref, acc_ref,
                   *, n_heads, n_kv, lam_init, tq, tk):
    i = pl.program_id(0)
    grp = n_heads // n_kv
    hd = HEAD_DIM
    vd = 2 * hd
    nq = grp * tq
    m_ref[...] = jnp.full(m_ref.shape, -jnp.inf, F32)
    l_ref[...] = jnp.zeros_like(l_ref)
    acc_ref[...] = jnp.zeros_like(acc_ref)
    qs_ref[...] = jnp.zeros_like(qs_ref)
    for idx in range(2 * n_kv):
        qs_ref[idx, idx * hd:(idx + 1) * hd, :] = (qt_ref[idx] * (hd ** -0.5 * LOG2E)).astype(BF16)
    n_full = (i * tq) // tk

    def step(j, masked):
        kb = k_ref[j]
        vtb = vt_ref[j]
        bias = None
        if masked:
            kpos = j * tk + lax.broadcasted_iota(I32, (tk, tq), 0)
            qpos = i * tq + lax.broadcasted_iota(I32, (tk, tq), 1)
            bias = jnp.where(kpos <= qpos, 0.0, -jnp.inf)
        for kv in range(n_kv):
            for c in range(2):
                idx = kv * 2 + c
                _flash_t(kb, qs_ref, bias, m_ref, l_ref, acc_ref, idx, vtb[kv * vd:(kv + 1) * vd, :])

    def body(j, carry):
        step(j, False)
        return carry

    lax.fori_loop(0, n_full, body, 0)
    step(n_full, True)
    lam = _lambda(lam_ref, lam_init)
    for kv in range(n_kv):
        o0 = acc_ref[kv * 2] / l_ref[kv * 2]
        o1 = acc_ref[kv * 2 + 1] / l_ref[kv * 2 + 1]
        o = o0 - lam * o1
        o = o * lax.rsqrt(jnp.mean(o * o, axis=0, keepdims=True) + EPS) * subg_ref[...] * (1.0 - lam_init)
        for g in range(grp):
            h = kv * grp + g
            o_ref[h * vd:(h + 1) * vd, :] = o[:, g * tq:(g + 1) * tq]


def _diff_seq_t(q_f, k_b, v_b, c_lam, sub_g, lam_init, n_heads, n_kv):
    t = q_f.shape[0]
    tq = min(256, t)
    tk = min(1024, t)
    nqb, nkb = t // tq, t // tk
    hd = HEAD_DIM
    vd = 2 * hd
    grp = n_heads // n_kv
    kw = k_b.shape[1]
    qt = q_f.reshape(nqb, tq, n_kv, grp, 2, hd).transpose(2, 4, 0, 5, 3, 1).reshape(2 * n_kv, nqb, hd, grp * tq)
    vt = v_b.reshape(nkb, tk, v_b.shape[1]).transpose(0, 2, 1)
    out_t = pl.pallas_call(
        functools.partial(_diff_t_kernel, n_heads=n_heads, n_kv=n_kv, lam_init=lam_init, tq=tq, tk=tk),
        out_shape=jax.ShapeDtypeStruct((nqb, n_heads * vd, tq), F32),
        grid=(nqb,),
        in_specs=[pl.BlockSpec((2 * n_kv, None, hd, grp * tq), lambda i: (0, i, 0, 0)),
                  _resident((nkb, tk, kw)), _resident(vt.shape),
                  pl.BlockSpec(c_lam.shape, lambda i: (0, 0)),
                  pl.BlockSpec((vd, 1), lambda i: (0, 0))],
        out_specs=pl.BlockSpec((None, n_heads * vd, tq), lambda i: (i, 0, 0)),
        scratch_shapes=[pltpu.VMEM((2 * n_kv, kw, grp * tq), BF16),
                        pltpu.VMEM((2 * n_kv, 1, grp * tq), F32), pltpu.VMEM((2 * n_kv, 1, grp * tq), F32),
                        pltpu.VMEM((2 * n_kv, vd, grp * tq), F32)],
        compiler_params=_cparams("parallel"),
        name="diff_seq",
    )(qt, k_b.reshape(nkb, tk, kw), vt, c_lam, sub_g.reshape(vd, 1))
    return out_t.transpose(0, 2, 1).reshape(t, n_heads * vd)


def _count_ge_t(s_ref, nblk, cand, tk, tq, strict=False):
    rows = 8 * SUBLANES

    def body(kb, acc):
        key = s_ref[kb]
        hit = (key > cand) if strict else (key >= cand)
        return acc + jnp.sum(jnp.where(hit, 1.0, 0.0).reshape(tk // rows, rows, tq), axis=0)
    acc = lax.fori_loop(0, nblk, body, jnp.zeros((rows, tq), F32))
    return jnp.sum(acc, axis=0, keepdims=True)


def _select_ties_t(key, kth, need, carry, lower):
    eqf = jnp.where(key == kth, 1.0, 0.0)
    rank = carry + _dot(lower, eqf.astype(BF16))
    sel = (key > kth) | ((key == kth) & (rank <= need))
    return sel, carry + jnp.sum(eqf, axis=0, keepdims=True)


def _dsa_t_kernel(iqt_ref, iwt_ref, dqt_ref, ik_ref, dk_ref, dvt_ref, o_ref,
                  s_ref, iqs_ref, qs_ref, m_ref, l_ref, acc_ref,
                  *, n_heads, n_kv, n_idx, n_sel, tq, tk):
    i = pl.program_id(0)
    hd = HEAD_DIM
    grp = n_heads // n_kv
    nblk = (i * tq + tq + tk - 1) // tk
    last = nblk - 1

    iqs_ref[...] = jnp.zeros_like(iqs_ref)
    qs_ref[...] = jnp.zeros_like(qs_ref)
    for hp in range(n_idx // 2):
        for j in range(2):
            iqs_ref[hp, 0:hd, j * tq:(j + 1) * tq] = iqt_ref[2 * hp + j]
    for kv in range(n_kv):
        qs_ref[kv, kv * hd:(kv + 1) * hd, :] = (dqt_ref[kv] * (hd ** -0.5 * LOG2E)).astype(BF16)
    iw = iwt_ref[...] * ((n_idx ** -0.5) * (hd ** -0.5))

    def idx_scores(kb):
        ikb = ik_ref[kb]
        score = None
        for hp in range(n_idx // 2):
            sc = jnp.maximum(_dot(ikb, iqs_ref[hp]), 0.0)
            for j in range(2):
                h = 2 * hp + j
                term = sc[:, j * tq:(j + 1) * tq] * iw[h:h + 1, :]
                score = term if score is None else score + term
        return score

    def fill(kb, carry):
        s_ref[kb] = _sort_key(idx_scores(kb))
        return carry

    lax.fori_loop(0, last, fill, 0)
    kpos = last * tk + lax.broadcasted_iota(I32, (tk, tq), 0)
    causal_last = kpos <= i * tq + lax.broadcasted_iota(I32, (tk, tq), 1)
    s_ref[last] = _sort_key(jnp.where(causal_last, idx_scores(last), -jnp.inf))

    def bis_cond(c):
        it, _, _, _, done = c
        return jnp.logical_and(it < 32, jnp.logical_not(done))

    def bis_body(c):
        it, ans, bit, cnt_ans, _ = c
        cand = ans + bit
        cnt = _count_ge_t(s_ref, nblk, cand, tk, tq)
        take = cnt >= n_sel
        ans = jnp.where(take, cand, ans)
        cnt_ans = jnp.where(take, cnt, cnt_ans)
        done = jnp.min(jnp.where(cnt_ans == n_sel, 1.0, 0.0)) > 0.5
        return it + 1, ans, lax.shift_right_logical(bit, 1), cnt_ans, done

    init = (jnp.int32(0), jnp.full((1, tq), INT_MIN, I32), jnp.int32(INT_MIN),
            jnp.full((1, tq), 1.0, F32) * (nblk * tk).astype(F32), jnp.bool_(False))
    _, kth, _, n_ge, _ = lax.while_loop(bis_cond, bis_body, init)
    ties = jnp.max(n_ge) > n_sel

    m_ref[...] = jnp.full(m_ref.shape, -jnp.inf, F32)
    l_ref[...] = jnp.zeros_like(l_ref)
    acc_ref[...] = jnp.zeros_like(acc_ref)

    def attend(kb, sel):
        dkb = dk_ref[kb]
        dvtb = dvt_ref[kb]
        bias = jnp.where(sel, 0.0, -jnp.inf)
        for kv in range(n_kv):
            _flash_t(dkb, qs_ref, bias, m_ref, l_ref, acc_ref, kv, dvtb[kv * hd:(kv + 1) * hd, :])

    @pl.when(jnp.logical_not(ties))
    def _():
        def body(kb, carry):
            attend(kb, s_ref[kb] >= kth)
            return carry
        lax.fori_loop(0, last, body, 0)
        attend(last, (s_ref[last] >= kth) & causal_last)

    @pl.when(ties)
    def _():
        need = n_sel - _count_ge_t(s_ref, nblk, kth, tk, tq, strict=True)
        r = lax.broadcasted_iota(I32, (tk, tk), 0)
        cidx = lax.broadcasted_iota(I32, (tk, tk), 1)
        lower = (cidx <= r).astype(BF16)

        def body(kb, carry):
            sel, carry = _select_ties_t(s_ref[kb], kth, need, carry, lower)
            attend(kb, sel)
            return carry
        carry = lax.fori_loop(0, last, body, jnp.zeros((1, tq), F32))
        sel, _ = _select_ties_t(s_ref[last], kth, need, carry, lower)
        attend(last, sel & causal_last)

    for kv in range(n_kv):
        o = acc_ref[kv] / l_ref[kv]
        for g in range(grp):
            h = kv * grp + g
            o_ref[h * hd:(h + 1) * hd, :] = o[:, g * tq:(g + 1) * tq]


def _dsa_seq_t(iq_b, iw, dq_f, ik_b, dk_b, dv_b, n_heads, n_kv, n_idx, n_sel):
    t = iq_b.shape[0]
    tq = min(256, t)
    tk = min(1024, t)
    nqb, nkb = t // tq, t // tk
    hd = HEAD_DIM
    kvw = n_kv * hd
    grp = n_heads // n_kv
    assert n_idx % 2 == 0 and kvw == LANES
    iqt = iq_b.reshape(nqb, tq, n_idx, hd).transpose(2, 0, 3, 1)
    iwt = iw[:, :n_idx].reshape(nqb, tq, n_idx).transpose(0, 2, 1)
    dqt = dq_f.reshape(nqb, tq, n_kv, grp, hd).transpose(2, 0, 4, 3, 1).reshape(n_kv, nqb, hd, grp * tq)
    dvt = dv_b.reshape(nkb, tk, kvw).transpose(0, 2, 1)
    out_t = pl.pallas_call(
        functools.partial(_dsa_t_kernel, n_heads=n_heads, n_kv=n_kv, n_idx=n_idx, n_sel=n_sel, tq=tq, tk=tk),
        out_shape=jax.ShapeDtypeStruct((nqb, n_heads * hd, tq), F32),
        grid=(nqb,),
        in_specs=[pl.BlockSpec((n_idx, None, hd, tq), lambda i: (0, i, 0, 0)),
                  pl.BlockSpec((None, n_idx, tq), lambda i: (i, 0, 0)),
                  pl.BlockSpec((n_kv, None, hd, grp * tq), lambda i: (0, i, 0, 0)),
                  _resident((nkb, tk, LANES)), _resident((nkb, tk, kvw)), _resident(dvt.shape)],
        out_specs=pl.BlockSpec((None, n_heads * hd, tq), lambda i: (i, 0, 0)),
        scratch_shapes=[pltpu.VMEM((nkb, tk, tq), I32),
                        pltpu.VMEM((n_idx // 2, LANES, 2 * tq), BF16),
                        pltpu.VMEM((n_kv, kvw, grp * tq), BF16),
                        pltpu.VMEM((n_kv, 1, grp * tq), F32), pltpu.VMEM((n_kv, 1, grp * tq), F32),
                        pltpu.VMEM((n_kv, hd, grp * tq), F32)],
        compiler_params=_cparams("parallel"),
        name="dsa_seq",
    )(iqt, iwt, dqt, ik_b.reshape(nkb, tk, LANES), dk_b.reshape(nkb, tk, kvw), dvt)
    return out_t.transpose(0, 2, 1).reshape(t, n_heads * hd)


def _page_specs2(n, blk, layer, n_pages, ch):
    def spec(p):
        return pl.BlockSpec((None, None) + blk,
                            lambda b, c, pt: (layer, pt[b * n_pages + c * ch + p]) + (0,) * len(blk))
    return [spec(p) for p in range(n)]


def _paged2_kernel(pt_ref, *refs, ch, mode, has_mask, lam_init, n_heads, n_kv):
    qm_ref, knew_ref, vnew_ref, selnew_ref = refs[0:4]
    pos = 4
    mask_ref = None
    if has_mask:
        mask_ref = refs[pos]
        pos += 1
    lam_ref = subg_ref = None
    if mode == "diff":
        lam_ref, subg_ref = refs[pos], refs[pos + 1]
        pos += 2
    k_refs = refs[pos:pos + ch]
    v_refs = refs[pos + ch:pos + 2 * ch]
    o_ref = refs[pos + 2 * ch]
    m_ref, l_ref, acc_ref = refs[pos + 2 * ch + 1:pos + 2 * ch + 4]
    c = pl.program_id(1)
    scale = HEAD_DIM ** -0.5
    dk = qm_ref.shape[2]

    @pl.when(c == 0)
    def _():
        m_ref[...] = jnp.full(m_ref.shape, -jnp.inf, F32)
        l_ref[...] = jnp.zeros_like(l_ref)
        acc_ref[...] = jnp.zeros_like(acc_ref)

    qm = qm_ref[0]
    qb = qm.astype(BF16)
    s = jnp.concatenate([_dot(qb, k_refs[p][...].reshape(dk, PAGE).astype(BF16)) for p in range(ch)],
                        axis=1) * scale
    if has_mask:
        s = jnp.where(mask_ref[0] > 0.0, s, -jnp.inf)
    m_old = m_ref[...]
    m_new = jnp.maximum(m_old, jnp.max(s, axis=1, keepdims=True))
    m_safe = jnp.where(m_new == -jnp.inf, 0.0, m_new)
    alpha = jnp.exp(m_old - m_safe)
    p = jnp.exp(s - m_safe).astype(BF16)
    l_ref[...] = alpha * l_ref[...] + jnp.sum(p.astype(F32), axis=1, keepdims=True)

    def pv_of(j):
        pj = p[:, j * PAGE:(j + 1) * PAGE]
        if mode == "diff":
            return jnp.concatenate(
                [_dot(pj, v_refs[j][pl.ds(kv, PAGE, stride=n_kv), :].astype(BF16)) for kv in range(n_kv)], axis=1)
        return _dot_nt(pj, v_refs[j][...].reshape(-1, PAGE).astype(BF16))

    pv = pv_of(0)
    for j in range(1, ch):
        pv = pv + pv_of(j)
    acc_ref[...] = alpha * acc_ref[...] + pv
    m_ref[...] = m_new

    @pl.when(c == pl.num_programs(1) - 1)
    def _():
        s_new = jnp.sum(qm * knew_ref[0], axis=1, keepdims=True) * scale
        s_new = jnp.where(selnew_ref[0][:, 0:1] > 0.0, s_new, -jnp.inf)
        m_o = m_ref[...]
        m_f = jnp.maximum(m_o, s_new)
        m_s = jnp.where(m_f == -jnp.inf, 0.0, m_f)
        al = jnp.exp(m_o - m_s)
        p_new = jnp.exp(s_new - m_s)
        l = al * l_ref[...] + p_new
        o = (al * acc_ref[...] + p_new * vnew_ref[0]) / l
        rows = o.shape[0]
        r = lax.broadcasted_iota(I32, (rows, 1), 0)
        grp = n_heads // n_kv
        if mode == "diff":
            vd = 2 * HEAD_DIM
            kv_of = (r % n_heads) // grp
            osel = o[:, 0:vd]
            for kv in range(1, n_kv):
                osel = jnp.where(kv_of == kv, o[:, kv * vd:(kv + 1) * vd], osel)
            lam = _lambda(lam_ref, lam_init)
            od = osel[0:n_heads, :] - lam * osel[n_heads:2 * n_heads, :]
            o_ref[0] = _subln(od, subg_ref[...], lam_init)
        else:
            kv_of = r // grp
            osel = o[:, 0:HEAD_DIM]
            for kv in range(1, n_kv):
                osel = jnp.where(kv_of == kv, o[:, kv * HEAD_DIM:(kv + 1) * HEAD_DIM], osel)
            o_ref[0] = osel


def _paged_attn2(page_table, layer, qm, knew, vnew, selnew, mask, cache_kt, cache_v, *, mode, n_heads, n_kv,
                 lam=None, sub_g=None, lam_init=0.0):
    nb, n_pages = page_table.shape
    rows, dk = qm.shape[1:]
    dv = vnew.shape[-1]
    ch = min(16, n_pages)
    nc = n_pages // ch
    has_mask = mask is not None
    per_b = lambda shape: pl.BlockSpec((1,) + shape, lambda b, c, pt: (b,) + (0,) * len(shape))
    in_specs = [per_b((rows, dk)), per_b((1, dk)), per_b((1, dv)), per_b((1, LANES))]
    args = [qm, knew, vnew, selnew]
    if has_mask:
        in_specs.append(pl.BlockSpec((1, 1, ch * PAGE), lambda b, c, pt: (b, 0, c)))
        args.append(mask)
    if mode == "diff":
        in_specs += [pl.BlockSpec(lam.shape, lambda b, c, pt: (0, 0)),
                     pl.BlockSpec((1, 2 * HEAD_DIM), lambda b, c, pt: (0, 0))]
        args += [lam, sub_g.reshape(1, -1)]
        out_w = 2 * HEAD_DIM
    else:
        out_w = HEAD_DIM
    in_specs += (_page_specs2(ch, cache_kt.shape[2:], layer, n_pages, ch)
                 + _page_specs2(ch, cache_v.shape[2:], layer, n_pages, ch))
    args += [cache_kt] * ch + [cache_v] * ch
    grid_spec = pltpu.PrefetchScalarGridSpec(
        num_scalar_prefetch=1, grid=(nb, nc), in_specs=in_specs,
        out_specs=pl.BlockSpec((1, n_heads, out_w), lambda b, c, pt: (b, 0, 0)),
        scratch_shapes=[pltpu.VMEM((rows, 1), F32), pltpu.VMEM((rows, 1), F32), pltpu.VMEM((rows, dv), F32)])
    return pl.pallas_call(
        functools.partial(_paged2_kernel, ch=ch, mode=mode, has_mask=has_mask, lam_init=lam_init,
                          n_heads=n_heads, n_kv=n_kv),
        out_shape=jax.ShapeDtypeStruct((nb, n_heads, out_w), F32),
        grid_spec=grid_spec,
        compiler_params=_cparams("parallel", "arbitrary"),
        name="paged_attn_" + mode,
    )(page_table.reshape(-1), *args)


def _paged_idx2_kernel(pt_ref, iq_ref, w_ref, *refs, ch, n_idx):
    pages = refs[:ch]
    o_ref = refs[ch]
    scale = HEAD_DIM ** -0.5
    iq = iq_ref[0]
    w = w_ref[0] * (n_idx ** -0.5)
    outs = []
    for p in range(ch):
        sc = jnp.maximum(_dot(iq, pages[p][...].astype(BF16)) * scale, 0.0)
        outs.append(jnp.sum(w * sc, axis=0, keepdims=True))
    o_ref[0] = jnp.concatenate(outs, axis=1)


def _paged_idx_scores2(page_table, layer, iq, w, cache_it, n_idx):
    nb, n_pages = page_table.shape
    ch = min(16, n_pages)
    nc = n_pages // ch
    grid_spec = pltpu.PrefetchScalarGridSpec(
        num_scalar_prefetch=1, grid=(nb, nc),
        in_specs=[pl.BlockSpec((1, n_idx, HEAD_DIM), lambda b, c, pt: (b, 0, 0)),
                  pl.BlockSpec((1, n_idx, 1), lambda b, c, pt: (b, 0, 0))]
        + _page_specs2(ch, cache_it.shape[2:], layer, n_pages, ch),
        out_specs=pl.BlockSpec((1, 1, ch * PAGE), lambda b, c, pt: (b, 0, c)))
    return pl.pallas_call(
        functools.partial(_paged_idx2_kernel, ch=ch, n_idx=n_idx),
        out_shape=jax.ShapeDtypeStruct((nb, 1, n_pages * PAGE), F32),
        grid_spec=grid_spec,
        compiler_params=_cparams("parallel", "arbitrary"),
        name="paged_idx_scores",
    )(page_table.reshape(-1), iq, w, *([cache_it] * ch))


def _page_copies(pt_ref, step, slot, caches, bufs, sem, *, nc, n_pages, ch, layer):
    b = step // nc
    c = step % nc
    out = []
    for a, (cache, buf) in enumerate(zip(caches, bufs)):
        for p in range(ch):
            page = pt_ref[b * n_pages + c * ch + p]
            out.append(pltpu.make_async_copy(cache.at[layer, page], buf.at[slot, p], sem.at[a, slot]))
    return out


PAGE_RING = 3


def _pipelined_pages(pt_ref, caches, bufs, sem, *, nc, n_pages, ch, layer):
    step = pl.program_id(0) * nc + pl.program_id(1)
    total = pl.num_programs(0) * nc
    ahead = PAGE_RING - 1
    kw = dict(nc=nc, n_pages=n_pages, ch=ch, layer=layer)

    def start(s):
        for cp in _page_copies(pt_ref, s, s % PAGE_RING, caches, bufs, sem, **kw):
            cp.start()

    for k in range(ahead):
        @pl.when(jnp.logical_and(step == 0, k < total))
        def _(k=k):
            start(step + k)

    @pl.when(step + ahead < total)
    def _():
        start(step + ahead)

    slot = step % PAGE_RING
    for cp in _page_copies(pt_ref, step, slot, caches, bufs, sem, **kw):
        cp.wait()
    return slot


def _paged3_kernel(pt_ref, *refs, ch, nc, n_pages, layer, mode, has_mask, lam_init, n_heads, n_kv):
    qm_ref, knew_ref, vnew_ref, selnew_ref = refs[0:4]
    pos = 4
    mask_ref = None
    if has_mask:
        mask_ref = refs[pos]
        pos += 1
    lam_ref = subg_ref = None
    if mode == "diff":
        lam_ref, subg_ref = refs[pos], refs[pos + 1]
        pos += 2
    ck_ref, cv_ref, o_ref, kbuf, vbuf, sem, m_ref, l_ref, acc_ref = refs[pos:pos + 9]
    c = pl.program_id(1)
    scale = HEAD_DIM ** -0.5
    dk = qm_ref.shape[2]
    slot = _pipelined_pages(pt_ref, (ck_ref, cv_ref), (kbuf, vbuf), sem, nc=nc, n_pages=n_pages, ch=ch, layer=layer)

    @pl.when(c == 0)
    def _():
        m_ref[...] = jnp.full(m_ref.shape, -jnp.inf, F32)
        l_ref[...] = jnp.zeros_like(l_ref)
        acc_ref[...] = jnp.zeros_like(acc_ref)

    qm = qm_ref[0]
    qb = qm.astype(BF16)
    s = jnp.concatenate([_dot(qb, kbuf[slot, p].reshape(dk, PAGE).astype(BF16)) for p in range(ch)],
                        axis=1) * scale
    if has_mask:
        s = jnp.where(mask_ref[0] > 0.0, s, -jnp.inf)
    m_old = m_ref[...]
    m_new = jnp.maximum(m_old, jnp.max(s, axis=1, keepdims=True))
    m_safe = jnp.where(m_new == -jnp.inf, 0.0, m_new)
    alpha = jnp.exp(m_old - m_safe)
    p = jnp.exp(s - m_safe).astype(BF16)
    l_ref[...] = alpha * l_ref[...] + jnp.sum(p.astype(F32), axis=1, keepdims=True)

    def pv_of(j):
        pj = p[:, j * PAGE:(j + 1) * PAGE]
        if mode == "diff":
            return jnp.concatenate(
                [_dot(pj, vbuf[slot, j, pl.ds(kv, PAGE, stride=n_kv), :].astype(BF16)) for kv in range(n_kv)],
                axis=1)
        return _dot_nt(pj, vbuf[slot, j].reshape(-1, PAGE).astype(BF16))

    pv = pv_of(0)
    for j in range(1, ch):
        pv = pv + pv_of(j)
    acc_ref[...] = alpha * acc_ref[...] + pv
    m_ref[...] = m_new

    @pl.when(c == nc - 1)
    def _():
        s_new = jnp.sum(qm * knew_ref[0], axis=1, keepdims=True) * scale
        s_new = jnp.where(selnew_ref[0][:, 0:1] > 0.0, s_new, -jnp.inf)
        m_o = m_ref[...]
        m_f = jnp.maximum(m_o, s_new)
        m_s = jnp.where(m_f == -jnp.inf, 0.0, m_f)
        al = jnp.exp(m_o - m_s)
        p_new = jnp.exp(s_new - m_s)
        l = al * l_ref[...] + p_new
        o = (al * acc_ref[...] + p_new * vnew_ref[0]) / l
        rows = o.shape[0]
        r = lax.broadcasted_iota(I32, (rows, 1), 0)
        grp = n_heads // n_kv
        if mode == "diff":
            vd = 2 * HEAD_DIM
            kv_of = (r % n_heads) // grp
            osel = o[:, 0:vd]
            for kv in range(1, n_kv):
                osel = jnp.where(kv_of == kv, o[:, kv * vd:(kv + 1) * vd], osel)
            lam = _lambda(lam_ref, lam_init)
            od = osel[0:n_heads, :] - lam * osel[n_heads:2 * n_heads, :]
            o_ref[0] = _subln(od, subg_ref[...], lam_init)
        else:
            kv_of = r // grp
            osel = o[:, 0:HEAD_DIM]
            for kv in range(1, n_kv):
                osel = jnp.where(kv_of == kv, o[:, kv * HEAD_DIM:(kv + 1) * HEAD_DIM], osel)
            o_ref[0] = osel


def _paged_attn3(page_table, layer, qm, knew, vnew, selnew, mask, cache_kt, cache_v, *, mode, n_heads, n_kv,
                 lam=None, sub_g=None, lam_init=0.0):
    nb, n_pages = page_table.shape
    rows, dk = qm.shape[1:]
    dv = vnew.shape[-1]
    ch = min(16, n_pages)
    nc = n_pages // ch
    has_mask = mask is not None
    per_b = lambda shape: pl.BlockSpec((1,) + shape, lambda b, c, pt: (b,) + (0,) * len(shape))
    in_specs = [per_b((rows, dk)), per_b((1, dk)), per_b((1, dv)), per_b((1, LANES))]
    args = [qm, knew, vnew, selnew]
    if has_mask:
        in_specs.append(pl.BlockSpec((1, 1, ch * PAGE), lambda b, c, pt: (b, 0, c)))
        args.append(mask)
    if mode == "diff":
        in_specs += [pl.BlockSpec(lam.shape, lambda b, c, pt: (0, 0)),
                     pl.BlockSpec((1, 2 * HEAD_DIM), lambda b, c, pt: (0, 0))]
        args += [lam, sub_g.reshape(1, -1)]
        out_w = 2 * HEAD_DIM
    else:
        out_w = HEAD_DIM
    in_specs += [pl.BlockSpec(memory_space=pl.ANY), pl.BlockSpec(memory_space=pl.ANY)]
    args += [cache_kt, cache_v]
    grid_spec = pltpu.PrefetchScalarGridSpec(
        num_scalar_prefetch=1, grid=(nb, nc), in_specs=in_specs,
        out_specs=pl.BlockSpec((1, n_heads, out_w), lambda b, c, pt: (b, 0, 0)),
        scratch_shapes=[pltpu.VMEM((PAGE_RING, ch) + cache_kt.shape[2:], F32),
                        pltpu.VMEM((PAGE_RING, ch) + cache_v.shape[2:], F32),
                        pltpu.SemaphoreType.DMA((2, PAGE_RING)),
                        pltpu.VMEM((rows, 1), F32), pltpu.VMEM((rows, 1), F32), pltpu.VMEM((rows, dv), F32)])
    return pl.pallas_call(
        functools.partial(_paged3_kernel, ch=ch, nc=nc, n_pages=n_pages, layer=layer, mode=mode, has_mask=has_mask,
                          lam_init=lam_init, n_heads=n_heads, n_kv=n_kv),
        out_shape=jax.ShapeDtypeStruct((nb, n_heads, out_w), F32),
        grid_spec=grid_spec,
        compiler_params=_cparams("arbitrary", "arbitrary"),
        name="paged_attn_" + mode,
    )(page_table.reshape(-1), *args)


def _paged_idx3_kernel(pt_ref, iq_ref, w_ref, ci_ref, o_ref, ibuf, sem, *, ch, nc, n_pages, layer, n_idx):
    slot = _pipelined_pages(pt_ref, (ci_ref,), (ibuf,), sem, nc=nc, n_pages=n_pages, ch=ch, layer=layer)
    scale = HEAD_DIM ** -0.5
    iq = iq_ref[0]
    w = w_ref[0] * (n_idx ** -0.5)
    outs = []
    for p in range(ch):
        sc = jnp.maximum(_dot(iq, ibuf[slot, p].astype(BF16)) * scale, 0.0)
        outs.append(jnp.sum(w * sc, axis=0, keepdims=True))
    o_ref[0] = jnp.concatenate(outs, axis=1)


def _paged_idx_scores3(page_table, layer, iq, w, cache_it, n_idx):
    nb, n_pages = page_table.shape
    ch = min(64, n_pages)
    nc = n_pages // ch
    grid_spec = pltpu.PrefetchScalarGridSpec(
        num_scalar_prefetch=1, grid=(nb, nc),
        in_specs=[pl.BlockSpec((1, n_idx, HEAD_DIM), lambda b, c, pt: (b, 0, 0)),
                  pl.BlockSpec((1, n_idx, 1), lambda b, c, pt: (b, 0, 0)),
                  pl.BlockSpec(memory_space=pl.ANY)],
        out_specs=pl.BlockSpec((1, 1, ch * PAGE), lambda b, c, pt: (b, 0, c)),
        scratch_shapes=[pltpu.VMEM((PAGE_RING, ch) + cache_it.shape[2:], F32),
                        pltpu.SemaphoreType.DMA((1, PAGE_RING))])
    return pl.pallas_call(
        functools.partial(_paged_idx3_kernel, ch=ch, nc=nc, n_pages=n_pages, layer=layer, n_idx=n_idx),
        out_shape=jax.ShapeDtypeStruct((nb, 1, n_pages * PAGE), F32),
        grid_spec=grid_spec,
        compiler_params=_cparams("arbitrary", "arbitrary"),
        name="paged_idx_scores",
    )(page_table.reshape(-1), iq, w, cache_it)


def _bf(w):
    return w.astype(BF16)


def _place(q3, slot_of_row, n_slots):
    onehot = (np.asarray(slot_of_row)[:, None] == np.arange(n_slots)[None, :]).astype(np.float32)
    out = q3[:, :, None, :] * jnp.asarray(onehot)[None, :, :, None]
    return out.reshape(q3.shape[0], q3.shape[1], n_slots * q3.shape[2])


def _even_weights(w_in, d_inner, conv_ch, n_a_heads, qw, kvw):
    c = np.cumsum([0, d_inner, conv_ch, n_a_heads, qw, kvw, kvw])
    z, xbc, dt, q, k, v = (w_in[:, c[j]:c[j + 1]] for j in range(6))
    dt = jnp.pad(dt, ((0, 0), (0, LANES - n_a_heads)))
    w = _bf(jnp.concatenate([z, xbc, q, k, v, dt], axis=1))
    off = np.cumsum([0, d_inner, conv_ch, qw, kvw, kvw])
    return w, dict(z=int(off[0]), xbc=int(off[1]), q=int(off[2]), k=int(off[3]), v=int(off[4]), dt=int(off[5]))


def _odd_weights(w_in, sizes):
    c = np.cumsum([0] + list(sizes))
    cq, ck, cv, dq, dk, dv, iq, iw, ik = (w_in[:, c[j]:c[j + 1]] for j in range(9))
    ik = jnp.pad(ik, ((0, 0), (0, LANES - ik.shape[1])))
    iw = jnp.pad(iw, ((0, 0), (0, LANES - iw.shape[1])))
    parts = [cq, dq, iq, ck, cv, dk, dv, ik, iw]
    off = np.cumsum([0] + [p.shape[1] for p in parts])
    names = ["cq", "dq", "iq", "ck", "cv", "dk", "dv", "ik", "iw"]
    return _bf(jnp.concatenate(parts, axis=1)), {n: int(o) for n, o in zip(names, off[:-1])}


def _mixer_even(x, pos, seq_mode, st_conv, st_ssm, win_k, win_v, norm_g, w_in, conv_w, conv_b, dt_bias, a_log,
                d_skip, gain, qn_g, kn_g, sinks, w_out):
    m, _ = x.shape
    n_a_heads = a_log.shape[0]
    d_inner = gain.shape[0]
    conv_ch = conv_w.shape[1]
    n_heads = sinks.shape[0]
    qw = n_heads * HEAD_DIM
    kvw = (w_in.shape[1] - d_inner - conv_ch - n_a_heads - qw) // 2
    n_kv = kvw // HEAD_DIM
    kw = conv_w.shape[0]
    gn = A_GROUPS * A_STATE
    hpg = n_a_heads // A_GROUPS
    w, off = _even_weights(w_in, d_inner, conv_ch, n_a_heads, qw, kvw)
    cos, sin = _rope_tables(pos)
    proj = _mm([x], [w], norm_g=norm_g, name="in_proj_even")
    q_f, q_b = _norm_rope(proj, off["q"], qw, qn_g, cos, sin, name="swa_q_rope")
    k_f, k_b = _norm_rope(proj, off["k"], kvw, kn_g, cos, sin, name="swa_k_rope")
    v_f = proj[:, off["v"]:off["v"] + kvw]
    xbc_raw = proj[:, off["xbc"]:off["xbc"] + conv_ch]
    if seq_mode:
        state8 = jnp.zeros((SUBLANES, conv_ch), F32)
        xbc = _conv_seq(proj, off["xbc"], conv_ch, state8, conv_w, conv_b)
        s0_t = jnp.zeros((A_GROUPS, A_STATE, hpg * A_HEAD_DIM), F32)
        ya, st = _ssd_seq(xbc, proj, off["z"], off["dt"], dt_bias, a_log, d_skip, gain, s0_t, n_a_heads)
        ssm_new = st.reshape(A_GROUPS, A_STATE, hpg, A_HEAD_DIM).transpose(0, 2, 3, 1).reshape(
            1, n_a_heads, A_HEAD_DIM, A_STATE)
        conv_new = xbc_raw[m - (kw - 1):][None]
        ob = _swa_seq(q_b, k_b, _bf(v_f), sinks, n_heads, n_kv)
        wb = min(WINDOW, m)
        new_k = k_f[m - wb:].reshape(1, wb, n_kv, HEAD_DIM)
        new_v = v_f[m - wb:].reshape(1, wb, n_kv, HEAD_DIM)
    else:
        xbc = _conv_step(proj, off["xbc"], conv_ch, st_conv.transpose(1, 0, 2), conv_w, conv_b)
        xdt, dec = _ssd_step_pre(xbc, proj, off["dt"], dt_bias, a_log, d_inner)
        ssm_new, y = _ssd_step(st_ssm, xdt.T, dec.T, xbc[:, d_inner:d_inner + gn], xbc[:, d_inner + gn:])
        ya = _gated_norm_call(y, xbc, proj, off["z"], d_skip, gain)
        conv_new = jnp.concatenate([st_conv[:, 1:], xbc_raw[:, None, :]], axis=1)
        wb = win_k.shape[1]
        grp = n_heads // n_kv
        qm = _place(q_f.reshape(m, n_heads, HEAD_DIM), [h // grp for h in range(n_heads)], n_kv)
        o, new_k, new_v = _swa_step(qm, k_f[:, None, :], v_f[:, None, :], win_k.reshape(m, wb, kvw),
                                    win_v.reshape(m, wb, kvw), sinks, n_heads, n_kv)
        ob = o.reshape(m, qw)
        new_k = new_k.reshape(m, wb, n_kv, HEAD_DIM)
        new_v = new_v.reshape(m, wb, n_kv, HEAD_DIM)
    wo = _bf(w_out)
    y = _mm([ya, ob], [wo[:d_inner], wo[d_inner:]], res=x, name="out_proj_even")
    return y, (conv_new, ssm_new, new_k, new_v)


def _mixer_odd(x, pos, seq_mode, paged, norm_g, w_in, qn_g, kn_g, lam_p, sub_g, dqn_g, dkn_g, w_out, lam_init,
               sizes, n_sel):
    m, _ = x.shape
    hd = HEAD_DIM
    c_heads = sizes[0] // (2 * hd)
    c_kv = sizes[1] // (2 * hd)
    d_heads = sizes[3] // hd
    d_kv = sizes[4] // hd
    n_idx = sizes[7]
    w, off = _odd_weights(w_in, sizes)
    cos, sin = _rope_tables(pos)
    proj = _mm([x], [w], norm_g=norm_g, name="in_proj_odd")
    cq_f, cq_b = _norm_rope(proj, off["cq"], sizes[0], qn_g, cos, sin, name="diff_q_rope")
    ck_f, ck_b = _norm_rope(proj, off["ck"], sizes[1], kn_g, cos, sin, name="diff_k_rope")
    dq_f, dq_b = _norm_rope(proj, off["dq"], sizes[3], dqn_g, cos, sin, name="dsa_q_rope")
    dk_f, dk_b = _norm_rope(proj, off["dk"], sizes[4], dkn_g, cos, sin, name="dsa_k_rope")
    iq_f, iq_b = _norm_rope(proj, off["iq"], sizes[6], None, cos, sin, name="idx_q_rope")
    ik_f, ik_b = _norm_rope(proj, off["ik"], LANES, None, cos, sin, name="idx_k_rope")
    cv_f = proj[:, off["cv"]:off["cv"] + sizes[2]]
    dv_f = proj[:, off["dv"]:off["dv"] + sizes[5]]
    iw = proj[:, off["iw"]:off["iw"] + LANES]
    if seq_mode:
        oc = _diff_seq_t(cq_f, ck_b, _bf(cv_f), lam_p, sub_g, lam_init, c_heads, c_kv)
        od = _dsa_seq_t(iq_b, iw, dq_f, ik_b, dk_b, _bf(dv_f), d_heads, d_kv, n_idx, n_sel)
    else:
        c_k, c_v, d_k, d_v, d_i, table, layer = paged
        pool = c_k.shape[1]
        ones = jnp.ones((m, 1, LANES), F32)
        grp = c_heads // c_kv
        q4 = cq_f.reshape(m, c_heads, 2, hd).transpose(0, 2, 1, 3).reshape(m, 2 * c_heads, hd)
        slots = [(h // grp) * 2 + c for c in range(2) for h in range(c_heads)]
        qm_c = _place(q4, slots, 2 * c_kv)
        oc = _paged_attn3(table, layer, qm_c, ck_f[:, None, :], cv_f[:, None, :], ones, None,
                          jnp.transpose(c_k, (0, 1, 3, 4, 5, 2)), c_v.reshape(c_v.shape[0], pool, PAGE * c_kv, -1),
                          mode="diff", n_heads=c_heads, n_kv=c_kv, lam=lam_p, sub_g=sub_g, lam_init=lam_init)
        oc = oc.reshape(m, -1)
        scores = _paged_idx_scores3(table, layer, iq_b.reshape(m, n_idx, hd), iw[:, :n_idx, None],
                                    jnp.transpose(d_i, (0, 1, 3, 2)), n_idx)
        ik_tiled = jnp.tile(ik_b[:, :hd], (1, n_idx))
        mask, mnew = _topk_mask(scores.reshape(m, -1), iq_b, ik_tiled, iw, n_idx, n_sel)
        dgrp = d_heads // d_kv
        qm_d = _place(dq_f.reshape(m, d_heads, hd), [h // dgrp for h in range(d_heads)], d_kv)
        od = _paged_attn3(table, layer, qm_d, dk_f[:, None, :], dv_f[:, None, :], mnew[:, None, :], mask[:, None, :],
                          jnp.transpose(d_k, (0, 1, 3, 4, 2)), jnp.transpose(d_v, (0, 1, 3, 4, 2)),
                          mode="gqa", n_heads=d_heads, n_kv=d_kv)
        od = od.reshape(m, -1)
    wo = _bf(w_out)
    y = _mm([oc, od], [wo[:oc.shape[1]], wo[oc.shape[1]:]], res=x, name="out_proj_odd")
    lead = (1, m) if seq_mode else (m, 1)
    caches = (ck_f.reshape(lead + (c_kv, 2, hd)), cv_f.reshape(lead + (c_kv, 2 * hd)),
              dk_f.reshape(lead + (d_kv, hd)), dv_f.reshape(lead + (d_kv, hd)), ik_f[:, :hd].reshape(lead + (hd,)))
    return y, caches


def kernel(x_prompt, x_sample, state_ssm, state_ssm_conv, cache_swa_k, cache_swa_v, cache_c_k, cache_c_v, cache_d_k, cache_d_v, cache_d_idx, state_ffn_conv, page_table, norm_mix_g, norm_ffn_g, a_w_in, a_conv_w, a_conv_b, a_dt_bias, a_A_log, a_D, a_norm_g, b_qn_g, b_kn_g, b_sinks, e_w_out, m_w_in, c_qn_g, c_kn_g, c_lam, c_subln_g, d_qn_g, d_kn_g, m_w_out, ffn_w_gate, ffn_w_up, ffn_conv_w, ffn_conv_b, ffn_w_down):
    bp, seq, d_model = x_prompt.shape
    nb = x_sample.shape[0]
    assert bp == 1 and x_sample.shape[1] == 1
    depth = norm_mix_g.shape[0]
    d_ff = ffn_w_gate.shape[2]
    past = page_table.shape[1] * PAGE
    xp = x_prompt.reshape(seq, d_model)
    xs = x_sample.reshape(nb, d_model)
    pos_p = jnp.arange(seq)
    pos_s = jnp.full((nb,), past, I32)
    hd = HEAD_DIM
    c_kv, d_kv, idx_dim = cache_c_k.shape[3], cache_d_k.shape[3], cache_d_idx.shape[3]
    d_heads = d_model // 128
    c_heads = d_model // 256
    n_idx = m_w_in.shape[2] - (c_heads * 2 * hd + 2 * c_kv * 2 * hd + d_heads * hd + 2 * d_kv * hd
                               + d_heads * hd + idx_dim)
    odd_sizes = (c_heads * 2 * hd, c_kv * 2 * hd, c_kv * 2 * hd, d_heads * hd, d_kv * hd, d_kv * hd,
                 d_heads * hd, n_idx, idx_dim)
    outs_p = {k: [] for k in ("ssm", "cnv", "swk", "swv", "ck", "cv", "dk", "dv", "di", "fc")}
    outs_s = {k: [] for k in outs_p}
    for i in range(depth):
        if i % 2 == 0:
            e = i // 2
            wts = (norm_mix_g[i], a_w_in[e], a_conv_w[e], a_conv_b[e], a_dt_bias[e], a_A_log[e], a_D[e],
                   a_norm_g[e], b_qn_g[e], b_kn_g[e], b_sinks[e], e_w_out[e])
            xp, (c1, s1, k1, v1) = _mixer_even(xp, pos_p, True, None, None, None, None, *wts)
            xs, (c2, s2, k2, v2) = _mixer_even(xs, pos_s, False, state_ssm_conv[e], state_ssm[e],
                                               cache_swa_k[e], cache_swa_v[e], *wts)
            for d, vals in ((outs_p, (c1, s1, k1, v1)), (outs_s, (c2, s2, k2, v2))):
                for key, val in zip(("cnv", "ssm", "swk", "swv"), vals):
                    d[key].append(val)
        else:
            o = i // 2
            lam_init = 0.8 - 0.6 * math.exp(-0.3 * i)
            wts = (norm_mix_g[i], m_w_in[o], c_qn_g[o], c_kn_g[o], c_lam[o], c_subln_g[o], d_qn_g[o], d_kn_g[o],
                   m_w_out[o], lam_init, odd_sizes)
            xp, cp = _mixer_odd(xp, pos_p, True, None, *wts, min(256, seq // 4))
            xs, cs = _mixer_odd(xs, pos_s, False,
                                (cache_c_k, cache_c_v, cache_d_k, cache_d_v, cache_d_idx, page_table, o),
                                *wts, min(256, (past + 1) // 4))
            for d, vals in ((outs_p, cp), (outs_s, cs)):
                for key, val in zip(("ck", "cv", "dk", "dv", "di"), vals):
                    d[key].append(val)
        fw = (norm_ffn_g[i], _bf(ffn_w_gate[i]), _bf(ffn_w_up[i]), ffn_conv_w[i], ffn_conv_b[i], _bf(ffn_w_down[i]))
        zrow = jnp.zeros((1, d_ff), F32)
        xp, gp = _ffn(xp, zrow, zrow, *fw, seq_mode=True)
        outs_p["fc"].append(gp[gp.shape[0] - (ffn_conv_w.shape[1] - 1):][None])
        st = state_ffn_conv[i]
        xs, gs = _ffn(xs, st[:, 0, :], st[:, 1, :], *fw, seq_mode=False)
        outs_s["fc"].append(jnp.stack([st[:, 1, :], gs], axis=1))
    order = ("ssm", "cnv", "swk", "swv", "ck", "cv", "dk", "dv", "di", "fc")
    return ((xp.reshape(1, seq, d_model), xs.reshape(nb, 1, d_model))
            + tuple(jnp.stack(outs_p[k]) for k in order) + tuple(jnp.stack(outs_s[k]) for k in order))
```

```python
import functools
import math

import jax
import jax.numpy as jnp
import numpy as np
from jax import lax
from jax.experimental import pallas as pl
from jax.experimental.pallas import tpu as pltpu

F32 = jnp.float32
BF16 = jnp.bfloat16
I32 = jnp.int32

EPS = 1e-6
ROPE_THETA = 10000.0
HEAD_DIM = 64
LANES = 128
SUBLANES = 8
VMEM_LIMIT = 56 * 1024 * 1024
WINDOW = 128
SSD_CHUNK = 128
PAGE = 128
A_GROUPS = 2
A_HEAD_DIM = 64
A_STATE = 128
INT_MIN = -2147483648


def _cparams(*sem):
    return pltpu.CompilerParams(dimension_semantics=sem, vmem_limit_bytes=VMEM_LIMIT)


def _pick_tile(n, cap):
    best = LANES
    for m in range(1, n // LANES + 1):
        if n % (m * LANES) == 0 and m * LANES <= cap:
            best = m * LANES
    return best


def _row_tile(m, cap):
    t = min(m, cap)
    while m % t:
        t //= 2
    return t


def _split3(x):
    h = x.astype(BF16)
    r = x - h.astype(F32)
    m = r.astype(BF16)
    lo = (r - m.astype(F32)).astype(BF16)
    return h, m, lo


def _dot(a, b):
    return jnp.dot(a, b, preferred_element_type=F32)


def _dot_nt(a, b):
    return lax.dot_general(a, b, (((1,), (1,)), ((), ())), preferred_element_type=F32)


def _dot3(x, w01):
    h, m, lo = _split3(x)
    return _dot(h, w01) + _dot(m, w01) + _dot(lo, w01)


def _dot3_left(w01, x):
    h, m, lo = _split3(x)
    return _dot(w01, h) + _dot(w01, m) + _dot(w01, lo)


def _silu(x):
    return x * (1.0 / (1.0 + jnp.exp(-x)))


def _softplus(x):
    return jnp.maximum(x, 0.0) + jnp.log(1.0 + jnp.exp(-jnp.abs(x)))


def _mm_kernel(*refs, n_lhs, has_norm, has_res):
    xs = refs[:n_lhs]
    pos = n_lhs
    g_ref = None
    if has_norm:
        g_ref = refs[pos]
        pos += 1
    ws = refs[pos:pos + n_lhs]
    pos += n_lhs
    res_ref = None
    if has_res:
        res_ref = refs[pos]
        pos += 1
    o_ref = refs[pos]
    xb = refs[pos + 1:pos + 1 + n_lhs]

    @pl.when(pl.program_id(1) == 0)
    def _():
        for k in range(n_lhs):
            x = xs[k][...]
            if has_norm and k == 0:
                x = x * lax.rsqrt(jnp.mean(x * x, axis=-1, keepdims=True) + EPS) * g_ref[...]
            xb[k][...] = x.astype(BF16)

    acc = _dot(xb[0][...], ws[0][...])
    for k in range(1, n_lhs):
        acc = acc + _dot(xb[k][...], ws[k][...])
    if has_res:
        acc = acc + res_ref[...]
    o_ref[...] = acc


def _mm(xs, ws, *, norm_g=None, res=None, tm_cap=512, tn_cap=1280, name="mm"):
    m = xs[0].shape[0]
    n = ws[0].shape[1]
    tm = _row_tile(m, tm_cap)
    tn = _pick_tile(n, tn_cap)
    n_lhs = len(xs)
    in_specs = [pl.BlockSpec((tm, x.shape[1]), lambda i, j: (i, 0)) for x in xs]
    args = list(xs)
    if norm_g is not None:
        in_specs.append(pl.BlockSpec((1, xs[0].shape[1]), lambda i, j: (0, 0)))
        args.append(norm_g.reshape(1, -1))
    in_specs += [pl.BlockSpec((w.shape[0], tn), lambda i, j: (0, j)) for w in ws]
    args += list(ws)
    if res is not None:
        in_specs.append(pl.BlockSpec((tm, tn), lambda i, j: (i, j)))
        args.append(res)
    return pl.pallas_call(
        functools.partial(_mm_kernel, n_lhs=n_lhs, has_norm=norm_g is not None, has_res=res is not None),
        out_shape=jax.ShapeDtypeStruct((m, n), F32),
        grid=(m // tm, n // tn),
        in_specs=in_specs,
        out_specs=pl.BlockSpec((tm, tn), lambda i, j: (i, j)),
        scratch_shapes=[pltpu.VMEM((tm, x.shape[1]), BF16) for x in xs],
        compiler_params=_cparams("parallel", "arbitrary"),
        name=name,
    )(*args)


def _ffn_kernel(x_ref, halo_ref, g_ref, wg_ref, wu_ref, cw_ref, cb_ref, wd_ref, p0_ref, p1_ref,
                o_ref, gout_ref, xb_ref, hb_ref, gs_ref, acc_ref, *, seq_mode, tm):
    i = pl.program_id(0)
    j = pl.program_id(1)
    nj = pl.num_programs(1)

    def norm(x):
        return (x * lax.rsqrt(jnp.mean(x * x, axis=-1, keepdims=True) + EPS) * g_ref[...]).astype(BF16)

    @pl.when(j == 0)
    def _():
        xb_ref[...] = norm(x_ref[...])
        if seq_mode:
            hb_ref[...] = norm(halo_ref[...])
        acc_ref[...] = jnp.zeros_like(acc_ref)

    g = _dot(xb_ref[...], wg_ref[...])
    u = _dot(xb_ref[...], wu_ref[...])
    cw = cw_ref[...]
    if seq_mode:
        carried = jnp.concatenate([jnp.zeros((SUBLANES - 2, g.shape[1]), F32), p0_ref[...], p1_ref[...]], axis=0)
        prev = jnp.where(i == 0, carried, _dot(hb_ref[...], wg_ref[...]))
        gs_ref[0:SUBLANES, :] = prev
        gs_ref[SUBLANES:, :] = g
        g1 = gs_ref[pl.ds(SUBLANES - 1, tm), :]
        g2 = gs_ref[pl.ds(SUBLANES - 2, tm), :]
        gout_ref[...] = g[tm - SUBLANES:, :]
    else:
        g1 = p1_ref[...]
        g2 = p0_ref[...]
        gout_ref[...] = g
    c = cw[0:1, :] * g2 + cw[1:2, :] * g1 + cw[2:3, :] * g + cb_ref[...]
    act = (_silu(c) * u).astype(BF16)
    acc_ref[...] += _dot(act, wd_ref[...])

    @pl.when(j == nj - 1)
    def _():
        o_ref[...] = x_ref[...] + acc_ref[...]


def _ffn(x, prev0, prev1, norm_g, wg, wu, conv_w, conv_b, wd, *, seq_mode):
    m, d = x.shape
    f = wg.shape[1]
    tm = _row_tile(m, 1024 if seq_mode else 128)
    tn = _pick_tile(f, 1408)
    ni, nj = m // tm, f // tn
    hb = tm // SUBLANES
    if seq_mode:
        prev_spec = pl.BlockSpec((1, tn), lambda i, j: (0, j))
        gout_rows, gout_shape = SUBLANES, (ni * SUBLANES, f)
    else:
        prev_spec = pl.BlockSpec((tm, tn), lambda i, j: (i, j))
        gout_rows, gout_shape = tm, (m, f)
    out, gout = pl.pallas_call(
        functools.partial(_ffn_kernel, seq_mode=seq_mode, tm=tm),
        out_shape=(jax.ShapeDtypeStruct((m, d), F32), jax.ShapeDtypeStruct(gout_shape, F32)),
        grid=(ni, nj),
        in_specs=[
            pl.BlockSpec((tm, d), lambda i, j: (i, 0)),
            pl.BlockSpec((SUBLANES, d), lambda i, j: (jnp.maximum(i * hb - 1, 0), 0)),
            pl.BlockSpec((1, d), lambda i, j: (0, 0)),
            pl.BlockSpec((d, tn), lambda i, j: (0, j)),
            pl.BlockSpec((d, tn), lambda i, j: (0, j)),
            pl.BlockSpec((conv_w.shape[0], tn), lambda i, j: (0, j)),
            pl.BlockSpec((1, tn), lambda i, j: (0, j)),
            pl.BlockSpec((tn, d), lambda i, j: (j, 0)),
            prev_spec, prev_spec,
        ],
        out_specs=(pl.BlockSpec((tm, d), lambda i, j: (i, 0)),
                   pl.BlockSpec((gout_rows, tn), lambda i, j: (i, j))),
        scratch_shapes=[pltpu.VMEM((tm, d), BF16), pltpu.VMEM((SUBLANES, d), BF16),
                        pltpu.VMEM((tm + SUBLANES, tn), F32), pltpu.VMEM((tm, d), F32)],
        compiler_params=_cparams("parallel", "arbitrary"),
        name="conv_ffn",
    )(x, x, norm_g.reshape(1, -1), wg, wu, conv_w, conv_b.reshape(1, -1), wd, prev0, prev1)
    return out, gout


def _conv_seq_kernel(x_ref, halo_ref, st_ref, w_ref, b_ref, o_ref, xs_ref, *, tm, kw):
    i = pl.program_id(0)
    prev = jnp.where(i == 0, st_ref[...], halo_ref[...])
    xs_ref[0:SUBLANES, :] = prev
    xs_ref[SUBLANES:, :] = x_ref[...]
    w = w_ref[...]
    acc = b_ref[...] + w[kw - 1:kw, :] * x_ref[...]
    for t in range(1, kw):
        acc = acc + w[kw - 1 - t:kw - t, :] * xs_ref[pl.ds(SUBLANES - t, tm), :]
    o_ref[...] = _silu(acc)


def _conv_seq(src, col0, width, state8, w, b):
    m = src.shape[0]
    kw = w.shape[0]
    tm = _row_tile(m, 512)
    tn = _pick_tile(math.gcd(width, col0) if col0 else width, 512)
    cb = col0 // tn
    hb = tm // SUBLANES
    return pl.pallas_call(
        functools.partial(_conv_seq_kernel, tm=tm, kw=kw),
        out_shape=jax.ShapeDtypeStruct((m, width), F32),
        grid=(m // tm, width // tn),
        in_specs=[
            pl.BlockSpec((tm, tn), lambda i, j: (i, cb + j)),
            pl.BlockSpec((SUBLANES, tn), lambda i, j: (jnp.maximum(i * hb - 1, 0), cb + j)),
            pl.BlockSpec((SUBLANES, tn), lambda i, j: (0, j)),
            pl.BlockSpec((kw, tn), lambda i, j: (0, j)),
            pl.BlockSpec((1, tn), lambda i, j: (0, j)),
        ],
        out_specs=pl.BlockSpec((tm, tn), lambda i, j: (i, j)),
        scratch_shapes=[pltpu.VMEM((tm + SUBLANES, tn), F32)],
        compiler_params=_cparams("parallel", "parallel"),
        name="ssm_conv_seq",
    )(src, src, state8, w, b.reshape(1, -1))


def _conv_step_kernel(x_ref, s_ref, w_ref, b_ref, o_ref, *, kw):
    w = w_ref[...]
    acc = b_ref[...] + w[kw - 1:kw, :] * x_ref[...]
    for t in range(kw - 1):
        acc = acc + w[t:t + 1, :] * s_ref[t]
    o_ref[...] = _silu(acc)


def _conv_step(src, col0, width, state, w, b):
    m = src.shape[0]
    kw = w.shape[0]
    tn = _pick_tile(math.gcd(width, col0) if col0 else width, 512)
    cb = col0 // tn
    return pl.pallas_call(
        functools.partial(_conv_step_kernel, kw=kw),
        out_shape=jax.ShapeDtypeStruct((m, width), F32),
        grid=(width // tn,),
        in_specs=[
            pl.BlockSpec((m, tn), lambda j: (0, cb + j)),
            pl.BlockSpec((kw - 1, m, tn), lambda j: (0, 0, j)),
            pl.BlockSpec((kw, tn), lambda j: (0, j)),
            pl.BlockSpec((1, tn), lambda j: (0, j)),
        ],
        out_specs=pl.BlockSpec((m, tn), lambda j: (0, j)),
        compiler_params=_cparams("parallel"),
        name="ssm_conv_step",
    )(src, state, w, b.reshape(1, -1))


def _seg_ones(seg):
    r = lax.broadcasted_iota(I32, (LANES, LANES), 0) // seg
    c = lax.broadcasted_iota(I32, (LANES, LANES), 1) // seg
    return (r == c).astype(BF16)


def _rope128(x, cos, sin_signed):
    lane = lax.broadcasted_iota(I32, x.shape, 1)
    rot = jnp.where(lane % HEAD_DIM < HEAD_DIM // 2,
                    pltpu.roll(x, LANES - HEAD_DIM // 2, 1), pltpu.roll(x, HEAD_DIM // 2, 1))
    return x * cos + rot * sin_signed


def _norm_rope_kernel(x_ref, g_ref, cos_ref, sin_ref, o_ref, ob_ref, *, do_norm, width):
    cos = cos_ref[...]
    sin = sin_ref[...]
    ones = _seg_ones(HEAD_DIM)
    for c in range(width // LANES):
        x = x_ref[:, c * LANES:(c + 1) * LANES]
        if do_norm:
            ms = _dot3(x * x, ones) * (1.0 / HEAD_DIM)
            x = x * lax.rsqrt(ms + EPS) * g_ref[...]
        y = _rope128(x, cos, sin)
        o_ref[:, c * LANES:(c + 1) * LANES] = y
        ob_ref[:, c * LANES:(c + 1) * LANES] = y.astype(BF16)


def _norm_rope(src, col0, width, gain, cos, sin, *, name):
    m = src.shape[0]
    tm = _row_tile(m, 512)
    assert col0 % width == 0
    cb = col0 // width
    g = jnp.ones((1, LANES), F32) if gain is None else jnp.tile(gain.reshape(1, HEAD_DIM), (1, LANES // HEAD_DIM))
    return pl.pallas_call(
        functools.partial(_norm_rope_kernel, do_norm=gain is not None, width=width),
        out_shape=(jax.ShapeDtypeStruct((m, width), F32), jax.ShapeDtypeStruct((m, width), BF16)),
        grid=(m // tm,),
        in_specs=[
            pl.BlockSpec((tm, width), lambda i: (i, cb)),
            pl.BlockSpec((1, LANES), lambda i: (0, 0)),
            pl.BlockSpec((tm, LANES), lambda i: (i, 0)),
            pl.BlockSpec((tm, LANES), lambda i: (i, 0)),
        ],
        out_specs=(pl.BlockSpec((tm, width), lambda i: (i, 0)), pl.BlockSpec((tm, width), lambda i: (i, 0))),
        compiler_params=_cparams("parallel"),
        name=name,
    )(src, g, cos, sin)


def _rope_tables(pos):
    half = HEAD_DIM // 2
    inv = ROPE_THETA ** (-jnp.arange(half, dtype=F32) / half)
    ang = pos.astype(F32)[:, None] * inv[None, :]
    cos, sin = jnp.cos(ang), jnp.sin(ang)
    cos128 = jnp.tile(jnp.concatenate([cos, cos], axis=1), (1, LANES // HEAD_DIM))
    sin128 = jnp.tile(jnp.concatenate([-sin, sin], axis=1), (1, LANES // HEAD_DIM))
    return cos128, sin128


def _head_expand(n_heads_pad, width):
    r = lax.broadcasted_iota(I32, (n_heads_pad, width), 0)
    c = lax.broadcasted_iota(I32, (n_heads_pad, width), 1) // A_HEAD_DIM
    return (r == c).astype(BF16)


def _gated_norm(y, xs, z, dskip, gain):
    yz = (y + xs * dskip) * _silu(z)
    gw = yz.shape[1] // A_GROUPS
    parts = []
    for g in range(A_GROUPS):
        p = yz[:, g * gw:(g + 1) * gw]
        parts.append(p * lax.rsqrt(jnp.mean(p * p, axis=-1, keepdims=True) + EPS))
    return jnp.concatenate(parts, axis=1) * gain


def _ssd_seq_kernel(xs_ref, b_ref, c_ref, dt_ref, z_ref, dtb_ref, alog_ref, dskip_ref, gain_ref, s0_ref,
                    ya_ref, sout_ref, st_ref, y_ref, *, n_heads):
    ci = pl.program_id(0)
    q = SSD_CHUNK
    d_inner = xs_ref.shape[1]
    hpg = n_heads // A_GROUPS
    gw = d_inner // A_GROUPS

    @pl.when(ci == 0)
    def _():
        st_ref[...] = s0_ref[...]

    xs = xs_ref[...]
    dt = _softplus(dt_ref[...] + dtb_ref[...])
    a = dt * (-jnp.exp(alog_ref[...]))
    row = lax.broadcasted_iota(I32, (q, q), 0)
    col = lax.broadcasted_iota(I32, (q, q), 1)
    causal = col <= row
    tri = causal.astype(BF16)
    tri_t = (row <= col).astype(BF16)
    acs = _dot3_left(tri, a)
    acs_t = _dot3(a.T, tri_t)
    expand = _head_expand(LANES, d_inner)
    acs_x = _dot3(acs, expand)
    dt_x = _dot3(dt, expand)
    e_acs = jnp.exp(acs_x)
    last = acs_x[q - 1:q, :]
    decay_s = jnp.exp(last - acs_x)
    xdt = xs * dt_x
    xdt_b = xdt.astype(BF16)
    xdec_b = (xdt * decay_s).astype(BF16)
    chunk_decay = e_acs[q - 1:q, :]

    for g in range(A_GROUPS):
        bg = b_ref[:, g * A_STATE:(g + 1) * A_STATE]
        cg = c_ref[:, g * A_STATE:(g + 1) * A_STATE].astype(BF16)
        cb = _dot_nt(cg, bg.astype(BF16))
        st_g = st_ref[g]
        y_off = _dot(cg, st_g.astype(BF16)) * e_acs[:, g * gw:(g + 1) * gw]
        for hh in range(hpg):
            h = g * hpg + hh
            diff = acs[:, h:h + 1] - acs_t[h:h + 1, :]
            m = (cb * jnp.where(causal, jnp.exp(diff), 0.0)).astype(BF16)
            lo = h * A_HEAD_DIM
            y_ref[:, lo:lo + A_HEAD_DIM] = (_dot(m, xdt_b[:, lo:lo + A_HEAD_DIM])
                                            + y_off[:, hh * A_HEAD_DIM:(hh + 1) * A_HEAD_DIM])
        st_ref[g] = st_g * chunk_decay[:, g * gw:(g + 1) * gw] + _dot(bg.T.astype(BF16), xdec_b[:, g * gw:(g + 1) * gw])

    ya_ref[...] = _gated_norm(y_ref[...], xs, z_ref[...], dskip_ref[...], gain_ref[...])

    @pl.when(ci == pl.num_programs(0) - 1)
    def _():
        sout_ref[...] = st_ref[...]


def _ssd_seq(xbc, proj, z_col, dt_col, dt_bias, a_log, d_skip, gain, s0_t, n_heads):
    t = xbc.shape[0]
    q = SSD_CHUNK
    d_inner = n_heads * A_HEAD_DIM
    gn = A_GROUPS * A_STATE
    bcol = d_inner // gn
    pad = lambda v: jnp.pad(v.reshape(1, -1), ((0, 0), (0, LANES - v.shape[-1])))
    dskip_x = jnp.repeat(d_skip, A_HEAD_DIM).reshape(1, d_inner)
    const = lambda shape: pl.BlockSpec(shape, lambda c: (0,) * len(shape))
    return pl.pallas_call(
        functools.partial(_ssd_seq_kernel, n_heads=n_heads),
        out_shape=(jax.ShapeDtypeStruct((t, d_inner), F32), jax.ShapeDtypeStruct(s0_t.shape, F32)),
        grid=(t // q,),
        in_specs=[
            pl.BlockSpec((q, d_inner), lambda c: (c, 0)),
            pl.BlockSpec((q, gn), lambda c: (c, bcol)),
            pl.BlockSpec((q, gn), lambda c: (c, bcol + 1)),
            pl.BlockSpec((q, LANES), lambda c: (c, dt_col // LANES)),
            pl.BlockSpec((q, d_inner), lambda c: (c, z_col // d_inner)),
            const((1, LANES)), const((1, LANES)), const((1, d_inner)), const((1, d_inner)),
            const(s0_t.shape),
        ],
        out_specs=(pl.BlockSpec((q, d_inner), lambda c: (c, 0)), const(s0_t.shape)),
        scratch_shapes=[pltpu.VMEM(s0_t.shape, F32), pltpu.VMEM((q, d_inner), F32)],
        compiler_params=_cparams("arbitrary"),
        name="ssd_seq",
    )(xbc, xbc, xbc, proj, proj, pad(dt_bias), pad(a_log), dskip_x, gain.reshape(1, -1), s0_t)


def _ssd_step_kernel(s_ref, xdt_t_ref, dec_t_ref, b_ref, c_ref, sout_ref, y_ref, *, n_heads):
    b = pl.program_id(0)
    nb = xdt_t_ref.shape[1]
    hp = n_heads * A_HEAD_DIM
    gw = hp // A_GROUPS
    lane = lax.broadcasted_iota(I32, (hp, nb), 1)
    dec = jnp.sum(jnp.where(lane == b, dec_t_ref[...], 0.0), axis=1, keepdims=True)
    rows = lax.broadcasted_iota(I32, (nb, A_STATE), 0)
    s = s_ref[0].reshape(hp, A_STATE)
    xdt_t = xdt_t_ref[...].astype(BF16)
    outs = []
    for g in range(A_GROUPS):
        brow = b_ref[0, :, g * A_STATE:(g + 1) * A_STATE]
        zb = jnp.where(rows == b, jnp.broadcast_to(brow, (nb, A_STATE)), 0.0).astype(BF16)
        upd = _dot(xdt_t[g * gw:(g + 1) * gw, :], zb)
        sn = s[g * gw:(g + 1) * gw, :] * dec[g * gw:(g + 1) * gw, :] + upd
        sout_ref[0, g * (n_heads // A_GROUPS):(g + 1) * (n_heads // A_GROUPS)] = sn.reshape(
            n_heads // A_GROUPS, A_HEAD_DIM, A_STATE)
        crow = c_ref[0, :, g * A_STATE:(g + 1) * A_STATE]
        c8 = jnp.broadcast_to(crow, (SUBLANES, A_STATE)).astype(BF16)
        outs.append(_dot_nt(c8, sn.astype(BF16))[0:1, :])
    y_ref[0] = jnp.concatenate(outs, axis=1)


def _ssd_step(state, xdt_t, dec_t, bmat, cmat):
    nb, n_heads, p, n = state.shape
    hp = n_heads * p
    new_state, y = pl.pallas_call(
        functools.partial(_ssd_step_kernel, n_heads=n_heads),
        out_shape=(jax.ShapeDtypeStruct(state.shape, F32), jax.ShapeDtypeStruct((nb, 1, hp), F32)),
        grid=(nb,),
        in_specs=[
            pl.BlockSpec((1, n_heads, p, n), lambda b: (b, 0, 0, 0)),
            pl.BlockSpec((hp, nb), lambda b: (0, 0)),
            pl.BlockSpec((hp, nb), lambda b: (0, 0)),
            pl.BlockSpec((1, 1, bmat.shape[1]), lambda b: (b, 0, 0)),
            pl.BlockSpec((1, 1, cmat.shape[1]), lambda b: (b, 0, 0)),
        ],
        out_specs=(pl.BlockSpec((1, n_heads, p, n), lambda b: (b, 0, 0, 0)),
                   pl.BlockSpec((1, 1, hp), lambda b: (b, 0, 0))),
        compiler_params=_cparams("arbitrary"),
        name="ssd_step",
    )(state, xdt_t, dec_t, bmat[:, None, :], cmat[:, None, :])
    return new_state, y.reshape(nb, hp)


def _ssd_step_pre_kernel(xs_ref, dt_ref, dtb_ref, alog_ref, xdt_ref, dec_ref):
    dt = _softplus(dt_ref[...] + dtb_ref[...])
    expand = _head_expand(LANES, xs_ref.shape[1])
    xdt_ref[...] = xs_ref[...] * _dot3(dt, expand)
    dec_ref[...] = jnp.exp(_dot3(dt * (-jnp.exp(alog_ref[...])), expand))


def _ssd_step_pre(xbc, proj, dt_col, dt_bias, a_log, d_inner):
    m = xbc.shape[0]
    pad = lambda v: jnp.pad(v.reshape(1, -1), ((0, 0), (0, LANES - v.shape[-1])))
    return pl.pallas_call(
        _ssd_step_pre_kernel,
        out_shape=(jax.ShapeDtypeStruct((m, d_inner), F32), jax.ShapeDtypeStruct((m, d_inner), F32)),
        grid=(1,),
        in_specs=[pl.BlockSpec((m, d_inner), lambda i: (0, 0)),
                  pl.BlockSpec((m, LANES), lambda i: (0, dt_col // LANES)),
                  pl.BlockSpec((1, LANES), lambda i: (0, 0)), pl.BlockSpec((1, LANES), lambda i: (0, 0))],
        out_specs=(pl.BlockSpec((m, d_inner), lambda i: (0, 0)), pl.BlockSpec((m, d_inner), lambda i: (0, 0))),
        compiler_params=_cparams("arbitrary"),
        name="ssd_step_pre",
    )(xbc, proj, pad(dt_bias), pad(a_log))


def _gated_norm_kernel(y_ref, xs_ref, z_ref, dskip_ref, gain_ref, o_ref):
    o_ref[...] = _gated_norm(y_ref[...], xs_ref[...], z_ref[...], dskip_ref[...], gain_ref[...])


def _gated_norm_call(y, xbc, proj, z_col, d_skip, gain):
    m, d_inner = y.shape
    dskip_x = jnp.repeat(d_skip, A_HEAD_DIM).reshape(1, d_inner)
    return pl.pallas_call(
        _gated_norm_kernel,
        out_shape=jax.ShapeDtypeStruct((m, d_inner), F32),
        grid=(1,),
        in_specs=[pl.BlockSpec((m, d_inner), lambda i: (0, 0)),
                  pl.BlockSpec((m, d_inner), lambda i: (0, 0)),
                  pl.BlockSpec((m, d_inner), lambda i: (0, z_col // d_inner)),
                  pl.BlockSpec((1, d_inner), lambda i: (0, 0)), pl.BlockSpec((1, d_inner), lambda i: (0, 0))],
        out_specs=pl.BlockSpec((m, d_inner), lambda i: (0, 0)),
        compiler_params=_cparams("arbitrary"),
        name="gated_norm",
    )(y, xbc, proj, dskip_x, gain.reshape(1, -1))


def _swa_seq_kernel(sink_ref, q_ref, kc_ref, kp_ref, vc_ref, vp_ref, o_ref, *, n_heads, n_kv):
    i = pl.program_id(0)
    w = WINDOW
    grp = n_heads // n_kv
    r = lax.broadcasted_iota(I32, (grp * w, 2 * w), 0) % w
    c = lax.broadcasted_iota(I32, (grp * w, 2 * w), 1)
    ok = (c > r) & (c <= r + w) & ((i > 0) | (c >= w))
    hrow = lax.broadcasted_iota(I32, (grp * w, 1), 0) // w
    scale = HEAD_DIM ** -0.5
    for kv in range(n_kv):
        sl = slice(kv * HEAD_DIM, (kv + 1) * HEAD_DIM)
        kcat = jnp.concatenate([kp_ref[:, sl], kc_ref[:, sl]], axis=0)
        vcat = jnp.concatenate([vp_ref[:, sl], vc_ref[:, sl]], axis=0)
        q4 = jnp.concatenate([q_ref[:, (kv * grp + j) * HEAD_DIM:(kv * grp + j + 1) * HEAD_DIM]
                              for j in range(grp)], axis=0)
        sink = jnp.zeros((grp * w, 1), F32)
        for j in range(grp):
            sink = jnp.where(hrow == j, sink_ref[kv * grp + j], sink)
        s = jnp.where(ok, _dot_nt(q4, kcat) * scale, -jnp.inf)
        m = jnp.maximum(jnp.max(s, axis=1, keepdims=True), sink)
        p = jnp.exp(s - m)
        denom = jnp.sum(p, axis=1, keepdims=True) + jnp.exp(sink - m)
        o = _dot(p.astype(BF16), vcat) / denom
        for j in range(grp):
            h = kv * grp + j
            o_ref[:, h * HEAD_DIM:(h + 1) * HEAD_DIM] = o[j * w:(j + 1) * w, :]


def _swa_seq(q_b, k_b, v_b, sinks, n_heads, n_kv):
    t = q_b.shape[0]
    w = WINDOW
    kvw = n_kv * HEAD_DIM
    cur = lambda i: (i, 0)
    prv = lambda i: (jnp.maximum(i - 1, 0), 0)
    return pl.pallas_call(
        functools.partial(_swa_seq_kernel, n_heads=n_heads, n_kv=n_kv),
        out_shape=jax.ShapeDtypeStruct((t, n_heads * HEAD_DIM), F32),
        grid=(t // w,),
        in_specs=[pl.BlockSpec(memory_space=pltpu.SMEM),
                  pl.BlockSpec((w, n_heads * HEAD_DIM), cur),
                  pl.BlockSpec((w, kvw), cur), pl.BlockSpec((w, kvw), prv),
                  pl.BlockSpec((w, kvw), cur), pl.BlockSpec((w, kvw), prv)],
        out_specs=pl.BlockSpec((w, n_heads * HEAD_DIM), cur),
        compiler_params=_cparams("parallel"),
        name="swa_seq",
    )(sinks, q_b, k_b, k_b, v_b, v_b)


def _swa_step_kernel(sink_ref, qm_ref, knew_ref, vnew_ref, kc_ref, vc_ref, o_ref, ko_ref, vo_ref,
                     *, n_heads, n_kv, bs):
    grp = n_heads // n_kv
    wb = kc_ref.shape[1]
    scale = HEAD_DIM ** -0.5
    hrow = lax.broadcasted_iota(I32, (n_heads, 1), 0)
    sink = jnp.zeros((n_heads, 1), F32)
    for h in range(n_heads):
        sink = jnp.where(hrow == h, sink_ref[h], sink)
    col = lax.broadcasted_iota(I32, (n_heads, wb), 1)
    ok = col > wb - WINDOW
    for bi in range(bs):
        qm = qm_ref[bi]
        kc = kc_ref[bi]
        vc = vc_ref[bi]
        knew = knew_ref[bi]
        vnew = vnew_ref[bi]
        s = jnp.where(ok, _dot_nt(qm.astype(BF16), kc.astype(BF16)) * scale, -jnp.inf)
        s_new = jnp.sum(qm * knew, axis=1, keepdims=True) * scale
        m = jnp.maximum(jnp.maximum(jnp.max(s, axis=1, keepdims=True), s_new), sink)
        p = jnp.exp(s - m)
        p_new = jnp.exp(s_new - m)
        denom = jnp.sum(p, axis=1, keepdims=True) + p_new + jnp.exp(sink - m)
        o = (_dot(p.astype(BF16), vc.astype(BF16)) + p_new * vnew) / denom
        osel = o[:, 0:HEAD_DIM]
        for kv in range(1, n_kv):
            osel = jnp.where(hrow // grp == kv, o[:, kv * HEAD_DIM:(kv + 1) * HEAD_DIM], osel)
        o_ref[bi] = osel
        ko_ref[bi, 0:wb - 1, :] = kc_ref[bi, 1:wb, :]
        ko_ref[bi, wb - 1:wb, :] = knew
        vo_ref[bi, 0:wb - 1, :] = vc_ref[bi, 1:wb, :]
        vo_ref[bi, wb - 1:wb, :] = vnew


def _swa_step(qm, knew, vnew, cache_k, cache_v, sinks, n_heads, n_kv):
    nb, wb, kvw = cache_k.shape
    bs = 8 if nb % 8 == 0 else 1
    blk = lambda shape: pl.BlockSpec((bs,) + shape, lambda b: (b,) + (0,) * len(shape))
    return pl.pallas_call(
        functools.partial(_swa_step_kernel, n_heads=n_heads, n_kv=n_kv, bs=bs),
        out_shape=(jax.ShapeDtypeStruct((nb, n_heads, HEAD_DIM), F32),
                   jax.ShapeDtypeStruct(cache_k.shape, F32), jax.ShapeDtypeStruct(cache_v.shape, F32)),
        grid=(nb // bs,),
        in_specs=[pl.BlockSpec(memory_space=pltpu.SMEM), blk((n_heads, kvw)), blk((1, kvw)), blk((1, kvw)),
                  blk((wb, kvw)), blk((wb, kvw))],
        out_specs=(blk((n_heads, HEAD_DIM)), blk((wb, kvw)), blk((wb, kvw))),
        compiler_params=_cparams("parallel"),
        name="swa_step",
    )(sinks, qm, knew, vnew, cache_k, cache_v)


def _lambda(lam_ref, lam_init):
    lp = lam_ref[...]
    return (jnp.exp(jnp.sum(lp[0:1, :] * lp[1:2, :], axis=1, keepdims=True))
            - jnp.exp(jnp.sum(lp[2:3, :] * lp[3:4, :], axis=1, keepdims=True)) + lam_init)


def _subln(o, subg, lam_init):
    return o * lax.rsqrt(jnp.mean(o * o, axis=-1, keepdims=True) + EPS) * subg * (1.0 - lam_init)


def _resident(shape):
    return pl.BlockSpec(shape, lambda *_: (0,) * len(shape), pipeline_mode=pl.Buffered(1))


def _sort_key(score):
    bits = pltpu.bitcast(score, I32)
    bits = jnp.where(score == 0.0, 0, bits)
    return jnp.where(bits < 0, bits ^ 0x7FFFFFFF, bits)


def _count_ge(s_ref, nblk, cand, shape, strict=False):
    rows, width = shape

    def body(kb, acc):
        for cg in range(width // LANES):
            key = s_ref[kb, :, cg * LANES:(cg + 1) * LANES]
            hit = (key > cand) if strict else (key >= cand)
            acc = acc + jnp.where(hit, 1.0, 0.0)
        return acc
    acc = lax.fori_loop(0, nblk, body, jnp.zeros((rows, LANES), F32))
    return jnp.sum(acc, axis=1, keepdims=True)


def _kth_largest(s_ref, nblk, n_sel, shape):
    rows = shape[0]

    def body(_, carry):
        ans, bit = carry
        cand = ans + bit
        cnt = _count_ge(s_ref, nblk, cand, shape)
        return jnp.where(cnt >= n_sel, cand, ans), lax.shift_right_logical(bit, 1)

    ans, _ = lax.fori_loop(0, 32, body, (jnp.full((rows, 1), INT_MIN, I32), jnp.int32(INT_MIN)))
    return ans


def _upper_ones(n):
    r = lax.broadcasted_iota(I32, (n, n), 0)
    c = lax.broadcasted_iota(I32, (n, n), 1)
    return (r <= c).astype(BF16)


def _select_ties(key, kth, need, carry, upper):
    eqf = jnp.where(key == kth, 1.0, 0.0)
    rank = carry + _dot(eqf.astype(BF16), upper)
    sel = (key > kth) | ((key == kth) & (rank <= need))
    return sel, carry + jnp.sum(eqf, axis=1, keepdims=True)


def _topk_mask_kernel(sc_ref, iq_ref, ik_ref, w_ref, mask_ref, mnew_ref, s_ref, *, n_idx, n_sel, bw):
    nb, t = sc_ref.shape
    nblk = t // bw
    scale = HEAD_DIM ** -0.5
    prod = iq_ref[...].astype(F32) * ik_ref[...].astype(F32)
    seg_r = lax.broadcasted_iota(I32, (prod.shape[1], LANES), 0) // HEAD_DIM
    seg_c = lax.broadcasted_iota(I32, (prod.shape[1], LANES), 1)
    qk = _dot3(prod, (seg_r == seg_c).astype(BF16))
    s_new = jnp.sum(w_ref[...] * (n_idx ** -0.5) * jnp.maximum(qk * scale, 0.0), axis=1, keepdims=True)
    lane = lax.broadcasted_iota(I32, (nb, bw), 1)
    for kb in range(nblk):
        s_ref[kb] = _sort_key(sc_ref[:, kb * bw:(kb + 1) * bw])
    s_ref[nblk] = _sort_key(jnp.where(lane == 0, s_new, -jnp.inf))
    shape = (nb, bw)
    kth = _kth_largest(s_ref, nblk + 1, n_sel, shape)
    need = n_sel - _count_ge(s_ref, nblk + 1, kth, shape, strict=True)
    upper = _upper_ones(bw)
    carry = jnp.zeros((nb, 1), F32)
    for kb in range(nblk):
        sel, carry = _select_ties(s_ref[kb], kth, need, carry, upper)
        mask_ref[:, kb * bw:(kb + 1) * bw] = jnp.where(sel, 1.0, 0.0)
    sel, _ = _select_ties(s_ref[nblk], kth, need, carry, upper)
    mnew_ref[...] = jnp.where(sel & (lane == 0), 1.0, 0.0)[:, 0:LANES]


def _topk_mask(scores, iq_b, ik_new_b, w, n_idx, n_sel):
    nb, t = scores.shape
    bw = min(256, t)
    full = lambda a: pl.BlockSpec(a.shape, lambda i: (0,) * a.ndim)
    return pl.pallas_call(
        functools.partial(_topk_mask_kernel, n_idx=n_idx, n_sel=n_sel, bw=bw),
        out_shape=(jax.ShapeDtypeStruct((nb, t), F32), jax.ShapeDtypeStruct((nb, LANES), F32)),
        grid=(1,),
        in_specs=[full(scores), full(iq_b), full(ik_new_b), full(w)],
        out_specs=(pl.BlockSpec((nb, t), lambda i: (0, 0)), pl.BlockSpec((nb, LANES), lambda i: (0, 0))),
        scratch_shapes=[pltpu.VMEM((t // bw + 1, nb, bw), I32)],
        compiler_params=_cparams("arbitrary"),
        name="topk_mask",
    )(scores, iq_b, ik_new_b, w)


LOG2E = 1.4426950408889634


def _flash_t(k_blk, qs_ref, bias, m_ref, l_ref, acc_ref, idx, v_t):
    s_t = _dot(k_blk, qs_ref[idx])
    if bias is not None:
        s_t = s_t + jnp.concatenate([bias] * (s_t.shape[1] // bias.shape[1]), axis=1)
    m_old = m_ref[idx]
    m_new = jnp.maximum(m_old, jnp.max(s_t, axis=0, keepdims=True))
    m_safe = jnp.where(m_new == -jnp.inf, 0.0, m_new)
    alpha = jnp.exp2(m_old - m_safe)
    p = jnp.exp2(s_t - m_safe)
    l_ref[idx] = alpha * l_ref[idx] + jnp.sum(p, axis=0, keepdims=True)
    acc_ref[idx] = alpha * acc_ref[idx] + _dot(v_t, p.astype(BF16))
    m_ref[idx] = m_new


def _diff_t_kernel(qt_ref, k_ref, vt_ref, lam_ref, subg_ref, o_ref, qs_ref, m_ref, l_ref, acc_ref,
                   *, n_heads, n_kv, lam_init, tq, tk):
    i = pl.program_id(0)
    grp = n_heads // n_kv
    hd = HEAD_DIM
    vd = 2 * hd
    nq = grp * tq
    m_ref[...] = jnp.full(m_ref.shape, -jnp.inf, F32)
    l_ref[...] = jnp.zeros_like(l_ref)
    acc_ref[...] = jnp.zeros_like(acc_ref)
    qs_ref[...] = jnp.zeros_like(qs_ref)
    for idx in range(2 * n_kv):
        qs_ref[idx, idx * hd:(idx + 1) * hd, :] = (qt_ref[idx] * (hd ** -0.5 * LOG2E)).astype(BF16)
    n_full = (i * tq) // tk

    def step(j, masked):
        kb = k_ref[j]
        vtb = vt_ref[j]
        bias = None
        if masked:
            kpos = j * tk + lax.broadcasted_iota(I32, (tk, tq), 0)
            qpos = i * tq + lax.broadcasted_iota(I32, (tk, tq), 1)
            bias = jnp.where(kpos <= qpos, 0.0, -jnp.inf)
        for kv in range(n_kv):
            for c in range(2):
                idx = kv * 2 + c
                _flash_t(kb, qs_ref, bias, m_ref, l_ref, acc_ref, idx, vtb[kv * vd:(kv + 1) * vd, :])

    def body(j, carry):
        step(j, False)
        return carry

    lax.fori_loop(0, n_full, body, 0)
    step(n_full, True)
    lam = _lambda(lam_ref, lam_init)
    for kv in range(n_kv):
        o0 = acc_ref[kv * 2] / l_ref[kv * 2]
        o1 = acc_ref[kv * 2 + 1] / l_ref[kv * 2 + 1]
        o = o0 - lam * o1
        o = o * lax.rsqrt(jnp.mean(o * o, axis=0, keepdims=True) + EPS) * subg_ref[...] * (1.0 - lam_init)
        for g in range(grp):
            h = kv * grp + g
            o_ref[h * vd:(h + 1) * vd, :] = o[:, g * tq:(g + 1) * tq]


def _diff_seq_t(q_f, k_b, v_b, c_lam, sub_g, lam_init, n_heads, n_kv):
    t = q_f.shape[0]
    tq = min(256, t)
    tk = min(1024, t)
    nqb, nkb = t // tq, t // tk
    hd = HEAD_DIM
    vd = 2 * hd
    grp = n_heads // n_kv
    kw = k_b.shape[1]
    qt = q_f.reshape(nqb, tq, n_kv, grp, 2, hd).transpose(2, 4, 0, 5, 3, 1).reshape(2 * n_kv, nqb, hd, grp * tq)
    vt = v_b.reshape(nkb, tk, v_b.shape[1]).transpose(0, 2, 1)
    out_t = pl.pallas_call(
        functools.partial(_diff_t_kernel, n_heads=n_heads, n_kv=n_kv, lam_init=lam_init, tq=tq, tk=tk),
        out_shape=jax.ShapeDtypeStruct((nqb, n_heads * vd, tq), F32),
        grid=(nqb,),
        in_specs=[pl.BlockSpec((2 * n_kv, None, hd, grp * tq), lambda i: (0, i, 0, 0)),
                  _resident((nkb, tk, kw)), _resident(vt.shape),
                  pl.BlockSpec(c_lam.shape, lambda i: (0, 0)),
                  pl.BlockSpec((vd, 1), lambda i: (0, 0))],
        out_specs=pl.BlockSpec((None, n_heads * vd, tq), lambda i: (i, 0, 0)),
        scratch_shapes=[pltpu.VMEM((2 * n_kv, kw, grp * tq), BF16),
                        pltpu.VMEM((2 * n_kv, 1, grp * tq), F32), pltpu.VMEM((2 * n_kv, 1, grp * tq), F32),
                        pltpu.VMEM((2 * n_kv, vd, grp * tq), F32)],
        compiler_params=_cparams("parallel"),
        name="diff_seq",
    )(qt, k_b.reshape(nkb, tk, kw), vt, c_lam, sub_g.reshape(vd, 1))
    return out_t.transpose(0, 2, 1).reshape(t, n_heads * vd)


def _count_ge_t(s_ref, nblk, cand, tk, tq, strict=False):
    rows = 8 * SUBLANES

    def body(kb, acc):
        key = s_ref[kb]
        hit = (key > cand) if strict else (key >= cand)
        return acc + jnp.sum(jnp.where(hit, 1.0, 0.0).reshape(tk // rows, rows, tq), axis=0)
    acc = lax.fori_loop(0, nblk, body, jnp.zeros((rows, tq), F32))
    return jnp.sum(acc, axis=0, keepdims=True)


def _select_ties_t(key, kth, need, carry, lower):
    eqf = jnp.where(key == kth, 1.0, 0.0)
    rank = carry + _dot(lower, eqf.astype(BF16))
    sel = (key > kth) | ((key == kth) & (rank <= need))
    return sel, carry + jnp.sum(eqf, axis=0, keepdims=True)


def _dsa_t_kernel(iqt_ref, iwt_ref, dqt_ref, ik_ref, dk_ref, dvt_ref, o_ref,
                  s_ref, iqs_ref, qs_ref, m_ref, l_ref, acc_ref,
                  *, n_heads, n_kv, n_idx, n_sel, tq, tk):
    i = pl.program_id(0)
    hd = HEAD_DIM
    grp = n_heads // n_kv
    nblk = (i * tq + tq + tk - 1) // tk
    last = nblk - 1

    iqs_ref[...] = jnp.zeros_like(iqs_ref)
    qs_ref[...] = jnp.zeros_like(qs_ref)
    for hp in range(n_idx // 2):
        for j in range(2):
            iqs_ref[hp, 0:hd, j * tq:(j + 1) * tq] = iqt_ref[2 * hp + j]
    for kv in range(n_kv):
        qs_ref[kv, kv * hd:(kv + 1) * hd, :] = (dqt_ref[kv] * (hd ** -0.5 * LOG2E)).astype(BF16)
    iw = iwt_ref[...] * ((n_idx ** -0.5) * (hd ** -0.5))

    def idx_scores(kb):
        ikb = ik_ref[kb]
        score = None
        for hp in range(n_idx // 2):
            sc = jnp.maximum(_dot(ikb, iqs_ref[hp]), 0.0)
            for j in range(2):
                h = 2 * hp + j
                term = sc[:, j * tq:(j + 1) * tq] * iw[h:h + 1, :]
                score = term if score is None else score + term
        return score

    def fill(kb, carry):
        s_ref[kb] = _sort_key(idx_scores(kb))
        return carry

    lax.fori_loop(0, last, fill, 0)
    kpos = last * tk + lax.broadcasted_iota(I32, (tk, tq), 0)
    causal_last = kpos <= i * tq + lax.broadcasted_iota(I32, (tk, tq), 1)
    s_ref[last] = _sort_key(jnp.where(causal_last, idx_scores(last), -jnp.inf))

    def bis_cond(c):
        it, _, _, _, done = c
        return jnp.logical_and(it < 32, jnp.logical_not(done))

    def bis_body(c):
        it, ans, bit, cnt_ans, _ = c
        cand = ans + bit
        cnt = _count_ge_t(s_ref, nblk, cand, tk, tq)
        take = cnt >= n_sel
        ans = jnp.where(take, cand, ans)
        cnt_ans = jnp.where(take, cnt, cnt_ans)
        done = jnp.min(jnp.where(cnt_ans == n_sel, 1.0, 0.0)) > 0.5
        return it + 1, ans, lax.shift_right_logical(bit, 1), cnt_ans, done

    init = (jnp.int32(0), jnp.full((1, tq), INT_MIN, I32), jnp.int32(INT_MIN),
            jnp.full((1, tq), 1.0, F32) * (nblk * tk).astype(F32), jnp.bool_(False))
    _, kth, _, n_ge, _ = lax.while_loop(bis_cond, bis_body, init)
    ties = jnp.max(n_ge) > n_sel

    m_ref[...] = jnp.full(m_ref.shape, -jnp.inf, F32)
    l_ref[...] = jnp.zeros_like(l_ref)
    acc_ref[...] = jnp.zeros_like(acc_ref)

    def attend(kb, sel):
        dkb = dk_ref[kb]
        dvtb = dvt_ref[kb]
        bias = jnp.where(sel, 0.0, -jnp.inf)
        for kv in range(n_kv):
            _flash_t(dkb, qs_ref, bias, m_ref, l_ref, acc_ref, kv, dvtb[kv * hd:(kv + 1) * hd, :])

    @pl.when(jnp.logical_not(ties))
    def _():
        def body(kb, carry):
            attend(kb, s_ref[kb] >= kth)
            return carry
        lax.fori_loop(0, last, body, 0)
        attend(last, (s_ref[last] >= kth) & causal_last)

    @pl.when(ties)
    def _():
        need = n_sel - _count_ge_t(s_ref, nblk, kth, tk, tq, strict=True)
        r = lax.broadcasted_iota(I32, (tk, tk), 0)
        cidx = lax.broadcasted_iota(I32, (tk, tk), 1)
        lower = (cidx <= r).astype(BF16)

        def body(kb, carry):
            sel, carry = _select_ties_t(s_ref[kb], kth, need, carry, lower)
            attend(kb, sel)
            return carry
        carry = lax.fori_loop(0, last, body, jnp.zeros((1, tq), F32))
        sel, _ = _select_ties_t(s_ref[last], kth, need, carry, lower)
        attend(last, sel & causal_last)

    for kv in range(n_kv):
        o = acc_ref[kv] / l_ref[kv]
        for g in range(grp):
            h = kv * grp + g
            o_ref[h * hd:(h + 1) * hd, :] = o[:, g * tq:(g + 1) * tq]


def _dsa_seq_t(iq_b, iw, dq_f, ik_b, dk_b, dv_b, n_heads, n_kv, n_idx, n_sel):
    t = iq_b.shape[0]
    tq = min(256, t)
    tk = min(1024, t)
    nqb, nkb = t // tq, t // tk
    hd = HEAD_DIM
    kvw = n_kv * hd
    grp = n_heads // n_kv
    assert n_idx % 2 == 0 and kvw == LANES
    iqt = iq_b.reshape(nqb, tq, n_idx, hd).transpose(2, 0, 3, 1)
    iwt = iw[:, :n_idx].reshape(nqb, tq, n_idx).transpose(0, 2, 1)
    dqt = dq_f.reshape(nqb, tq, n_kv, grp, hd).transpose(2, 0, 4, 3, 1).reshape(n_kv, nqb, hd, grp * tq)
    dvt = dv_b.reshape(nkb, tk, kvw).transpose(0, 2, 1)
    out_t = pl.pallas_call(
        functools.partial(_dsa_t_kernel, n_heads=n_heads, n_kv=n_kv, n_idx=n_idx, n_sel=n_sel, tq=tq, tk=tk),
        out_shape=jax.ShapeDtypeStruct((nqb, n_heads * hd, tq), F32),
        grid=(nqb,),
        in_specs=[pl.BlockSpec((n_idx, None, hd, tq), lambda i: (0, i, 0, 0)),
                  pl.BlockSpec((None, n_idx, tq), lambda i: (i, 0, 0)),
                  pl.BlockSpec((n_kv, None, hd, grp * tq), lambda i: (0, i, 0, 0)),
                  _resident((nkb, tk, LANES)), _resident((nkb, tk, kvw)), _resident(dvt.shape)],
        out_specs=pl.BlockSpec((None, n_heads * hd, tq), lambda i: (i, 0, 0)),
        scratch_shapes=[pltpu.VMEM((nkb, tk, tq), I32),
                        pltpu.VMEM((n_idx // 2, LANES, 2 * tq), BF16),
                        pltpu.VMEM((n_kv, kvw, grp * tq), BF16),
                        pltpu.VMEM((n_kv, 1, grp * tq), F32), pltpu.VMEM((n_kv, 1, grp * tq), F32),
                        pltpu.VMEM((n_kv, hd, grp * tq), F32)],
        compiler_params=_cparams("parallel"),
        name="dsa_seq",
    )(iqt, iwt, dqt, ik_b.reshape(nkb, tk, LANES), dk_b.reshape(nkb, tk, kvw), dvt)
    return out_t.transpose(0, 2, 1).reshape(t, n_heads * hd)


def _page_copies(pt_ref, step, slot, caches, bufs, sem, *, nc, n_pages, ch, layer):
    b = step // nc
    c = step % nc
    out = []
    for a, (cache, buf) in enumerate(zip(caches, bufs)):
        for p in range(ch):
            page = pt_ref[b * n_pages + c * ch + p]
            out.append(pltpu.make_async_copy(cache.at[layer, page], buf.at[slot, p], sem.at[a, slot]))
    return out


PAGE_RING = 3


def _pipelined_pages(pt_ref, caches, bufs, sem, *, nc, n_pages, ch, layer):
    step = pl.program_id(0) * nc + pl.program_id(1)
    total = pl.num_programs(0) * nc
    ahead = PAGE_RING - 1
    kw = dict(nc=nc, n_pages=n_pages, ch=ch, layer=layer)

    def start(s):
        for cp in _page_copies(pt_ref, s, s % PAGE_RING, caches, bufs, sem, **kw):
            cp.start()

    for k in range(ahead):
        @pl.when(jnp.logical_and(step == 0, k < total))
        def _(k=k):
            start(step + k)

    @pl.when(step + ahead < total)
    def _():
        start(step + ahead)

    slot = step % PAGE_RING
    for cp in _page_copies(pt_ref, step, slot, caches, bufs, sem, **kw):
        cp.wait()
    return slot


def _paged3_kernel(pt_ref, *refs, ch, nc, n_pages, layer, mode, has_mask, lam_init, n_heads, n_kv):
    qm_ref, knew_ref, vnew_ref, selnew_ref = refs[0:4]
    pos = 4
    mask_ref = None
    if has_mask:
        mask_ref = refs[pos]
        pos += 1
    lam_ref = subg_ref = None
    if mode == "diff":
        lam_ref, subg_ref = refs[pos], refs[pos + 1]
        pos += 2
    ck_ref, cv_ref, o_ref, kbuf, vbuf, sem, m_ref, l_ref, acc_ref = refs[pos:pos + 9]
    c = pl.program_id(1)
    scale = HEAD_DIM ** -0.5
    dk = qm_ref.shape[2]
    slot = _pipelined_pages(pt_ref, (ck_ref, cv_ref), (kbuf, vbuf), sem, nc=nc, n_pages=n_pages, ch=ch, layer=layer)

    @pl.when(c == 0)
    def _():
        m_ref[...] = jnp.full(m_ref.shape, -jnp.inf, F32)
        l_ref[...] = jnp.zeros_like(l_ref)
        acc_ref[...] = jnp.zeros_like(acc_ref)

    qm = qm_ref[0]
    qb = qm.astype(BF16)
    s = jnp.concatenate([_dot(qb, kbuf[slot, p].reshape(dk, PAGE).astype(BF16)) for p in range(ch)],
                        axis=1) * scale
    if has_mask:
        s = jnp.where(mask_ref[0] > 0.0, s, -jnp.inf)
    m_old = m_ref[...]
    m_new = jnp.maximum(m_old, jnp.max(s, axis=1, keepdims=True))
    m_safe = jnp.where(m_new == -jnp.inf, 0.0, m_new)
    alpha = jnp.exp(m_old - m_safe)
    p = jnp.exp(s - m_safe).astype(BF16)
    l_ref[...] = alpha * l_ref[...] + jnp.sum(p.astype(F32), axis=1, keepdims=True)

    def pv_of(j):
        pj = p[:, j * PAGE:(j + 1) * PAGE]
        if mode == "diff":
            return jnp.concatenate(
                [_dot(pj, vbuf[slot, j, pl.ds(kv, PAGE, stride=n_kv), :].astype(BF16)) for kv in range(n_kv)],
                axis=1)
        return _dot_nt(pj, vbuf[slot, j].reshape(-1, PAGE).astype(BF16))

    pv = pv_of(0)
    for j in range(1, ch):
        pv = pv + pv_of(j)
    acc_ref[...] = alpha * acc_ref[...] + pv
    m_ref[...] = m_new

    @pl.when(c == nc - 1)
    def _():
        s_new = jnp.sum(qm * knew_ref[0], axis=1, keepdims=True) * scale
        s_new = jnp.where(selnew_ref[0][:, 0:1] > 0.0, s_new, -jnp.inf)
        m_o = m_ref[...]
        m_f = jnp.maximum(m_o, s_new)
        m_s = jnp.where(m_f == -jnp.inf, 0.0, m_f)
        al = jnp.exp(m_o - m_s)
        p_new = jnp.exp(s_new - m_s)
        l = al * l_ref[...] + p_new
        o = (al * acc_ref[...] + p_new * vnew_ref[0]) / l
        rows = o.shape[0]
        r = lax.broadcasted_iota(I32, (rows, 1), 0)
        grp = n_heads // n_kv
        if mode == "diff":
            vd = 2 * HEAD_DIM
            kv_of = (r % n_heads) // grp
            osel = o[:, 0:vd]
            for kv in range(1, n_kv):
                osel = jnp.where(kv_of == kv, o[:, kv * vd:(kv + 1) * vd], osel)
            lam = _lambda(lam_ref, lam_init)
            od = osel[0:n_heads, :] - lam * osel[n_heads:2 * n_heads, :]
            o_ref[0] = _subln(od, subg_ref[...], lam_init)
        else:
            kv_of = r // grp
            osel = o[:, 0:HEAD_DIM]
            for kv in range(1, n_kv):
                osel = jnp.where(kv_of == kv, o[:, kv * HEAD_DIM:(kv + 1) * HEAD_DIM], osel)
            o_ref[0] = osel


def _paged_attn3(page_table, layer, qm, knew, vnew, selnew, mask, cache_kt, cache_v, *, mode, n_heads, n_kv,
                 lam=None, sub_g=None, lam_init=0.0):
    nb, n_pages = page_table.shape
    rows, dk = qm.shape[1:]
    dv = vnew.shape[-1]
    ch = min(16, n_pages)
    nc = n_pages // ch
    has_mask = mask is not None
    per_b = lambda shape: pl.BlockSpec((1,) + shape, lambda b, c, pt: (b,) + (0,) * len(shape))
    in_specs = [per_b((rows, dk)), per_b((1, dk)), per_b((1, dv)), per_b((1, LANES))]
    args = [qm, knew, vnew, selnew]
    if has_mask:
        in_specs.append(pl.BlockSpec((1, 1, ch * PAGE), lambda b, c, pt: (b, 0, c)))
        args.append(mask)
    if mode == "diff":
        in_specs += [pl.BlockSpec(lam.shape, lambda b, c, pt: (0, 0)),
                     pl.BlockSpec((1, 2 * HEAD_DIM), lambda b, c, pt: (0, 0))]
        args += [lam, sub_g.reshape(1, -1)]
        out_w = 2 * HEAD_DIM
    else:
        out_w = HEAD_DIM
    in_specs += [pl.BlockSpec(memory_space=pl.ANY), pl.BlockSpec(memory_space=pl.ANY)]
    args += [cache_kt, cache_v]
    grid_spec = pltpu.PrefetchScalarGridSpec(
        num_scalar_prefetch=1, grid=(nb, nc), in_specs=in_specs,
        out_specs=pl.BlockSpec((1, n_heads, out_w), lambda b, c, pt: (b, 0, 0)),
        scratch_shapes=[pltpu.VMEM((PAGE_RING, ch) + cache_kt.shape[2:], F32),
                        pltpu.VMEM((PAGE_RING, ch) + cache_v.shape[2:], F32),
                        pltpu.SemaphoreType.DMA((2, PAGE_RING)),
                        pltpu.VMEM((rows, 1), F32), pltpu.VMEM((rows, 1), F32), pltpu.VMEM((rows, dv), F32)])
    return pl.pallas_call(
        functools.partial(_paged3_kernel, ch=ch, nc=nc, n_pages=n_pages, layer=layer, mode=mode, has_mask=has_mask,
                          lam_init=lam_init, n_heads=n_heads, n_kv=n_kv),
        out_shape=jax.ShapeDtypeStruct((nb, n_heads, out_w), F32),
        grid_spec=grid_spec,
        compiler_params=_cparams("arbitrary", "arbitrary"),
        name="paged_attn_" + mode,
    )(page_table.reshape(-1), *args)


def _paged_idx3_kernel(pt_ref, iq_ref, w_ref, ci_ref, o_ref, ibuf, sem, *, ch, nc, n_pages, layer, n_idx):
    slot = _pipelined_pages(pt_ref, (ci_ref,), (ibuf,), sem, nc=nc, n_pages=n_pages, ch=ch, layer=layer)
    scale = HEAD_DIM ** -0.5
    iq = iq_ref[0]
    w = w_ref[0] * (n_idx ** -0.5)
    outs = []
    for p in range(ch):
        sc = jnp.maximum(_dot(iq, ibuf[slot, p].astype(BF16)) * scale, 0.0)
        outs.append(jnp.sum(w * sc, axis=0, keepdims=True))
    o_ref[0] = jnp.concatenate(outs, axis=1)


def _paged_idx_scores3(page_table, layer, iq, w, cache_it, n_idx):
    nb, n_pages = page_table.shape
    ch = min(64, n_pages)
    nc = n_pages // ch
    grid_spec = pltpu.PrefetchScalarGridSpec(
        num_scalar_prefetch=1, grid=(nb, nc),
        in_specs=[pl.BlockSpec((1, n_idx, HEAD_DIM), lambda b, c, pt: (b, 0, 0)),
                  pl.BlockSpec((1, n_idx, 1), lambda b, c, pt: (b, 0, 0)),
                  pl.BlockSpec(memory_space=pl.ANY)],
        out_specs=pl.BlockSpec((1, 1, ch * PAGE), lambda b, c, pt: (b, 0, c)),
        scratch_shapes=[pltpu.VMEM((PAGE_RING, ch) + cache_it.shape[2:], F32),
                        pltpu.SemaphoreType.DMA((1, PAGE_RING))])
    return pl.pallas_call(
        functools.partial(_paged_idx3_kernel, ch=ch, nc=nc, n_pages=n_pages, layer=layer, n_idx=n_idx),
        out_shape=jax.ShapeDtypeStruct((nb, 1, n_pages * PAGE), F32),
        grid_spec=grid_spec,
        compiler_params=_cparams("arbitrary", "arbitrary"),
        name="paged_idx_scores",
    )(page_table.reshape(-1), iq, w, cache_it)


def _bf(w):
    return w.astype(BF16)


def _place(q3, slot_of_row, n_slots):
    onehot = (np.asarray(slot_of_row)[:, None] == np.arange(n_slots)[None, :]).astype(np.float32)
    out = q3[:, :, None, :] * jnp.asarray(onehot)[None, :, :, None]
    return out.reshape(q3.shape[0], q3.shape[1], n_slots * q3.shape[2])


def _even_weights(w_in, d_inner, conv_ch, n_a_heads, qw, kvw):
    c = np.cumsum([0, d_inner, conv_ch, n_a_heads, qw, kvw, kvw])
    z, xbc, dt, q, k, v = (w_in[:, c[j]:c[j + 1]] for j in range(6))
    dt = jnp.pad(dt, ((0, 0), (0, LANES - n_a_heads)))
    w = _bf(jnp.concatenate([z, xbc, q, k, v, dt], axis=1))
    off = np.cumsum([0, d_inner, conv_ch, qw, kvw, kvw])
    return w, dict(z=int(off[0]), xbc=int(off[1]), q=int(off[2]), k=int(off[3]), v=int(off[4]), dt=int(off[5]))


def _odd_weights(w_in, sizes):
    c = np.cumsum([0] + list(sizes))
    cq, ck, cv, dq, dk, dv, iq, iw, ik = (w_in[:, c[j]:c[j + 1]] for j in range(9))
    ik = jnp.pad(ik, ((0, 0), (0, LANES - ik.shape[1])))
    iw = jnp.pad(iw, ((0, 0), (0, LANES - iw.shape[1])))
    parts = [cq, dq, iq, ck, cv, dk, dv, ik, iw]
    off = np.cumsum([0] + [p.shape[1] for p in parts])
    names = ["cq", "dq", "iq", "ck", "cv", "dk", "dv", "ik", "iw"]
    return _bf(jnp.concatenate(parts, axis=1)), {n: int(o) for n, o in zip(names, off[:-1])}


def _mixer_even(x, pos, seq_mode, st_conv, st_ssm, win_k, win_v, norm_g, w_in, conv_w, conv_b, dt_bias, a_log,
                d_skip, gain, qn_g, kn_g, sinks, w_out):
    m, _ = x.shape
    n_a_heads = a_log.shape[0]
    d_inner = gain.shape[0]
    conv_ch = conv_w.shape[1]
    n_heads = sinks.shape[0]
    qw = n_heads * HEAD_DIM
    kvw = (w_in.shape[1] - d_inner - conv_ch - n_a_heads - qw) // 2
    n_kv = kvw // HEAD_DIM
    kw = conv_w.shape[0]
    gn = A_GROUPS * A_STATE
    hpg = n_a_heads // A_GROUPS
    w, off = _even_weights(w_in, d_inner, conv_ch, n_a_heads, qw, kvw)
    cos, sin = _rope_tables(pos)
    proj = _mm([x], [w], norm_g=norm_g, name="in_proj_even")
    q_f, q_b = _norm_rope(proj, off["q"], qw, qn_g, cos, sin, name="swa_q_rope")
    k_f, k_b = _norm_rope(proj, off["k"], kvw, kn_g, cos, sin, name="swa_k_rope")
    v_f = proj[:, off["v"]:off["v"] + kvw]
    xbc_raw = proj[:, off["xbc"]:off["xbc"] + conv_ch]
    if seq_mode:
        state8 = jnp.zeros((SUBLANES, conv_ch), F32)
        xbc = _conv_seq(proj, off["xbc"], conv_ch, state8, conv_w, conv_b)
        s0_t = jnp.zeros((A_GROUPS, A_STATE, hpg * A_HEAD_DIM), F32)
        ya, st = _ssd_seq(xbc, proj, off["z"], off["dt"], dt_bias, a_log, d_skip, gain, s0_t, n_a_heads)
        ssm_new = st.reshape(A_GROUPS, A_STATE, hpg, A_HEAD_DIM).transpose(0, 2, 3, 1).reshape(
            1, n_a_heads, A_HEAD_DIM, A_STATE)
        conv_new = xbc_raw[m - (kw - 1):][None]
        ob = _swa_seq(q_b, k_b, _bf(v_f), sinks, n_heads, n_kv)
        wb = min(WINDOW, m)
        new_k = k_f[m - wb:].reshape(1, wb, n_kv, HEAD_DIM)
        new_v = v_f[m - wb:].reshape(1, wb, n_kv, HEAD_DIM)
    else:
        xbc = _conv_step(proj, off["xbc"], conv_ch, st_conv.transpose(1, 0, 2), conv_w, conv_b)
        xdt, dec = _ssd_step_pre(xbc, proj, off["dt"], dt_bias, a_log, d_inner)
        ssm_new, y = _ssd_step(st_ssm, xdt.T, dec.T, xbc[:, d_inner:d_inner + gn], xbc[:, d_inner + gn:])
        ya = _gated_norm_call(y, xbc, proj, off["z"], d_skip, gain)
        conv_new = jnp.concatenate([st_conv[:, 1:], xbc_raw[:, None, :]], axis=1)
        wb = win_k.shape[1]
        grp = n_heads // n_kv
        qm = _place(q_f.reshape(m, n_heads, HEAD_DIM), [h // grp for h in range(n_heads)], n_kv)
        o, new_k, new_v = _swa_step(qm, k_f[:, None, :], v_f[:, None, :], win_k.reshape(m, wb, kvw),
                                    win_v.reshape(m, wb, kvw), sinks, n_heads, n_kv)
        ob = o.reshape(m, qw)
        new_k = new_k.reshape(m, wb, n_kv, HEAD_DIM)
        new_v = new_v.reshape(m, wb, n_kv, HEAD_DIM)
    wo = _bf(w_out)
    y = _mm([ya, ob], [wo[:d_inner], wo[d_inner:]], res=x, name="out_proj_even")
    return y, (conv_new, ssm_new, new_k, new_v)


def _mixer_odd(x, pos, seq_mode, paged, norm_g, w_in, qn_g, kn_g, lam_p, sub_g, dqn_g, dkn_g, w_out, lam_init,
               sizes, n_sel):
    m, _ = x.shape
    hd = HEAD_DIM
    c_heads = sizes[0] // (2 * hd)
    c_kv = sizes[1] // (2 * hd)
    d_heads = sizes[3] // hd
    d_kv = sizes[4] // hd
    n_idx = sizes[7]
    w, off = _odd_weights(w_in, sizes)
    cos, sin = _rope_tables(pos)
    proj = _mm([x], [w], norm_g=norm_g, name="in_proj_odd")
    cq_f, cq_b = _norm_rope(proj, off["cq"], sizes[0], qn_g, cos, sin, name="diff_q_rope")
    ck_f, ck_b = _norm_rope(proj, off["ck"], sizes[1], kn_g, cos, sin, name="diff_k_rope")
    dq_f, dq_b = _norm_rope(proj, off["dq"], sizes[3], dqn_g, cos, sin, name="dsa_q_rope")
    dk_f, dk_b = _norm_rope(proj, off["dk"], sizes[4], dkn_g, cos, sin, name="dsa_k_rope")
    iq_f, iq_b = _norm_rope(proj, off["iq"], sizes[6], None, cos, sin, name="idx_q_rope")
    ik_f, ik_b = _norm_rope(proj, off["ik"], LANES, None, cos, sin, name="idx_k_rope")
    cv_f = proj[:, off["cv"]:off["cv"] + sizes[2]]
    dv_f = proj[:, off["dv"]:off["dv"] + sizes[5]]
    iw = proj[:, off["iw"]:off["iw"] + LANES]
    if seq_mode:
        oc = _diff_seq_t(cq_f, ck_b, _bf(cv_f), lam_p, sub_g, lam_init, c_heads, c_kv)
        od = _dsa_seq_t(iq_b, iw, dq_f, ik_b, dk_b, _bf(dv_f), d_heads, d_kv, n_idx, n_sel)
    else:
        c_k, c_v, d_k, d_v, d_i, table, layer = paged
        pool = c_k.shape[1]
        ones = jnp.ones((m, 1, LANES), F32)
        grp = c_heads // c_kv
        q4 = cq_f.reshape(m, c_heads, 2, hd).transpose(0, 2, 1, 3).reshape(m, 2 * c_heads, hd)
        slots = [(h // grp) * 2 + c for c in range(2) for h in range(c_heads)]
        qm_c = _place(q4, slots, 2 * c_kv)
        oc = _paged_attn3(table, layer, qm_c, ck_f[:, None, :], cv_f[:, None, :], ones, None,
                          jnp.transpose(c_k, (0, 1, 3, 4, 5, 2)), c_v.reshape(c_v.shape[0], pool, PAGE * c_kv, -1),
                          mode="diff", n_heads=c_heads, n_kv=c_kv, lam=lam_p, sub_g=sub_g, lam_init=lam_init)
        oc = oc.reshape(m, -1)
        scores = _paged_idx_scores3(table, layer, iq_b.reshape(m, n_idx, hd), iw[:, :n_idx, None],
                                    jnp.transpose(d_i, (0, 1, 3, 2)), n_idx)
        ik_tiled = jnp.tile(ik_b[:, :hd], (1, n_idx))
        mask, mnew = _topk_mask(scores.reshape(m, -1), iq_b, ik_tiled, iw, n_idx, n_sel)
        dgrp = d_heads // d_kv
        qm_d = _place(dq_f.reshape(m, d_heads, hd), [h // dgrp for h in range(d_heads)], d_kv)
        od = _paged_attn3(table, layer, qm_d, dk_f[:, None, :], dv_f[:, None, :], mnew[:, None, :], mask[:, None, :],
                          jnp.transpose(d_k, (0, 1, 3, 4, 2)), jnp.transpose(d_v, (0, 1, 3, 4, 2)),
                          mode="gqa", n_heads=d_heads, n_kv=d_kv)
        od = od.reshape(m, -1)
    wo = _bf(w_out)
    y = _mm([oc, od], [wo[:oc.shape[1]], wo[oc.shape[1]:]], res=x, name="out_proj_odd")
    lead = (1, m) if seq_mode else (m, 1)
    caches = (ck_f.reshape(lead + (c_kv, 2, hd)), cv_f.reshape(lead + (c_kv, 2 * hd)),
              dk_f.reshape(lead + (d_kv, hd)), dv_f.reshape(lead + (d_kv, hd)), ik_f[:, :hd].reshape(lead + (hd,)))
    return y, caches


def kernel(x_prompt, x_sample, state_ssm, state_ssm_conv, cache_swa_k, cache_swa_v, cache_c_k, cache_c_v, cache_d_k, cache_d_v, cache_d_idx, state_ffn_conv, page_table, norm_mix_g, norm_ffn_g, a_w_in, a_conv_w, a_conv_b, a_dt_bias, a_A_log, a_D, a_norm_g, b_qn_g, b_kn_g, b_sinks, e_w_out, m_w_in, c_qn_g, c_kn_g, c_lam, c_subln_g, d_qn_g, d_kn_g, m_w_out, ffn_w_gate, ffn_w_up, ffn_conv_w, ffn_conv_b, ffn_w_down):
    bp, seq, d_model = x_prompt.shape
    nb = x_sample.shape[0]
    assert bp == 1 and x_sample.shape[1] == 1
    depth = norm_mix_g.shape[0]
    d_ff = ffn_w_gate.shape[2]
    past = page_table.shape[1] * PAGE
    xp = x_prompt.reshape(seq, d_model)
    xs = x_sample.reshape(nb, d_model)
    pos_p = jnp.arange(seq)
    pos_s = jnp.full((nb,), past, I32)
    hd = HEAD_DIM
    c_kv, d_kv, idx_dim = cache_c_k.shape[3], cache_d_k.shape[3], cache_d_idx.shape[3]
    d_heads = d_model // 128
    c_heads = d_model // 256
    n_idx = m_w_in.shape[2] - (c_heads * 2 * hd + 2 * c_kv * 2 * hd + d_heads * hd + 2 * d_kv * hd
                               + d_heads * hd + idx_dim)
    odd_sizes = (c_heads * 2 * hd, c_kv * 2 * hd, c_kv * 2 * hd, d_heads * hd, d_kv * hd, d_kv * hd,
                 d_heads * hd, n_idx, idx_dim)
    outs_p = {k: [] for k in ("ssm", "cnv", "swk", "swv", "ck", "cv", "dk", "dv", "di", "fc")}
    outs_s = {k: [] for k in outs_p}
    for i in range(depth):
        if i % 2 == 0:
            e = i // 2
            wts = (norm_mix_g[i], a_w_in[e], a_conv_w[e], a_conv_b[e], a_dt_bias[e], a_A_log[e], a_D[e],
                   a_norm_g[e], b_qn_g[e], b_kn_g[e], b_sinks[e], e_w_out[e])
            xp, (c1, s1, k1, v1) = _mixer_even(xp, pos_p, True, None, None, None, None, *wts)
            xs, (c2, s2, k2, v2) = _mixer_even(xs, pos_s, False, state_ssm_conv[e], state_ssm[e],
                                               cache_swa_k[e], cache_swa_v[e], *wts)
            for d, vals in ((outs_p, (c1, s1, k1, v1)), (outs_s, (c2, s2, k2, v2))):
                for key, val in zip(("cnv", "ssm", "swk", "swv"), vals):
                    d[key].append(val)
        else:
            o = i // 2
            lam_init = 0.8 - 0.6 * math.exp(-0.3 * i)
            wts = (norm_mix_g[i], m_w_in[o], c_qn_g[o], c_kn_g[o], c_lam[o], c_subln_g[o], d_qn_g[o], d_kn_g[o],
                   m_w_out[o], lam_init, odd_sizes)
            xp, cp = _mixer_odd(xp, pos_p, True, None, *wts, min(256, seq // 4))
            xs, cs = _mixer_odd(xs, pos_s, False,
                                (cache_c_k, cache_c_v, cache_d_k, cache_d_v, cache_d_idx, page_table, o),
                                *wts, min(256, (past + 1) // 4))
            for d, vals in ((outs_p, cp), (outs_s, cs)):
                for key, val in zip(("ck", "cv", "dk", "dv", "di"), vals):
                    d[key].append(val)
        fw = (norm_ffn_g[i], _bf(ffn_w_gate[i]), _bf(ffn_w_up[i]), ffn_conv_w[i], ffn_conv_b[i], _bf(ffn_w_down[i]))
        zrow = jnp.zeros((1, d_ff), F32)
        xp, gp = _ffn(xp, zrow, zrow, *fw, seq_mode=True)
        outs_p["fc"].append(gp[gp.shape[0] - (ffn_conv_w.shape[1] - 1):][None])
        st = state_ffn_conv[i]
        xs, gs = _ffn(xs, st[:, 0, :], st[:, 1, :], *fw, seq_mode=False)
        outs_s["fc"].append(jnp.stack([st[:, 1, :], gs], axis=1))
    order = ("ssm", "cnv", "swk", "swv", "ck", "cv", "dk", "dv", "di", "fc")
    return ((xp.reshape(1, seq, d_model), xs.reshape(nb, 1, d_model))
            + tuple(jnp.stack(outs_p[k]) for k in order) + tuple(jnp.stack(outs_s[k]) for k in order))
```
